```python
import jax
import jax.numpy as jnp
from jax import lax
import numpy as np

D_MODEL = 2048
BATCH = 2
SEQ = 16384
DEPTH = 2

N_A_LAYERS = DEPTH // 2
N_B_LAYERS = DEPTH - N_A_LAYERS
RMS_EPS = 1e-6

RET_HEADS = 8
RET_QK_DIM = D_MODEL // RET_HEADS
RET_V_DIM = 2 * RET_QK_DIM
RET_CHUNK = 128
ROPE_BASE = 10000.0
RET_PROJ = 2 * RET_HEADS * RET_QK_DIM + 2 * RET_HEADS * RET_V_DIM

NSA_Q_HEADS = 16
NSA_KV_HEADS = 4
NSA_GROUP = NSA_Q_HEADS // NSA_KV_HEADS
NSA_HEAD_DIM = 128
N_BRANCH = 3
CMP_BLOCK = 32
CMP_STRIDE = 16
CMP_HIDDEN = 256
SLC_BLOCK = 64
SLC_TOPK = 16
WIN = 512
NSA_Q_BLOCK = 64
CMP_RATIO = CMP_BLOCK // CMP_STRIDE
SLC_RATIO = SLC_BLOCK // CMP_STRIDE
NSA_Q_PROJ = NSA_Q_HEADS * NSA_HEAD_DIM + N_BRANCH * NSA_Q_HEADS
NSA_KV_PROJ = 2 * N_BRANCH * NSA_KV_HEADS * NSA_HEAD_DIM
NEG = -1e30
SEL_FORCE = 1e30

D_FF = 4096
CONV_W = 3

kernel_name = "retention_nsa_yoco_convffn"


def rmsnorm(x, gain):
    xf = x.astype(jnp.float32)
    y = xf * lax.rsqrt(jnp.mean(xf * xf, axis=-1, keepdims=True) + RMS_EPS)
    return (y * gain.astype(jnp.float32)).astype(x.dtype)


def rotary(x, pos):
    half = x.shape[-1] // 2
    freqs = ROPE_BASE ** (-jnp.arange(half, dtype=jnp.float32) / half)
    ang = pos[:, None] * freqs[None, :]
    cos = jnp.cos(ang)[None, :, None, :]
    sin = jnp.sin(ang)[None, :, None, :]
    x1 = x[..., :half].astype(jnp.float32)
    x2 = x[..., half:].astype(jnp.float32)
    return jnp.concatenate([x1 * cos - x2 * sin, x1 * sin + x2 * cos], axis=-1).astype(x.dtype)


def retention(xn, w_in, gn_gain, w_out):
    B, S, _ = xn.shape
    H, dk, dv, C = RET_HEADS, RET_QK_DIM, RET_V_DIM, RET_CHUNK
    nC = S // C
    proj = xn @ w_in
    q, k, v, g = jnp.split(proj, [H * dk, 2 * H * dk, 2 * H * dk + H * dv], axis=-1)
    pos = jnp.arange(S, dtype=jnp.float32)
    q = rotary(q.reshape(B, S, H, dk), pos)
    k = rotary(k.reshape(B, S, H, dk), pos) * (dk ** -0.5)
    v = v.reshape(B, S, H, dv)

    def chunks(t):
        return t.reshape(B, nC, C, H, t.shape[-1]).transpose(0, 3, 1, 2, 4)

    qc, kc, vc = chunks(q), chunks(k), chunks(v)
    log_gamma = jnp.log(1.0 - 2.0 ** (-5.0 - jnp.arange(H, dtype=jnp.float32)))
    idx = jnp.arange(C, dtype=jnp.float32)
    diff = idx[:, None] - idx[None, :]
    decay_in = jnp.where(diff >= 0, jnp.exp(jnp.maximum(diff, 0.0)[None] * log_gamma[:, None, None]), 0.0)
    scores = jnp.einsum('bhnqd,bhnkd->bhnqk', qc, kc) * decay_in[None, :, None]
    inner = jnp.einsum('bhnqk,bhnke->bhnqe', scores, vc)

    q_decay = jnp.exp((idx[None, :] + 1.0) * log_gamma[:, None])
    k_decay = jnp.exp((C - 1.0 - idx[None, :]) * log_gamma[:, None])
    chunk_decay = jnp.exp(C * log_gamma)

    def step(state, inp):
        q_i, k_i, v_i = inp
        cross = jnp.einsum('bhqd,bhde->bhqe', q_i, state) * q_decay[None, :, :, None]
        state = state * chunk_decay[None, :, None, None] + jnp.einsum(
            'bhkd,bhke->bhde', k_i * k_decay[None, :, :, None], v_i)
        return state, cross

    state0 = jnp.zeros((B, H, dk, dv), jnp.float32)
    xs = (qc.transpose(2, 0, 1, 3, 4), kc.transpose(2, 0, 1, 3, 4), vc.transpose(2, 0, 1, 3, 4))
    _, cross = lax.scan(step, state0, xs)
    out = inner + cross.transpose(1, 2, 0, 3, 4)
    out = out.transpose(0, 2, 3, 1, 4).reshape(B, S, H, dv).astype(jnp.float32)
    mu = jnp.mean(out, axis=-1, keepdims=True)
    var = jnp.mean(jnp.square(out - mu), axis=-1, keepdims=True)
    out = ((out - mu) * lax.rsqrt(var + RMS_EPS)).reshape(B, S, H * dv) * gn_gain.astype(jnp.float32)
    y = jax.nn.silu(g.astype(jnp.float32)) * out
    return y.astype(xn.dtype) @ w_out


def conv_ffn(xn, w_in, conv_w, conv_b, w_out):
    S = xn.shape[1]
    u = xn @ w_in
    up = jnp.pad(u, ((0, 0), (CONV_W - 1, 0), (0, 0)))
    c = conv_b
    for tap in range(CONV_W):
        c = c + up[:, tap:tap + S] * conv_w[tap]
    a, b = jnp.split(c, 2, axis=-1)
    return (jax.nn.silu(a) * b) @ w_out


def compress_blocks(t, pe, w1, w2):
    B, S, H, d = t.shape
    n = S // CMP_STRIDE
    n_cmp = n - CMP_RATIO + 1
    ch = t.reshape(B, n, CMP_STRIDE, H, d)
    blocks = jnp.concatenate([ch[:, r:r + n_cmp] for r in range(CMP_RATIO)], axis=2)
    blocks = blocks + pe[None, None, :, None, :]
    blocks = blocks.transpose(0, 1, 3, 2, 4).reshape(B, n_cmp, H, CMP_BLOCK * d)
    return jax.nn.gelu(blocks @ w1) @ w2


def nsa_shared_kv(hn, w_kv, cmp_pe_k, cmp_w1_k, cmp_w2_k, cmp_pe_v, cmp_w1_v, cmp_w2_v):
    B, S, _ = hn.shape
    kv = (hn @ w_kv).reshape(B, S, 2 * N_BRANCH, NSA_KV_HEADS, NSA_HEAD_DIM)
    k_cmp = compress_blocks(kv[:, :, 0], cmp_pe_k, cmp_w1_k, cmp_w2_k)
    v_cmp = compress_blocks(kv[:, :, 1], cmp_pe_v, cmp_w1_v, cmp_w2_v)
    return (k_cmp, v_cmp, kv[:, :, 2], kv[:, :, 3], kv[:, :, 4], kv[:, :, 5])


def nsa_attention(xn, w_q, w_o, k_cmp, v_cmp, k_slc, v_slc, k_win, v_win):
    B, S, _ = xn.shape
    Hkv, G, d, Qb = NSA_KV_HEADS, NSA_GROUP, NSA_HEAD_DIM, NSA_Q_BLOCK
    proj = xn @ w_q
    q = proj[..., :NSA_Q_HEADS * d].reshape(B, S, Hkv, G, d) * (d ** -0.5)
    gates = jax.nn.sigmoid(proj[..., NSA_Q_HEADS * d:].astype(jnp.float32)).reshape(B, S, Hkv, G, N_BRANCH)
    n_cmp = k_cmp.shape[1]
    n_sel = S // SLC_BLOCK
    top_k = min(SLC_TOPK, n_sel)
    cmp_end = jnp.arange(n_cmp) * CMP_STRIDE + CMP_BLOCK - 1
    ks_blk = k_slc.reshape(B, n_sel, SLC_BLOCK, Hkv, d).transpose(0, 3, 1, 2, 4)
    vs_blk = v_slc.reshape(B, n_sel, SLC_BLOCK, Hkv, d).transpose(0, 3, 1, 2, 4)
    kw_pad = jnp.pad(k_win, ((0, 0), (WIN, 0), (0, 0), (0, 0)))
    vw_pad = jnp.pad(v_win, ((0, 0), (WIN, 0), (0, 0), (0, 0)))
    bi = jnp.arange(B)[:, None, None, None]
    hi = jnp.arange(Hkv)[None, None, :, None]
    j_sel = jnp.arange(n_sel)

    def block(start):
        qb = lax.dynamic_slice_in_dim(q, start, Qb, axis=1)
        gb = lax.dynamic_slice_in_dim(gates, start, Qb, axis=1)
        t = start + jnp.arange(Qb)
        s = jnp.einsum('bqhgd,bchd->bhgqc', qb, k_cmp).astype(jnp.float32)
        m_cmp = cmp_end[None, :] <= t[:, None]
        p_cmp = jax.nn.softmax(jnp.where(m_cmp, s, NEG), axis=-1) * m_cmp
        o_cmp = jnp.einsum('bhgqc,bchd->bqhgd', p_cmp.astype(v_cmp.dtype), v_cmp)
        imp = p_cmp.sum(axis=2)
        imp = jnp.pad(imp, ((0, 0), (0, 0), (0, 0), (CMP_RATIO - 1, CMP_RATIO - 1)))
        p_slc = imp[..., 0:SLC_RATIO * n_sel:SLC_RATIO]
        for r in range(1, SLC_RATIO + CMP_RATIO - 1):
            p_slc = p_slc + imp[..., r:r + SLC_RATIO * n_sel:SLC_RATIO]
        cur = t // SLC_BLOCK
        forced = (j_sel[None, :] == 0) | (j_sel[None, :] == cur[:, None]) | (j_sel[None, :] == cur[:, None] - 1)
        valid = j_sel[None, :] <= cur[:, None]
        sel_score = jnp.where(forced, SEL_FORCE, jnp.where(valid, p_slc, NEG))
        _, idx = lax.top_k(sel_score.transpose(0, 2, 1, 3), top_k)
        kg = ks_blk[bi, hi, idx].reshape(B, Qb, Hkv, top_k * SLC_BLOCK, d)
        vg = vs_blk[bi, hi, idx].reshape(B, Qb, Hkv, top_k * SLC_BLOCK, d)
        tok = (idx[..., None] * SLC_BLOCK + jnp.arange(SLC_BLOCK)).reshape(B, Qb, Hkv, top_k * SLC_BLOCK)
        m_slc = (tok <= t[None, :, None, None])[:, :, :, None, :]
        s = jnp.einsum('bqhgd,bqhkd->bqhgk', qb, kg).astype(jnp.float32)
        p = jax.nn.softmax(jnp.where(m_slc, s, NEG), axis=-1)
        o_slc = jnp.einsum('bqhgk,bqhkd->bqhgd', p.astype(vg.dtype), vg)
        kw = lax.dynamic_slice_in_dim(kw_pad, start, WIN + Qb, axis=1)
        vw = lax.dynamic_slice_in_dim(vw_pad, start, WIN + Qb, axis=1)
        pos = start - WIN + jnp.arange(WIN + Qb)
        m_win = (pos[None, :] <= t[:, None]) & (pos[None, :] > t[:, None] - WIN) & (pos[None, :] >= 0)
        s = jnp.einsum('bqhgd,bkhd->bhgqk', qb, kw).astype(jnp.float32)
        p = jax.nn.softmax(jnp.where(m_win, s, NEG), axis=-1)
        o_win = jnp.einsum('bhgqk,bkhd->bqhgd', p.astype(vw.dtype), vw)
        o = gb[..., 0:1] * o_cmp + gb[..., 1:2] * o_slc + gb[..., 2:3] * o_win
        return o.astype(xn.dtype)

    starts = jnp.arange(S // Qb) * Qb
    out = lax.map(block, starts)
    out = out.transpose(1, 0, 2, 3, 4, 5).reshape(B, S, NSA_Q_HEADS * d)
    return out @ w_o


def setup_inputs(seed: int = 0) -> dict:
    key = jax.random.key(seed)
    ks = jax.random.split(key, 24)
    f32 = jnp.float32
    d = NSA_HEAD_DIM

    def nrm(k, shape, scale):
        return jax.random.normal(k, shape, f32) * scale

    return {
        "x": nrm(ks[0], (BATCH, SEQ, D_MODEL), 1.0),
        "norm_mix_gain": 1.0 + nrm(ks[1], (DEPTH, D_MODEL), 0.02),
        "norm_ffn_gain": 1.0 + nrm(ks[2], (DEPTH, D_MODEL), 0.02),
        "ret_w_in": nrm(ks[3], (N_A_LAYERS, D_MODEL, RET_PROJ), D_MODEL ** -0.5),
        "ret_gn_gain": 1.0 + nrm(ks[4], (N_A_LAYERS, RET_HEADS * RET_V_DIM), 0.02),
        "ret_w_out": nrm(ks[5], (N_A_LAYERS, RET_HEADS * RET_V_DIM, D_MODEL), (RET_HEADS * RET_V_DIM) ** -0.5),
        "nsa_kv_norm_gain": 1.0 + nrm(ks[6], (D_MODEL,), 0.02),
        "nsa_w_kv": nrm(ks[7], (D_MODEL, NSA_KV_PROJ), D_MODEL ** -0.5),
        "cmp_pe_k": nrm(ks[8], (CMP_BLOCK, d), 0.1),
        "cmp_w1_k": nrm(ks[9], (CMP_BLOCK * d, CMP_HIDDEN), (CMP_BLOCK * d) ** -0.5),
        "cmp_w2_k": nrm(ks[10], (CMP_HIDDEN, d), CMP_HIDDEN ** -0.5),
        "cmp_pe_v": nrm(ks[11], (CMP_BLOCK, d), 0.1),
        "cmp_w1_v": nrm(ks[12], (CMP_BLOCK * d, CMP_HIDDEN), (CMP_BLOCK * d) ** -0.5),
        "cmp_w2_v": nrm(ks[13], (CMP_HIDDEN, d), CMP_HIDDEN ** -0.5),
        "nsa_w_q": nrm(ks[14], (N_B_LAYERS, D_MODEL, NSA_Q_PROJ), D_MODEL ** -0.5),
        "nsa_w_o": nrm(ks[15], (N_B_LAYERS, NSA_Q_HEADS * d, D_MODEL), (NSA_Q_HEADS * d) ** -0.5),
        "ffn_w_in": nrm(ks[16], (DEPTH, D_MODEL, 2 * D_FF), D_MODEL ** -0.5),
        "ffn_conv_w": nrm(ks[17], (DEPTH, CONV_W, 2 * D_FF), CONV_W ** -0.5),
        "ffn_conv_b": nrm(ks[18], (DEPTH, 2 * D_FF), 0.01),
        "ffn_w_out": nrm(ks[19], (DEPTH, D_FF, D_MODEL), D_FF ** -0.5),
        "final_norm_gain": 1.0 + nrm(ks[20], (D_MODEL,), 0.02),
    }


def reference(x, norm_mix_gain, norm_ffn_gain, ret_w_in, ret_gn_gain, ret_w_out, nsa_kv_norm_gain, nsa_w_kv,
              cmp_pe_k, cmp_w1_k, cmp_w2_k, cmp_pe_v, cmp_w1_v, cmp_w2_v, nsa_w_q, nsa_w_o,
              ffn_w_in, ffn_conv_w, ffn_conv_b, ffn_w_out, final_norm_gain):
    h = x
    shared = None
    for layer in range(DEPTH):
        xn = rmsnorm(h, norm_mix_gain[layer])
        if layer < N_A_LAYERS:
            h = h + retention(xn, ret_w_in[layer], ret_gn_gain[layer], ret_w_out[layer])
        else:
            if layer == N_A_LAYERS:
                shared = nsa_shared_kv(rmsnorm(h, nsa_kv_norm_gain), nsa_w_kv, cmp_pe_k, cmp_w1_k, cmp_w2_k,
                                       cmp_pe_v, cmp_w1_v, cmp_w2_v)
            b = layer - N_A_LAYERS
            h = h + nsa_attention(xn, nsa_w_q[b], nsa_w_o[b], *shared)
        h = h + conv_ffn(rmsnorm(h, norm_ffn_gain[layer]), ffn_w_in[layer], ffn_conv_w[layer],
                         ffn_conv_b[layer], ffn_w_out[layer])
    return rmsnorm(h, final_norm_gain)
```

```python
import functools

import jax
import jax.numpy as jnp
from jax import lax
from jax.experimental import pallas as pl
from jax.experimental.pallas import tpu as pltpu

F32 = jnp.float32
MXU_DTYPE = jnp.bfloat16

RMS_EPS = 1e-6
ROPE_BASE = 10000.0

RET_HEADS = 8
RET_CHUNK = 128

NSA_Q_HEADS = 16
NSA_KV_HEADS = 4
NSA_GROUP = NSA_Q_HEADS // NSA_KV_HEADS
NSA_HEAD_DIM = 128
N_BRANCH = 3
CMP_BLOCK = 32
CMP_STRIDE = 16
CMP_RATIO = CMP_BLOCK // CMP_STRIDE
SLC_BLOCK = 64
SLC_TOPK = 16
SLC_RATIO = SLC_BLOCK // CMP_STRIDE
WIN = 512
CONV_W = 3

NEG = -1e30
MASK_BIAS = -(2.0 ** 100)
PICKED = -3.0e38
SEL_FORCE = 1e30

LANE = 128
VMEM_LIMIT_BYTES = 56 * 1024 * 1024


def _cparams(*sem):
    return pltpu.CompilerParams(dimension_semantics=sem, vmem_limit_bytes=VMEM_LIMIT_BYTES)


def _sigmoid(x):
    return 1.0 / (1.0 + jnp.exp(-x))


def _rms_normalize(x, gain):
    ms = jnp.mean(x * x, axis=-1, keepdims=True)
    return x * lax.rsqrt(ms + RMS_EPS) * gain


def _norm_matmul_kernel(x_ref, g_ref, w_ref, o_ref, xn_ref):
    @pl.when(pl.program_id(1) == 0)
    def _():
        xn_ref[...] = _rms_normalize(x_ref[...], g_ref[...]).astype(xn_ref.dtype)

    o_ref[...] = jnp.dot(xn_ref[...], w_ref[...], preferred_element_type=F32).astype(o_ref.dtype)


def norm_matmul(h, gain, w, *, tm, tn, out_dtype=F32):
    T, D = h.shape
    N = w.shape[1]
    tm, tn = min(tm, T), min(tn, N)
    assert T % tm == 0 and N % tn == 0
    return pl.pallas_call(
        _norm_matmul_kernel,
        grid=(T // tm, N // tn),
        in_specs=[
            pl.BlockSpec((tm, D), lambda i, j: (i, 0)),
            pl.BlockSpec((1, D), lambda i, j: (0, 0)),
            pl.BlockSpec((D, tn), lambda i, j: (0, j)),
        ],
        out_specs=pl.BlockSpec((tm, tn), lambda i, j: (i, j)),
        out_shape=jax.ShapeDtypeStruct((T, N), out_dtype),
        scratch_shapes=[pltpu.VMEM((tm, D), w.dtype)],
        compiler_params=_cparams("arbitrary", "arbitrary"),
        name="norm_matmul",
    )(h, gain.reshape(1, D), w)


def _matmul_residual_kernel(y_ref, w_ref, h_ref, o_ref):
    o_ref[...] = h_ref[...] + jnp.dot(y_ref[...], w_ref[...], preferred_element_type=F32)


def matmul_residual(y, w, h, *, tm, tn):
    T, K = y.shape
    N = w.shape[1]
    tm, tn = min(tm, T), min(tn, N)
    assert T % tm == 0 and N % tn == 0
    return pl.pallas_call(
        _matmul_residual_kernel,
        grid=(T // tm, N // tn),
        in_specs=[
            pl.BlockSpec((tm, K), lambda i, j: (i, 0)),
            pl.BlockSpec((K, tn), lambda i, j: (0, j)),
            pl.BlockSpec((tm, tn), lambda i, j: (i, j)),
        ],
        out_specs=pl.BlockSpec((tm, tn), lambda i, j: (i, j)),
        out_shape=jax.ShapeDtypeStruct((T, N), F32),
        compiler_params=_cparams("arbitrary", "arbitrary"),
        name="matmul_residual",
    )(y, w, h)


def _retention_kernel(q_ref, k_ref, v_ref, g_ref, cos_ref, sin_ref, dmask_ref, qdec_ref, kdec_ref, gn_ref,
                      o_ref, state_ref, *, chunk, n_chunks, k_scale):
    @pl.when(pl.program_id(2) == 0)
    def _():
        state_ref[...] = jnp.zeros_like(state_ref)

    half = q_ref.shape[-1] // 2
    dmask = dmask_ref[0]
    qdec = qdec_ref[0]
    kdec = kdec_ref[0]
    chunk_decay = qdec[chunk - 1:chunk, :]
    gn = gn_ref[...]

    def rotate(x, cos, sin):
        x1, x2 = x[:, :half], x[:, half:]
        return jnp.concatenate([x1 * cos - x2 * sin, x1 * sin + x2 * cos], axis=-1)

    for ci in range(n_chunks):
        rows = pl.ds(ci * chunk, chunk)
        cos, sin = cos_ref[rows, :], sin_ref[rows, :]
        q = rotate(q_ref[0, rows, :], cos, sin)
        k = rotate(k_ref[0, rows, :], cos, sin) * k_scale
        v = v_ref[0, rows, :].astype(MXU_DTYPE)
        qm = q.astype(MXU_DTYPE)
        scores = lax.dot_general(qm, k.astype(MXU_DTYPE), (((1,), (1,)), ((), ())),
                                 preferred_element_type=F32) * dmask
        inner = jnp.dot(scores.astype(MXU_DTYPE), v, preferred_element_type=F32)
        state = state_ref[...]
        cross = jnp.dot(qm, state.astype(MXU_DTYPE), preferred_element_type=F32) * qdec
        k_dec_t = (k * kdec).T.astype(MXU_DTYPE)
        state_ref[...] = state * chunk_decay + jnp.dot(k_dec_t, v, preferred_element_type=F32)
        out = inner + cross
        mu = jnp.mean(out, axis=-1, keepdims=True)
        cen = out - mu
        var = jnp.mean(cen * cen, axis=-1, keepdims=True)
        normed = cen * lax.rsqrt(var + RMS_EPS) * gn
        g = g_ref[0, rows, :]
        o_ref[0, rows, :] = ((g * _sigmoid(g)) * normed).astype(o_ref.dtype)


def retention_core(proj, gn_gain, *, chunk=RET_CHUNK, tokens_per_step=512):
    B, S, P = proj.shape
    H = RET_HEADS
    dk = P // (6 * H)
    dv = 2 * dk
    tc = min(tokens_per_step, S)
    assert S % tc == 0 and tc % chunk == 0
    half = dk // 2
    pos = jnp.arange(S, dtype=F32)
    freqs = ROPE_BASE ** (-jnp.arange(half, dtype=F32) / half)
    ang = pos[:, None] * freqs[None, :]
    cos, sin = jnp.cos(ang), jnp.sin(ang)
    log_gamma = jnp.log(1.0 - 2.0 ** (-5.0 - jnp.arange(H, dtype=F32)))
    idx = jnp.arange(chunk, dtype=F32)
    diff = idx[:, None] - idx[None, :]
    dmask = jnp.where(diff >= 0, jnp.exp(jnp.maximum(diff, 0.0)[None] * log_gamma[:, None, None]), 0.0)
    qdec = jnp.exp((idx[None, :] + 1.0) * log_gamma[:, None])[:, :, None]
    kdec = jnp.exp((chunk - 1.0 - idx[None, :]) * log_gamma[:, None])[:, :, None]
    kern = functools.partial(_retention_kernel, chunk=chunk, n_chunks=tc // chunk, k_scale=dk ** -0.5)
    k_off, v_off, g_off = (H * dk) // dk, (2 * H * dk) // dv, (2 * H * dk + H * dv) // dv
    return pl.pallas_call(
        kern,
        grid=(B, H, S // tc),
        in_specs=[
            pl.BlockSpec((1, tc, dk), lambda b, h, c: (b, c, h)),
            pl.BlockSpec((1, tc, dk), lambda b, h, c: (b, c, k_off + h)),
            pl.BlockSpec((1, tc, dv), lambda b, h, c: (b, c, v_off + h)),
            pl.BlockSpec((1, tc, dv), lambda b, h, c: (b, c, g_off + h)),
            pl.BlockSpec((tc, half), lambda b, h, c: (c, 0)),
            pl.BlockSpec((tc, half), lambda b, h, c: (c, 0)),
            pl.BlockSpec((1, chunk, chunk), lambda b, h, c: (h, 0, 0)),
            pl.BlockSpec((1, chunk, 1), lambda b, h, c: (h, 0, 0)),
            pl.BlockSpec((1, chunk, 1), lambda b, h, c: (h, 0, 0)),
            pl.BlockSpec((1, dv), lambda b, h, c: (0, h)),
        ],
        out_specs=pl.BlockSpec((1, tc, dv), lambda b, h, c: (b, c, h)),
        out_shape=jax.ShapeDtypeStruct((B, S, H * dv), MXU_DTYPE),
        scratch_shapes=[pltpu.VMEM((dk, dv), F32)],
        compiler_params=_cparams("arbitrary", "arbitrary", "arbitrary"),
        name="retention_core",
    )(proj, proj, proj, proj, cos, sin, dmask, qdec, kdec, gn_gain.reshape(1, H * dv))


def _conv_ffn_kernel(x_ref, g_ref, wa_ref, wb_ref, cwa_ref, cwb_ref, cba_ref, cbb_ref, wo_ref, fg_ref,
                     o_ref, xn_ref, acc_ref, carry_a_ref, carry_b_ref, *, tiles_per_seq, final_norm):
    i, j = pl.program_id(0), pl.program_id(1)
    tm = x_ref.shape[0]

    @pl.when((i == 0) & (j == 0))
    def _():
        carry_a_ref[...] = jnp.zeros_like(carry_a_ref)
        carry_b_ref[...] = jnp.zeros_like(carry_b_ref)

    @pl.when(j == 0)
    def _():
        xn_ref[...] = _rms_normalize(x_ref[...], g_ref[...]).astype(xn_ref.dtype)
        acc_ref[...] = jnp.zeros_like(acc_ref)

    seq_start = (i % tiles_per_seq) == 0
    xn = xn_ref[...]

    def causal_conv(w_ref, cw_ref, cb_ref, carry_ref):
        u = jnp.dot(xn, w_ref[...], preferred_element_type=F32)
        prev = jnp.where(seq_start, 0.0, carry_ref[j])
        carry_ref[j] = u[tm - 8:, :]
        row = lax.broadcasted_iota(jnp.int32, u.shape, 0)
        u1 = jnp.where(row == 0, prev[7:8, :], pltpu.roll(u, 1, 0))
        u2 = jnp.where(row == 0, prev[6:7, :], jnp.where(row == 1, prev[7:8, :], pltpu.roll(u, 2, 0)))
        cw = cw_ref[...]
        return ((cb_ref[...] + u2 * cw[0:1, :]) + u1 * cw[1:2, :]) + u * cw[2:3, :]

    a = causal_conv(wa_ref, cwa_ref, cba_ref, carry_a_ref)
    b = causal_conv(wb_ref, cwb_ref, cbb_ref, carry_b_ref)
    act = ((a * _sigmoid(a)) * b).astype(wo_ref.dtype)
    acc_ref[...] += jnp.dot(act, wo_ref[...], preferred_element_type=F32)

    @pl.when(j == pl.num_programs(1) - 1)
    def _():
        out = x_ref[...] + acc_ref[...]
        if final_norm:
            out = _rms_normalize(out, fg_ref[...])
        o_ref[...] = out


def conv_ffn_block(h, seq_len, norm_gain, w_in, conv_w, conv_b, w_out, final_gain, *, final_norm, tm=512, tn=512):
    T, D = h.shape
    F = w_out.shape[0]
    tm, tn = min(tm, seq_len), min(tn, F)
    assert seq_len % tm == 0 and F % tn == 0 and tm >= 8
    nj = F // tn
    kern = functools.partial(_conv_ffn_kernel, tiles_per_seq=seq_len // tm, final_norm=final_norm)
    return pl.pallas_call(
        kern,
        grid=(T // tm, nj),
        in_specs=[
            pl.BlockSpec((tm, D), lambda i, j: (i, 0)),
            pl.BlockSpec((1, D), lambda i, j: (0, 0)),
            pl.BlockSpec((D, tn), lambda i, j: (0, j)),
            pl.BlockSpec((D, tn), lambda i, j: (0, nj + j)),
            pl.BlockSpec((CONV_W, tn), lambda i, j: (0, j)),
            pl.BlockSpec((CONV_W, tn), lambda i, j: (0, nj + j)),
            pl.BlockSpec((1, tn), lambda i, j: (0, j)),
            pl.BlockSpec((1, tn), lambda i, j: (0, nj + j)),
            pl.BlockSpec((tn, D), lambda i, j: (j, 0)),
            pl.BlockSpec((1, D), lambda i, j: (0, 0)),
        ],
        out_specs=pl.BlockSpec((tm, D), lambda i, j: (i, 0)),
        out_shape=jax.ShapeDtypeStruct((T, D), F32),
        scratch_shapes=[
            pltpu.VMEM((tm, D), w_in.dtype),
            pltpu.VMEM((tm, D), F32),
            pltpu.VMEM((nj, 8, tn), F32),
            pltpu.VMEM((nj, 8, tn), F32),
        ],
        compiler_params=_cparams("arbitrary", "arbitrary"),
        name="conv_ffn",
    )(h, norm_gain.reshape(1, D), w_in, w_in, conv_w, conv_w, conv_b.reshape(1, 2 * F), conv_b.reshape(1, 2 * F),
      w_out, final_gain.reshape(1, D))


def _compress_kernel(x_ref, pe_ref, w1_ref, w2_ref, o_ref):
    x = x_ref[0, 0]
    n, kc = x.shape
    pe = pe_ref[0]
    first = jnp.dot((x + pe[0:1, :]).astype(MXU_DTYPE), w1_ref[0, :kc, :], preferred_element_type=F32)
    second = jnp.dot((x + pe[1:2, :]).astype(MXU_DTYPE), w1_ref[0, kc:, :], preferred_element_type=F32)
    hid = first + pltpu.roll(second, n - 1, 0)
    gelu = 0.5 * hid * (1.0 + jnp.tanh(0.7978845608028654 * (hid + 0.044715 * (hid * hid * hid))))
    out = jnp.dot(gelu.astype(MXU_DTYPE), w2_ref[0], preferred_element_type=F32)
    row = lax.broadcasted_iota(jnp.int32, out.shape, 0)
    o_ref[0, 0] = jnp.where(row < n - 1, out, 0.0)


def compress_tokens(x, pe, w1, w2):
    two, G, n, kc = x.shape
    hid, d = w2.shape[1], w2.shape[2]
    return pl.pallas_call(
        _compress_kernel,
        grid=(two, G),
        in_specs=[
            pl.BlockSpec((1, 1, n, kc), lambda s, g: (s, g, 0, 0)),
            pl.BlockSpec((1, CMP_RATIO, kc), lambda s, g: (s, 0, 0)),
            pl.BlockSpec((1, CMP_RATIO * kc, hid), lambda s, g: (s, 0, 0)),
            pl.BlockSpec((1, hid, d), lambda s, g: (s, 0, 0)),
        ],
        out_specs=pl.BlockSpec((1, 1, n, d), lambda s, g: (s, g, 0, 0)),
        out_shape=jax.ShapeDtypeStruct((two, G, n, d), F32),
        compiler_params=_cparams("arbitrary", "arbitrary"),
        name="compress_tokens",
    )(x, pe, w1, w2)


def _nsa_cmp_kernel(q_ref, kc_ref, vc_ref, o_ref, sel_ref, *, q_scale, top_k):
    i = pl.program_id(2)
    tq = q_ref.shape[1]
    d = NSA_HEAD_DIM
    ncp = kc_ref.shape[2]
    nsel = ncp // SLC_RATIO
    kc, vc = kc_ref[0, 0], vc_ref[0, 0]
    t = i * tq + lax.broadcasted_iota(jnp.int32, (tq, ncp), 0)
    col = lax.broadcasted_iota(jnp.int32, (tq, ncp), 1)
    cblk = SLC_RATIO * (col & (nsel - 1)) + (col >> (nsel.bit_length() - 1))
    valid = cblk * CMP_STRIDE + (CMP_BLOCK - 1) <= t
    imp = jnp.zeros((tq, ncp), F32)
    for g in range(NSA_GROUP):
        qg = (q_ref[0, :, g * d:(g + 1) * d] * q_scale).astype(MXU_DTYPE)
        s = lax.dot_general(qg, kc, (((1,), (1,)), ((), ())), preferred_element_type=F32)
        s = jnp.where(valid, s, NEG)
        e = jnp.where(valid, jnp.exp(s - jnp.max(s, axis=-1, keepdims=True)), 0.0)
        denom = jnp.sum(e, axis=-1, keepdims=True)
        p = e * (1.0 / jnp.where(denom > 0.0, denom, 1.0))
        o_ref[0, :, g * d:(g + 1) * d] = jnp.dot(p.astype(MXU_DTYPE), vc, preferred_element_type=F32)
        imp = imp + p
    ph = [imp[:, r * nsel:(r + 1) * nsel] for r in range(SLC_RATIO)]
    jcol = lax.broadcasted_iota(jnp.int32, (tq, nsel), 1)
    prev_last = jnp.where(jcol == 0, 0.0, pltpu.roll(ph[SLC_RATIO - 1], 1, 1))
    p_slc = prev_last
    for r in range(SLC_RATIO):
        p_slc = p_slc + ph[r]
    tj = i * tq + lax.broadcasted_iota(jnp.int32, (tq, nsel), 0)
    cur = tj >> (SLC_BLOCK.bit_length() - 1)
    forced = (jcol == 0) | (jcol == cur) | (jcol == cur - 1)
    causal = jcol <= cur
    score = jnp.where(forced, SEL_FORCE, jnp.where(causal, p_slc, NEG))
    jf = jcol.astype(F32)
    picked = jnp.zeros((tq, nsel), F32)
    for _ in range(top_k):
        m = jnp.max(score, axis=-1, keepdims=True)
        first = jnp.min(jnp.where(score == m, jf, float(nsel)), axis=-1, keepdims=True)
        hit = jf == first
        picked = jnp.where(hit, 1.0, picked)
        score = jnp.where(hit, PICKED, score)
    sel_ref[0, 0] = jnp.where((picked > 0.0) & causal, 0.0, MASK_BIAS).astype(sel_ref.dtype)


def nsa_cmp_branch(qproj, kc, vc, *, tq=128):
    B, S, _ = qproj.shape
    Hkv, G, d = NSA_KV_HEADS, NSA_GROUP, NSA_HEAD_DIM
    ncp = kc.shape[2]
    nsel = ncp // SLC_RATIO
    tq = min(tq, S)
    assert S % tq == 0 and nsel & (nsel - 1) == 0
    kern = functools.partial(_nsa_cmp_kernel, q_scale=d ** -0.5, top_k=min(SLC_TOPK, nsel))
    return pl.pallas_call(
        kern,
        grid=(B, Hkv, S // tq),
        in_specs=[
            pl.BlockSpec((1, tq, G * d), lambda b, h, i: (b, i, h)),
            pl.BlockSpec((1, 1, ncp, d), lambda b, h, i: (b, h, 0, 0)),
            pl.BlockSpec((1, 1, ncp, d), lambda b, h, i: (b, h, 0, 0)),
        ],
        out_specs=[
            pl.BlockSpec((1, tq, G * d), lambda b, h, i: (b, i, h)),
            pl.BlockSpec((1, 1, tq, nsel), lambda b, h, i: (b, h, i, 0)),
        ],
        out_shape=[
            jax.ShapeDtypeStruct((B, S, Hkv * G * d), F32),
            jax.ShapeDtypeStruct((B, Hkv, S, nsel), MXU_DTYPE),
        ],
        compiler_params=_cparams("arbitrary", "arbitrary", "arbitrary"),
        name="nsa_cmp",
    )(qproj, kc, vc)


def _nsa_slc_kernel(q_ref, sel_ref, kt_ref, v_ref, o_ref, qa_ref, m_ref, l_ref, acc_ref, *, q_scale, tk):
    i = pl.program_id(2)
    tq = q_ref.shape[1]
    d = NSA_HEAD_DIM
    G = NSA_GROUP
    nblk = sel_ref.shape[3]
    span = LANE * SLC_BLOCK
    n_halves = pl.cdiv(nblk, LANE)

    m_ref[...] = jnp.full_like(m_ref, PICKED)
    l_ref[...] = jnp.zeros_like(l_ref)
    acc_ref[...] = jnp.zeros_like(acc_ref)
    for g in range(G):
        qa_ref[g * tq:(g + 1) * tq, :d] = (q_ref[0, :, g * d:(g + 1) * d] * q_scale).astype(qa_ref.dtype)

    def attend(kt_idx, diagonal):
        k0 = pl.multiple_of(kt_idx * tk, tk)
        s = jnp.dot(qa_ref[...], kt_ref[0, 0, :, pl.ds(k0, tk)], preferred_element_type=F32)
        if diagonal:
            row = lax.broadcasted_iota(jnp.int32, s.shape, 0)
            tok = i * tq + (row & (tq - 1))
            key = k0 + lax.broadcasted_iota(jnp.int32, s.shape, 1)
            s = jnp.where(key <= tok, s, NEG)
        m_old = m_ref[...]
        m_new = jnp.maximum(m_old, jnp.max(s, axis=-1, keepdims=True))
        alpha = jnp.exp(m_old - m_new)
        p = jnp.exp(s - m_new)
        l_ref[...] = alpha * l_ref[...] + jnp.sum(p, axis=-1, keepdims=True)
        acc_ref[...] = alpha * acc_ref[...] + jnp.dot(p.astype(MXU_DTYPE), v_ref[0, 0, pl.ds(k0, tk), :],
                                                      preferred_element_type=F32)
        m_ref[...] = m_new

    last = ((i + 1) * tq - 1) // tk
    tiles_per_half = span // tk
    for hf in range(n_halves):
        lo = hf * tiles_per_half
        hi = jnp.minimum(last, lo + tiles_per_half)

        @pl.when(lo <= last)
        def _():
            sel_half = sel_ref[0, 0, :, hf * LANE:(hf + 1) * LANE]
            for g in range(G):
                qa_ref[g * tq:(g + 1) * tq, d:] = sel_half

            def body(kt_idx, carry):
                attend(kt_idx, False)
                return carry

            lax.fori_loop(lo, hi, body, 0)

            @pl.when(last < lo + tiles_per_half)
            def _():
                attend(last, True)

    inv = 1.0 / l_ref[...]
    out = acc_ref[...] * inv
    for g in range(G):
        o_ref[0, :, g * d:(g + 1) * d] = out[g * tq:(g + 1) * tq, :]


def nsa_slc_branch(qproj, sel, kt_aug, v, *, tq=256, tk=512):
    B, S, _ = qproj.shape
    Hkv, G, d = NSA_KV_HEADS, NSA_GROUP, NSA_HEAD_DIM
    nblk = sel.shape[3]
    tq, tk = min(tq, S), min(tk, S)
    assert S % tq == 0 and S % tk == 0 and tk % tq == 0 and tq & (tq - 1) == 0
    assert nblk % LANE == 0 and (LANE * SLC_BLOCK) % tk == 0
    kern = functools.partial(_nsa_slc_kernel, q_scale=d ** -0.5, tk=tk)
    return pl.pallas_call(
        kern,
        grid=(B, Hkv, S // tq),
        in_specs=[
            pl.BlockSpec((1, tq, G * d), lambda b, h, i: (b, i, h)),
            pl.BlockSpec((1, 1, tq, nblk), lambda b, h, i: (b, h, i, 0)),
            pl.BlockSpec((1, 1, d + LANE, S), lambda b, h, i: (b, h, 0, 0)),
            pl.BlockSpec((1, 1, S, d), lambda b, h, i: (b, h, 0, 0)),
        ],
        out_specs=pl.BlockSpec((1, tq, G * d), lambda b, h, i: (b, i, h)),
        out_shape=jax.ShapeDtypeStruct((B, S, Hkv * G * d), F32),
        scratch_shapes=[
            pltpu.VMEM((G * tq, d + LANE), MXU_DTYPE),
            pltpu.VMEM((G * tq, 1), F32),
            pltpu.VMEM((G * tq, 1), F32),
            pltpu.VMEM((G * tq, d), F32),
        ],
        compiler_params=_cparams("arbitrary", "arbitrary", "arbitrary"),
        name="nsa_slc",
    )(qproj, sel, kt_aug, v)


def _nsa_win_kernel(q_ref, kt_ref, v_ref, o_ref, *, q_scale):
    i = pl.program_id(2)
    tq = q_ref.shape[1]
    d = NSA_HEAD_DIM
    nk = WIN + tq
    k0 = pl.multiple_of(i * tq, tq)
    kt = kt_ref[0, 0, :, pl.ds(k0, nk)]
    v = v_ref[0, 0, pl.ds(k0, nk), :]
    tok = i * tq + lax.broadcasted_iota(jnp.int32, (tq, nk), 0)
    pos = i * tq - WIN + lax.broadcasted_iota(jnp.int32, (tq, nk), 1)
    mask = (pos <= tok) & (pos > tok - WIN) & (pos >= 0)
    for g in range(NSA_GROUP):
        qg = (q_ref[0, :, g * d:(g + 1) * d] * q_scale).astype(MXU_DTYPE)
        s = jnp.where(mask, jnp.dot(qg, kt, preferred_element_type=F32), NEG)
        e = jnp.exp(s - jnp.max(s, axis=-1, keepdims=True))
        p = e * (1.0 / jnp.sum(e, axis=-1, keepdims=True))
        o_ref[0, :, g * d:(g + 1) * d] = jnp.dot(p.astype(MXU_DTYPE), v, preferred_element_type=F32)


def nsa_win_branch(qproj, kt_pad, v_pad, *, tq=256):
    B, S, _ = qproj.shape
    Hkv, G, d = NSA_KV_HEADS, NSA_GROUP, NSA_HEAD_DIM
    tq = min(tq, S)
    assert S % tq == 0 and tq % LANE == 0
    kern = functools.partial(_nsa_win_kernel, q_scale=d ** -0.5)
    return pl.pallas_call(
        kern,
        grid=(B, Hkv, S // tq),
        in_specs=[
            pl.BlockSpec((1, tq, G * d), lambda b, h, i: (b, i, h)),
            pl.BlockSpec((1, 1, d, WIN + S), lambda b, h, i: (b, h, 0, 0)),
            pl.BlockSpec((1, 1, WIN + S, d), lambda b, h, i: (b, h, 0, 0)),
        ],
        out_specs=pl.BlockSpec((1, tq, G * d), lambda b, h, i: (b, i, h)),
        out_shape=jax.ShapeDtypeStruct((B, S, Hkv * G * d), F32),
        compiler_params=_cparams("arbitrary", "arbitrary", "arbitrary"),
        name="nsa_win",
    )(qproj, kt_pad, v_pad)


def _nsa_out_kernel(oc_ref, os_ref, ow_ref, gate_ref, w_ref, h_ref, o_ref, merged_ref):
    d = NSA_HEAD_DIM
    gates = _sigmoid(gate_ref[...])
    for hq in range(NSA_Q_HEADS):
        cols = slice(hq * d, (hq + 1) * d)
        c = N_BRANCH * hq
        merged = (gates[:, c:c + 1] * oc_ref[:, cols] + gates[:, c + 1:c + 2] * os_ref[:, cols]
                  + gates[:, c + 2:c + 3] * ow_ref[:, cols])
        merged_ref[:, cols] = merged.astype(merged_ref.dtype)
    o_ref[...] = h_ref[...] + jnp.dot(merged_ref[...], w_ref[...], preferred_element_type=F32)


def nsa_out(o_cmp, o_slc, o_win, qproj, w_o, h, *, tm=256):
    T, HD = o_cmp.shape
    D = w_o.shape[1]
    tm = min(tm, T)
    assert T % tm == 0 and HD % LANE == 0
    gate_blk = HD // LANE
    row = lambda i: (i, 0)
    return pl.pallas_call(
        _nsa_out_kernel,
        grid=(T // tm,),
        in_specs=[
            pl.BlockSpec((tm, HD), row),
            pl.BlockSpec((tm, HD), row),
            pl.BlockSpec((tm, HD), row),
            pl.BlockSpec((tm, LANE), lambda i: (i, gate_blk)),
            pl.BlockSpec((HD, D), lambda i: (0, 0)),
            pl.BlockSpec((tm, D), row),
        ],
        out_specs=pl.BlockSpec((tm, D), row),
        out_shape=jax.ShapeDtypeStruct((T, D), F32),
        scratch_shapes=[pltpu.VMEM((tm, HD), w_o.dtype)],
        compiler_params=_cparams("arbitrary"),
        name="nsa_out",
    )(o_cmp, o_slc, o_win, qproj, w_o, h)


def _phase_major(x, nsel):
    B, n, H, d = x.shape
    return x.reshape(B, nsel, SLC_RATIO, H, d).transpose(0, 3, 2, 1, 4).reshape(B, H, n, d)


def _cast(w):
    return w.astype(MXU_DTYPE)


def nsa_shared_kv(h, B, S, kv_gain, w_kv, cmp_pe_k, cmp_w1_k, cmp_w2_k, cmp_pe_v, cmp_w1_v, cmp_w2_v):
    Hkv, d = NSA_KV_HEADS, NSA_HEAD_DIM
    kv = norm_matmul(h, kv_gain, _cast(w_kv), tm=1024, tn=1024)
    kv = kv.reshape(B, S, 2 * N_BRANCH, Hkv, d)
    n = S // CMP_STRIDE
    nsel = S // SLC_BLOCK
    xc = kv[:, :, 0:2].reshape(B, n, CMP_STRIDE, 2, Hkv, d).transpose(3, 0, 4, 1, 2, 5)
    xc = xc.reshape(2, B * Hkv, n, CMP_STRIDE * d)
    pe = jnp.stack([cmp_pe_k, cmp_pe_v]).reshape(2, CMP_RATIO, CMP_STRIDE * d)
    cmp = compress_tokens(xc, pe, _cast(jnp.stack([cmp_w1_k, cmp_w1_v])), _cast(jnp.stack([cmp_w2_k, cmp_w2_v])))
    cmp = cmp.reshape(2, B, Hkv, n, d).transpose(0, 1, 3, 2, 4)
    kc = _cast(_phase_major(cmp[0], nsel))
    vc = _cast(_phase_major(cmp[1], nsel))
    heads_first = lambda t: t.transpose(0, 2, 1, 3)
    k_slc, v_slc = heads_first(kv[:, :, 2]), _cast(heads_first(kv[:, :, 3]))
    k_win, v_win = heads_first(kv[:, :, 4]), heads_first(kv[:, :, 5])
    blk = (jnp.arange(S) // SLC_BLOCK) % LANE
    onehot = (blk[None, :] == jnp.arange(LANE)[:, None]).astype(MXU_DTYPE)
    kt_slc = jnp.concatenate([_cast(k_slc.transpose(0, 1, 3, 2)), jnp.broadcast_to(onehot, (B, Hkv, LANE, S))],
                             axis=2)
    kt_win = jnp.pad(_cast(k_win.transpose(0, 1, 3, 2)), ((0, 0), (0, 0), (0, 0), (WIN, 0)))
    v_win = jnp.pad(_cast(v_win), ((0, 0), (0, 0), (WIN, 0), (0, 0)))
    return kc, vc, kt_slc, v_slc, kt_win, v_win


def nsa_attention_block(h, B, S, mix_gain, w_q, w_o, shared):
    kc, vc, kt_slc, v_slc, kt_win, v_win = shared
    T = B * S
    nq = NSA_Q_HEADS * NSA_HEAD_DIM
    nqp = nq + LANE
    w_q = jnp.pad(w_q, ((0, 0), (0, nqp - w_q.shape[1])))
    qproj = norm_matmul(h, mix_gain, _cast(w_q), tm=512, tn=nqp)
    qp3 = qproj.reshape(B, S, nqp)
    o_cmp, sel = nsa_cmp_branch(qp3, kc, vc)
    o_slc = nsa_slc_branch(qp3, sel, kt_slc, v_slc)
    o_win = nsa_win_branch(qp3, kt_win, v_win)
    return nsa_out(o_cmp.reshape(T, nq), o_slc.reshape(T, nq), o_win.reshape(T, nq), qproj, _cast(w_o), h)


def nsa_mixer(h, B, S, mix_gain, kv_gain, w_kv, cmp_pe_k, cmp_w1_k, cmp_w2_k, cmp_pe_v, cmp_w1_v, cmp_w2_v, w_q, w_o):
    shared = nsa_shared_kv(h, B, S, kv_gain, w_kv, cmp_pe_k, cmp_w1_k, cmp_w2_k, cmp_pe_v, cmp_w1_v, cmp_w2_v)
    return nsa_attention_block(h, B, S, mix_gain, w_q, w_o, shared)


def kernel(x, norm_mix_gain, norm_ffn_gain, ret_w_in, ret_gn_gain, ret_w_out, nsa_kv_norm_gain, nsa_w_kv, cmp_pe_k, cmp_w1_k, cmp_w2_k, cmp_pe_v, cmp_w1_v, cmp_w2_v, nsa_w_q, nsa_w_o, ffn_w_in, ffn_conv_w, ffn_conv_b, ffn_w_out, final_norm_gain):
    B, S, D = x.shape
    T = B * S
    depth = norm_mix_gain.shape[0]
    n_a = depth // 2
    h = x.reshape(T, D)
    shared = None
    for layer in range(depth):
        if layer < n_a:
            proj = norm_matmul(h, norm_mix_gain[layer], _cast(ret_w_in[layer]), tm=1024, tn=1024)
            y = retention_core(proj.reshape(B, S, -1), ret_gn_gain[layer])
            h = matmul_residual(y.reshape(T, -1), _cast(ret_w_out[layer]), h, tm=512, tn=1024)
        else:
            if layer == n_a:
                shared = nsa_shared_kv(h, B, S, nsa_kv_norm_gain, nsa_w_kv, cmp_pe_k, cmp_w1_k, cmp_w2_k,
                                       cmp_pe_v, cmp_w1_v, cmp_w2_v)
            b = layer - n_a
            h = nsa_attention_block(h, B, S, norm_mix_gain[layer], nsa_w_q[b], nsa_w_o[b], shared)
        h = conv_ffn_block(h, S, norm_ffn_gain[layer], _cast(ffn_w_in[layer]), ffn_conv_w[layer], ffn_conv_b[layer],
                           _cast(ffn_w_out[layer]), final_norm_gain, final_norm=(layer == depth - 1))
    return h.reshape(B, S, D)
```

```python
import functools

import jax
import jax.numpy as jnp
from jax import lax
from jax.experimental import pallas as pl
from jax.experimental.pallas import tpu as pltpu

F32 = jnp.float32
MXU_DTYPE = jnp.bfloat16

RMS_EPS = 1e-6
ROPE_BASE = 10000.0

RET_HEADS = 8
RET_CHUNK = 128

NSA_Q_HEADS = 16
NSA_KV_HEADS = 4
NSA_GROUP = NSA_Q_HEADS // NSA_KV_HEADS
NSA_HEAD_DIM = 128
N_BRANCH = 3
CMP_BLOCK = 32
CMP_STRIDE = 16
CMP_RATIO = CMP_BLOCK // CMP_STRIDE
SLC_BLOCK = 64
SLC_TOPK = 16
SLC_RATIO = SLC_BLOCK // CMP_STRIDE
WIN = 512
CONV_W = 3

NEG = -1e30
MASK_BIAS = -(2.0 ** 100)
PICKED = -3.0e38
SEL_FORCE = 1e30

LANE = 128
VMEM_LIMIT_BYTES = 56 * 1024 * 1024


def _cparams(*sem):
    return pltpu.CompilerParams(dimension_semantics=sem, vmem_limit_bytes=VMEM_LIMIT_BYTES)


def _sigmoid(x):
    return 1.0 / (1.0 + jnp.exp(-x))


def _rms_normalize(x, gain):
    ms = jnp.mean(x * x, axis=-1, keepdims=True)
    return x * lax.rsqrt(ms + RMS_EPS) * gain


def _norm_matmul_kernel(x_ref, g_ref, w_ref, o_ref, xn_ref):
    @pl.when(pl.program_id(1) == 0)
    def _():
        xn_ref[...] = _rms_normalize(x_ref[...], g_ref[...]).astype(xn_ref.dtype)

    o_ref[...] = jnp.dot(xn_ref[...], w_ref[...], preferred_element_type=F32).astype(o_ref.dtype)


def norm_matmul(h, gain, w, *, tm, tn, out_dtype=F32):
    T, D = h.shape
    N = w.shape[1]
    tm, tn = min(tm, T), min(tn, N)
    assert T % tm == 0 and N % tn == 0
    return pl.pallas_call(
        _norm_matmul_kernel,
        grid=(T // tm, N // tn),
        in_specs=[
            pl.BlockSpec((tm, D), lambda i, j: (i, 0)),
            pl.BlockSpec((1, D), lambda i, j: (0, 0)),
            pl.BlockSpec((D, tn), lambda i, j: (0, j)),
        ],
        out_specs=pl.BlockSpec((tm, tn), lambda i, j: (i, j)),
        out_shape=jax.ShapeDtypeStruct((T, N), out_dtype),
        scratch_shapes=[pltpu.VMEM((tm, D), w.dtype)],
        compiler_params=_cparams("arbitrary", "arbitrary"),
        name="norm_matmul",
    )(h, gain.reshape(1, D), w)


def _matmul_residual_kernel(y_ref, w_ref, h_ref, o_ref):
    o_ref[...] = h_ref[...] + jnp.dot(y_ref[...], w_ref[...], preferred_element_type=F32)


def matmul_residual(y, w, h, *, tm, tn):
    T, K = y.shape
    N = w.shape[1]
    tm, tn = min(tm, T), min(tn, N)
    assert T % tm == 0 and N % tn == 0
    return pl.pallas_call(
        _matmul_residual_kernel,
        grid=(T // tm, N // tn),
        in_specs=[
            pl.BlockSpec((tm, K), lambda i, j: (i, 0)),
            pl.BlockSpec((K, tn), lambda i, j: (0, j)),
            pl.BlockSpec((tm, tn), lambda i, j: (i, j)),
        ],
        out_specs=pl.BlockSpec((tm, tn), lambda i, j: (i, j)),
        out_shape=jax.ShapeDtypeStruct((T, N), F32),
        compiler_params=_cparams("arbitrary", "arbitrary"),
        name="matmul_residual",
    )(y, w, h)


def _retention_kernel(q_ref, k_ref, v_ref, g_ref, cos_ref, sin_ref, dmask_ref, qdec_ref, kdec_ref, gn_ref,
                      o_ref, state_ref, *, chunk, n_chunks, k_scale):
    @pl.when(pl.program_id(2) == 0)
    def _():
        state_ref[...] = jnp.zeros_like(state_ref)

    half = q_ref.shape[-1] // 2
    dmask = dmask_ref[0]
    qdec = qdec_ref[0]
    kdec = kdec_ref[0]
    chunk_decay = qdec[chunk - 1:chunk, :]
    gn = gn_ref[...]

    def rotate(x, cos, sin):
        x1, x2 = x[:, :half], x[:, half:]
        return jnp.concatenate([x1 * cos - x2 * sin, x1 * sin + x2 * cos], axis=-1)

    for ci in range(n_chunks):
        rows = pl.ds(ci * chunk, chunk)
        cos, sin = cos_ref[rows, :], sin_ref[rows, :]
        q = rotate(q_ref[0, rows, :], cos, sin)
        k = rotate(k_ref[0, rows, :], cos, sin) * k_scale
        v = v_ref[0, rows, :].astype(MXU_DTYPE)
        qm = q.astype(MXU_DTYPE)
        scores = lax.dot_general(qm, k.astype(MXU_DTYPE), (((1,), (1,)), ((), ())),
                                 preferred_element_type=F32) * dmask
        inner = jnp.dot(scores.astype(MXU_DTYPE), v, preferred_element_type=F32)
        state = state_ref[...]
        cross = jnp.dot(qm, state.astype(MXU_DTYPE), preferred_element_type=F32) * qdec
        k_dec_t = (k * kdec).T.astype(MXU_DTYPE)
        state_ref[...] = state * chunk_decay + jnp.dot(k_dec_t, v, preferred_element_type=F32)
        out = inner + cross
        mu = jnp.mean(out, axis=-1, keepdims=True)
        cen = out - mu
        var = jnp.mean(cen * cen, axis=-1, keepdims=True)
        normed = cen * lax.rsqrt(var + RMS_EPS) * gn
        g = g_ref[0, rows, :]
        o_ref[0, rows, :] = ((g * _sigmoid(g)) * normed).astype(o_ref.dtype)


def retention_core(proj, gn_gain, *, chunk=RET_CHUNK, tokens_per_step=512):
    B, S, P = proj.shape
    H = RET_HEADS
    dk = P // (6 * H)
    dv = 2 * dk
    tc = min(tokens_per_step, S)
    assert S % tc == 0 and tc % chunk == 0
    half = dk // 2
    pos = jnp.arange(S, dtype=F32)
    freqs = ROPE_BASE ** (-jnp.arange(half, dtype=F32) / half)
    ang = pos[:, None] * freqs[None, :]
    cos, sin = jnp.cos(ang), jnp.sin(ang)
    log_gamma = jnp.log(1.0 - 2.0 ** (-5.0 - jnp.arange(H, dtype=F32)))
    idx = jnp.arange(chunk, dtype=F32)
    diff = idx[:, None] - idx[None, :]
    dmask = jnp.where(diff >= 0, jnp.exp(jnp.maximum(diff, 0.0)[None] * log_gamma[:, None, None]), 0.0)
    qdec = jnp.exp((idx[None, :] + 1.0) * log_gamma[:, None])[:, :, None]
    kdec = jnp.exp((chunk - 1.0 - idx[None, :]) * log_gamma[:, None])[:, :, None]
    kern = functools.partial(_retention_kernel, chunk=chunk, n_chunks=tc // chunk, k_scale=dk ** -0.5)
    k_off, v_off, g_off = (H * dk) // dk, (2 * H * dk) // dv, (2 * H * dk + H * dv) // dv
    return pl.pallas_call(
        kern,
        grid=(B, H, S // tc),
        in_specs=[
            pl.BlockSpec((1, tc, dk), lambda b, h, c: (b, c, h)),
            pl.BlockSpec((1, tc, dk), lambda b, h, c: (b, c, k_off + h)),
            pl.BlockSpec((1, tc, dv), lambda b, h, c: (b, c, v_off + h)),
            pl.BlockSpec((1, tc, dv), lambda b, h, c: (b, c, g_off + h)),
            pl.BlockSpec((tc, half), lambda b, h, c: (c, 0)),
            pl.BlockSpec((tc, half), lambda b, h, c: (c, 0)),
            pl.BlockSpec((1, chunk, chunk), lambda b, h, c: (h, 0, 0)),
            pl.BlockSpec((1, chunk, 1), lambda b, h, c: (h, 0, 0)),
            pl.BlockSpec((1, chunk, 1), lambda b, h, c: (h, 0, 0)),
            pl.BlockSpec((1, dv), lambda b, h, c: (0, h)),
        ],
        out_specs=pl.BlockSpec((1, tc, dv), lambda b, h, c: (b, c, h)),
        out_shape=jax.ShapeDtypeStruct((B, S, H * dv), MXU_DTYPE),
        scratch_shapes=[pltpu.VMEM((dk, dv), F32)],
        compiler_params=_cparams("arbitrary", "arbitrary", "arbitrary"),
        name="retention_core",
    )(proj, proj, proj, proj, cos, sin, dmask, qdec, kdec, gn_gain.reshape(1, H * dv))


def _conv_ffn_kernel(x_ref, g_ref, wa_ref, wb_ref, cwa_ref, cwb_ref, cba_ref, cbb_ref, wo_ref, fg_ref,
                     o_ref, xn_ref, acc_ref, carry_a_ref, carry_b_ref, *, tiles_per_seq, final_norm):
    i, j = pl.program_id(0), pl.program_id(1)
    tm = x_ref.shape[0]

    @pl.when((i == 0) & (j == 0))
    def _():
        carry_a_ref[...] = jnp.zeros_like(carry_a_ref)
        carry_b_ref[...] = jnp.zeros_like(carry_b_ref)

    @pl.when(j == 0)
    def _():
        xn_ref[...] = _rms_normalize(x_ref[...], g_ref[...]).astype(xn_ref.dtype)
        acc_ref[...] = jnp.zeros_like(acc_ref)

    seq_start = (i % tiles_per_seq) == 0
    xn = xn_ref[...]

    def causal_conv(w_ref, cw_ref, cb_ref, carry_ref):
        u = jnp.dot(xn, w_ref[...], preferred_element_type=F32)
        prev = jnp.where(seq_start, 0.0, carry_ref[j])
        carry_ref[j] = u[tm - 8:, :]
        row = lax.broadcasted_iota(jnp.int32, u.shape, 0)
        u1 = jnp.where(row == 0, prev[7:8, :], pltpu.roll(u, 1, 0))
        u2 = jnp.where(row == 0, prev[6:7, :], jnp.where(row == 1, prev[7:8, :], pltpu.roll(u, 2, 0)))
        cw = cw_ref[...]
        return ((cb_ref[...] + u2 * cw[0:1, :]) + u1 * cw[1:2, :]) + u * cw[2:3, :]

    a = causal_conv(wa_ref, cwa_ref, cba_ref, carry_a_ref)
    b = causal_conv(wb_ref, cwb_ref, cbb_ref, carry_b_ref)
    act = ((a * _sigmoid(a)) * b).astype(wo_ref.dtype)
    acc_ref[...] += jnp.dot(act, wo_ref[...], preferred_element_type=F32)

    @pl.when(j == pl.num_programs(1) - 1)
    def _():
        out = x_ref[...] + acc_ref[...]
        if final_norm:
            out = _rms_normalize(out, fg_ref[...])
        o_ref[...] = out


def conv_ffn_block(h, seq_len, norm_gain, w_in, conv_w, conv_b, w_out, final_gain, *, final_norm, tm=512, tn=512):
    T, D = h.shape
    F = w_out.shape[0]
    tm, tn = min(tm, seq_len), min(tn, F)
    assert seq_len % tm == 0 and F % tn == 0 and tm >= 8
    nj = F // tn
    kern = functools.partial(_conv_ffn_kernel, tiles_per_seq=seq_len // tm, final_norm=final_norm)
    return pl.pallas_call(
        kern,
        grid=(T // tm, nj),
        in_specs=[
            pl.BlockSpec((tm, D), lambda i, j: (i, 0)),
            pl.BlockSpec((1, D), lambda i, j: (0, 0)),
            pl.BlockSpec((D, tn), lambda i, j: (0, j)),
            pl.BlockSpec((D, tn), lambda i, j: (0, nj + j)),
            pl.BlockSpec((CONV_W, tn), lambda i, j: (0, j)),
            pl.BlockSpec((CONV_W, tn), lambda i, j: (0, nj + j)),
            pl.BlockSpec((1, tn), lambda i, j: (0, j)),
            pl.BlockSpec((1, tn), lambda i, j: (0, nj + j)),
            pl.BlockSpec((tn, D), lambda i, j: (j, 0)),
            pl.BlockSpec((1, D), lambda i, j: (0, 0)),
        ],
        out_specs=pl.BlockSpec((tm, D), lambda i, j: (i, 0)),
        out_shape=jax.ShapeDtypeStruct((T, D), F32),
        scratch_shapes=[
            pltpu.VMEM((tm, D), w_in.dtype),
            pltpu.VMEM((tm, D), F32),
            pltpu.VMEM((nj, 8, tn), F32),
            pltpu.VMEM((nj, 8, tn), F32),
        ],
        compiler_params=_cparams("arbitrary", "arbitrary"),
        name="conv_ffn",
    )(h, norm_gain.reshape(1, D), w_in, w_in, conv_w, conv_w, conv_b.reshape(1, 2 * F), conv_b.reshape(1, 2 * F),
      w_out, final_gain.reshape(1, D))


def _compress_kernel(x_ref, pe_ref, w1_ref, w2_ref, o_ref):
    x = x_ref[0, 0]
    n, kc = x.shape
    pe = pe_ref[0]
    first = jnp.dot((x + pe[0:1, :]).astype(MXU_DTYPE), w1_ref[0, :kc, :], preferred_element_type=F32)
    second = jnp.dot((x + pe[1:2, :]).astype(MXU_DTYPE), w1_ref[0, kc:, :], preferred_element_type=F32)
    hid = first + pltpu.roll(second, n - 1, 0)
    gelu = 0.5 * hid * (1.0 + jnp.tanh(0.7978845608028654 * (hid + 0.044715 * (hid * hid * hid))))
    out = jnp.dot(gelu.astype(MXU_DTYPE), w2_ref[0], preferred_element_type=F32)
    row = lax.broadcasted_iota(jnp.int32, out.shape, 0)
    o_ref[0, 0] = jnp.where(row < n - 1, out, 0.0)


def compress_tokens(x, pe, w1, w2):
    two, G, n, kc = x.shape
    hid, d = w2.shape[1], w2.shape[2]
    return pl.pallas_call(
        _compress_kernel,
        grid=(two, G),
        in_specs=[
            pl.BlockSpec((1, 1, n, kc), lambda s, g: (s, g, 0, 0)),
            pl.BlockSpec((1, CMP_RATIO, kc), lambda s, g: (s, 0, 0)),
            pl.BlockSpec((1, CMP_RATIO * kc, hid), lambda s, g: (s, 0, 0)),
            pl.BlockSpec((1, hid, d), lambda s, g: (s, 0, 0)),
        ],
        out_specs=pl.BlockSpec((1, 1, n, d), lambda s, g: (s, g, 0, 0)),
        out_shape=jax.ShapeDtypeStruct((two, G, n, d), F32),
        compiler_params=_cparams("arbitrary", "arbitrary"),
        name="compress_tokens",
    )(x, pe, w1, w2)


def _nsa_cmp_kernel(q_ref, kc_ref, vc_ref, o_ref, sel_ref, *, q_scale, top_k):
    i = pl.program_id(2)
    tq = q_ref.shape[1]
    d = NSA_HEAD_DIM
    ncp = kc_ref.shape[2]
    nsel = ncp // SLC_RATIO
    kc, vc = kc_ref[0, 0], vc_ref[0, 0]
    t = i * tq + lax.broadcasted_iota(jnp.int32, (tq, ncp), 0)
    col = lax.broadcasted_iota(jnp.int32, (tq, ncp), 1)
    cblk = SLC_RATIO * (col & (nsel - 1)) + (col >> (nsel.bit_length() - 1))
    valid = cblk * CMP_STRIDE + (CMP_BLOCK - 1) <= t
    imp = jnp.zeros((tq, ncp), F32)
    for g in range(NSA_GROUP):
        qg = (q_ref[0, :, g * d:(g + 1) * d] * q_scale).astype(MXU_DTYPE)
        s = lax.dot_general(qg, kc, (((1,), (1,)), ((), ())), preferred_element_type=F32)
        s = jnp.where(valid, s, NEG)
        e = jnp.where(valid, jnp.exp(s - jnp.max(s, axis=-1, keepdims=True)), 0.0)
        denom = jnp.sum(e, axis=-1, keepdims=True)
        p = e * (1.0 / jnp.where(denom > 0.0, denom, 1.0))
        o_ref[0, :, g * d:(g + 1) * d] = jnp.dot(p.astype(MXU_DTYPE), vc, preferred_element_type=F32)
        imp = imp + p
    ph = [imp[:, r * nsel:(r + 1) * nsel] for r in range(SLC_RATIO)]
    jcol = lax.broadcasted_iota(jnp.int32, (tq, nsel), 1)
    prev_last = jnp.where(jcol == 0, 0.0, pltpu.roll(ph[SLC_RATIO - 1], 1, 1))
    p_slc = prev_last
    for r in range(SLC_RATIO):
        p_slc = p_slc + ph[r]
    tj = i * tq + lax.broadcasted_iota(jnp.int32, (tq, nsel), 0)
    cur = tj >> (SLC_BLOCK.bit_length() - 1)
    forced = (jcol == 0) | (jcol == cur) | (jcol == cur - 1)
    causal = jcol <= cur
    score = jnp.where(forced, SEL_FORCE, jnp.where(causal, p_slc, NEG))
    jf = jcol.astype(F32)
    picked = jnp.zeros((tq, nsel), F32)
    for _ in range(top_k):
        m = jnp.max(score, axis=-1, keepdims=True)
        first = jnp.min(jnp.where(score == m, jf, float(nsel)), axis=-1, keepdims=True)
        hit = jf == first
        picked = jnp.where(hit, 1.0, picked)
        score = jnp.where(hit, PICKED, score)
    sel_ref[0, 0] = jnp.where((picked > 0.0) & causal, 0.0, MASK_BIAS).astype(sel_ref.dtype)


def nsa_cmp_branch(qproj, kc, vc, *, tq=128):
    B, S, _ = qproj.shape
    Hkv, G, d = NSA_KV_HEADS, NSA_GROUP, NSA_HEAD_DIM
    ncp = kc.shape[2]
    nsel = ncp // SLC_RATIO
    tq = min(tq, S)
    assert S % tq == 0 and nsel & (nsel - 1) == 0
    kern = functools.partial(_nsa_cmp_kernel, q_scale=d ** -0.5, top_k=min(SLC_TOPK, nsel))
    return pl.pallas_call(
        kern,
        grid=(B, Hkv, S // tq),
        in_specs=[
            pl.BlockSpec((1, tq, G * d), lambda b, h, i: (b, i, h)),
            pl.BlockSpec((1, 1, ncp, d), lambda b, h, i: (b, h, 0, 0)),
            pl.BlockSpec((1, 1, ncp, d), lambda b, h, i: (b, h, 0, 0)),
        ],
        out_specs=[
            pl.BlockSpec((1, tq, G * d), lambda b, h, i: (b, i, h)),
            pl.BlockSpec((1, 1, tq, nsel), lambda b, h, i: (b, h, i, 0)),
        ],
        out_shape=[
            jax.ShapeDtypeStruct((B, S, Hkv * G * d), F32),
            jax.ShapeDtypeStruct((B, Hkv, S, nsel), MXU_DTYPE),
        ],
        compiler_params=_cparams("arbitrary", "arbitrary", "arbitrary"),
        name="nsa_cmp",
    )(qproj, kc, vc)


def _nsa_slc_kernel(q_ref, sel_ref, k_ref, vt_ref, o_ref, qa_ref, m_ref, l_ref, acc_ref, *, q_scale, tk):
    i = pl.program_id(2)
    tq = q_ref.shape[1]
    d = NSA_HEAD_DIM
    G = NSA_GROUP
    nblk = sel_ref.shape[3]
    span = LANE * SLC_BLOCK
    n_halves = pl.cdiv(nblk, LANE)

    m_ref[...] = jnp.full_like(m_ref, PICKED)
    l_ref[...] = jnp.zeros_like(l_ref)
    acc_ref[...] = jnp.zeros_like(acc_ref)
    for g in range(G):
        qa_ref[:d, g * tq:(g + 1) * tq] = (q_ref[0, :, g * d:(g + 1) * d] * q_scale).T.astype(qa_ref.dtype)

    def attend(kt_idx, diagonal):
        k0 = pl.multiple_of(kt_idx * tk, tk)
        s = jnp.dot(k_ref[0, 0, pl.ds(k0, tk), :], qa_ref[...], preferred_element_type=F32)
        if diagonal:
            col = lax.broadcasted_iota(jnp.int32, s.shape, 1)
            tok = i * tq + (col & (tq - 1))
            key = k0 + lax.broadcasted_iota(jnp.int32, s.shape, 0)
            s = jnp.where(key <= tok, s, NEG)
        m_old = m_ref[...]
        m_new = jnp.maximum(m_old, jnp.max(s, axis=0, keepdims=True))
        alpha = jnp.exp(m_old - m_new)
        p = jnp.exp(s - m_new)
        l_ref[...] = alpha * l_ref[...] + jnp.sum(p, axis=0, keepdims=True)
        acc_ref[...] = alpha * acc_ref[...] + jnp.dot(vt_ref[0, 0, :, pl.ds(k0, tk)], p.astype(MXU_DTYPE),
                                                      preferred_element_type=F32)
        m_ref[...] = m_new

    last = ((i + 1) * tq - 1) // tk
    tiles_per_half = span // tk
    for hf in range(n_halves):
        lo = hf * tiles_per_half
        hi = jnp.minimum(last, lo + tiles_per_half)

        @pl.when(lo <= last)
        def _():
            sel_half = sel_ref[0, 0, :, hf * LANE:(hf + 1) * LANE].astype(F32).T.astype(qa_ref.dtype)
            for g in range(G):
                qa_ref[d:, g * tq:(g + 1) * tq] = sel_half

            def body(kt_idx, carry):
                attend(kt_idx, False)
                return carry

            lax.fori_loop(lo, hi, body, 0)

            @pl.when(last < lo + tiles_per_half)
            def _():
                attend(last, True)

    out = acc_ref[...] * (1.0 / l_ref[...])
    for g in range(G):
        o_ref[0, :, g * d:(g + 1) * d] = out[:, g * tq:(g + 1) * tq].T


def nsa_slc_branch(qproj, sel, k_aug, vt, *, tq=256, tk=512):
    B, S, _ = qproj.shape
    Hkv, G, d = NSA_KV_HEADS, NSA_GROUP, NSA_HEAD_DIM
    nblk = sel.shape[3]
    tq, tk = min(tq, S), min(tk, S)
    assert S % tq == 0 and S % tk == 0 and tk % tq == 0 and tq & (tq - 1) == 0
    assert nblk % LANE == 0 and (LANE * SLC_BLOCK) % tk == 0
    kern = functools.partial(_nsa_slc_kernel, q_scale=d ** -0.5, tk=tk)
    return pl.pallas_call(
        kern,
        grid=(B, Hkv, S // tq),
        in_specs=[
            pl.BlockSpec((1, tq, G * d), lambda b, h, i: (b, i, h)),
            pl.BlockSpec((1, 1, tq, nblk), lambda b, h, i: (b, h, i, 0)),
            pl.BlockSpec((1, 1, S, d + LANE), lambda b, h, i: (b, h, 0, 0)),
            pl.BlockSpec((1, 1, d, S), lambda b, h, i: (b, h, 0, 0)),
        ],
        out_specs=pl.BlockSpec((1, tq, G * d), lambda b, h, i: (b, i, h)),
        out_shape=jax.ShapeDtypeStruct((B, S, Hkv * G * d), F32),
        scratch_shapes=[
            pltpu.VMEM((d + LANE, G * tq), MXU_DTYPE),
            pltpu.VMEM((1, G * tq), F32),
            pltpu.VMEM((1, G * tq), F32),
            pltpu.VMEM((d, G * tq), F32),
        ],
        compiler_params=_cparams("arbitrary", "arbitrary", "arbitrary"),
        name="nsa_slc",
    )(qproj, sel, k_aug, vt)


def _nsa_win_kernel(q_ref, kt_ref, v_ref, o_ref, *, q_scale):
    i = pl.program_id(2)
    tq = q_ref.shape[1]
    d = NSA_HEAD_DIM
    nk = WIN + tq
    k0 = pl.multiple_of(i * tq, tq)
    kt = kt_ref[0, 0, :, pl.ds(k0, nk)]
    v = v_ref[0, 0, pl.ds(k0, nk), :]
    tok = i * tq + lax.broadcasted_iota(jnp.int32, (tq, nk), 0)
    pos = i * tq - WIN + lax.broadcasted_iota(jnp.int32, (tq, nk), 1)
    mask = (pos <= tok) & (pos > tok - WIN) & (pos >= 0)
    for g in range(NSA_GROUP):
        qg = (q_ref[0, :, g * d:(g + 1) * d] * q_scale).astype(MXU_DTYPE)
        s = jnp.where(mask, jnp.dot(qg, kt, preferred_element_type=F32), NEG)
        e = jnp.exp(s - jnp.max(s, axis=-1, keepdims=True))
        p = e * (1.0 / jnp.sum(e, axis=-1, keepdims=True))
        o_ref[0, :, g * d:(g + 1) * d] = jnp.dot(p.astype(MXU_DTYPE), v, preferred_element_type=F32)


def nsa_win_branch(qproj, kt_pad, v_pad, *, tq=256):
    B, S, _ = qproj.shape
    Hkv, G, d = NSA_KV_HEADS, NSA_GROUP, NSA_HEAD_DIM
    tq = min(tq, S)
    assert S % tq == 0 and tq % LANE == 0
    kern = functools.partial(_nsa_win_kernel, q_scale=d ** -0.5)
    return pl.pallas_call(
        kern,
        grid=(B, Hkv, S // tq),
        in_specs=[
            pl.BlockSpec((1, tq, G * d), lambda b, h, i: (b, i, h)),
            pl.BlockSpec((1, 1, d, WIN + S), lambda b, h, i: (b, h, 0, 0)),
            pl.BlockSpec((1, 1, WIN + S, d), lambda b, h, i: (b, h, 0, 0)),
        ],
        out_specs=pl.BlockSpec((1, tq, G * d), lambda b, h, i: (b, i, h)),
        out_shape=jax.ShapeDtypeStruct((B, S, Hkv * G * d), F32),
        compiler_params=_cparams("arbitrary", "arbitrary", "arbitrary"),
        name="nsa_win",
    )(qproj, kt_pad, v_pad)


def _nsa_out_kernel(oc_ref, os_ref, ow_ref, gate_ref, w_ref, h_ref, o_ref, merged_ref):
    d = NSA_HEAD_DIM
    gates = _sigmoid(gate_ref[...])
    for hq in range(NSA_Q_HEADS):
        cols = slice(hq * d, (hq + 1) * d)
        c = N_BRANCH * hq
        merged = (gates[:, c:c + 1] * oc_ref[:, cols] + gates[:, c + 1:c + 2] * os_ref[:, cols]
                  + gates[:, c + 2:c + 3] * ow_ref[:, cols])
        merged_ref[:, cols] = merged.astype(merged_ref.dtype)
    o_ref[...] = h_ref[...] + jnp.dot(merged_ref[...], w_ref[...], preferred_element_type=F32)


def nsa_out(o_cmp, o_slc, o_win, qproj, w_o, h, *, tm=256):
    T, HD = o_cmp.shape
    D = w_o.shape[1]
    tm = min(tm, T)
    assert T % tm == 0 and HD % LANE == 0
    gate_blk = HD // LANE
    row = lambda i: (i, 0)
    return pl.pallas_call(
        _nsa_out_kernel,
        grid=(T // tm,),
        in_specs=[
            pl.BlockSpec((tm, HD), row),
            pl.BlockSpec((tm, HD), row),
            pl.BlockSpec((tm, HD), row),
            pl.BlockSpec((tm, LANE), lambda i: (i, gate_blk)),
            pl.BlockSpec((HD, D), lambda i: (0, 0)),
            pl.BlockSpec((tm, D), row),
        ],
        out_specs=pl.BlockSpec((tm, D), row),
        out_shape=jax.ShapeDtypeStruct((T, D), F32),
        scratch_shapes=[pltpu.VMEM((tm, HD), w_o.dtype)],
        compiler_params=_cparams("arbitrary"),
        name="nsa_out",
    )(o_cmp, o_slc, o_win, qproj, w_o, h)


def _phase_major(x, nsel):
    B, n, H, d = x.shape
    return x.reshape(B, nsel, SLC_RATIO, H, d).transpose(0, 3, 2, 1, 4).reshape(B, H, n, d)


def _cast(w):
    return w.astype(MXU_DTYPE)


def nsa_shared_kv(h, B, S, kv_gain, w_kv, cmp_pe_k, cmp_w1_k, cmp_w2_k, cmp_pe_v, cmp_w1_v, cmp_w2_v):
    Hkv, d = NSA_KV_HEADS, NSA_HEAD_DIM
    kv = norm_matmul(h, kv_gain, _cast(w_kv), tm=1024, tn=1024)
    kv = kv.reshape(B, S, 2 * N_BRANCH, Hkv, d)
    n = S // CMP_STRIDE
    nsel = S // SLC_BLOCK
    xc = kv[:, :, 0:2].reshape(B, n, CMP_STRIDE, 2, Hkv, d).transpose(3, 0, 4, 1, 2, 5)
    xc = xc.reshape(2, B * Hkv, n, CMP_STRIDE * d)
    pe = jnp.stack([cmp_pe_k, cmp_pe_v]).reshape(2, CMP_RATIO, CMP_STRIDE * d)
    cmp = compress_tokens(xc, pe, _cast(jnp.stack([cmp_w1_k, cmp_w1_v])), _cast(jnp.stack([cmp_w2_k, cmp_w2_v])))
    cmp = cmp.reshape(2, B, Hkv, n, d).transpose(0, 1, 3, 2, 4)
    kc = _cast(_phase_major(cmp[0], nsel))
    vc = _cast(_phase_major(cmp[1], nsel))
    heads_first = lambda t: t.transpose(0, 2, 1, 3)
    k_slc, v_slc = heads_first(kv[:, :, 2]), heads_first(kv[:, :, 3])
    k_win, v_win = heads_first(kv[:, :, 4]), heads_first(kv[:, :, 5])
    blk = (jnp.arange(S) // SLC_BLOCK) % LANE
    onehot = (blk[:, None] == jnp.arange(LANE)[None, :]).astype(MXU_DTYPE)
    k_slc = jnp.concatenate([_cast(k_slc), jnp.broadcast_to(onehot, (B, Hkv, S, LANE))], axis=3)
    vt_slc = _cast(v_slc.transpose(0, 1, 3, 2))
    kt_win = jnp.pad(_cast(k_win.transpose(0, 1, 3, 2)), ((0, 0), (0, 0), (0, 0), (WIN, 0)))
    v_win = jnp.pad(_cast(v_win), ((0, 0), (0, 0), (WIN, 0), (0, 0)))
    return kc, vc, k_slc, vt_slc, kt_win, v_win


def nsa_attention_block(h, B, S, mix_gain, w_q, w_o, shared):
    kc, vc, kt_slc, v_slc, kt_win, v_win = shared
    T = B * S
    nq = NSA_Q_HEADS * NSA_HEAD_DIM
    nqp = nq + LANE
    w_q = jnp.pad(w_q, ((0, 0), (0, nqp - w_q.shape[1])))
    qproj = norm_matmul(h, mix_gain, _cast(w_q), tm=512, tn=nqp)
    qp3 = qproj.reshape(B, S, nqp)
    o_cmp, sel = nsa_cmp_branch(qp3, kc, vc)
    o_slc = nsa_slc_branch(qp3, sel, kt_slc, v_slc)
    o_win = nsa_win_branch(qp3, kt_win, v_win)
    return nsa_out(o_cmp.reshape(T, nq), o_slc.reshape(T, nq), o_win.reshape(T, nq), qproj, _cast(w_o), h)


def nsa_mixer(h, B, S, mix_gain, kv_gain, w_kv, cmp_pe_k, cmp_w1_k, cmp_w2_k, cmp_pe_v, cmp_w1_v, cmp_w2_v, w_q, w_o):
    shared = nsa_shared_kv(h, B, S, kv_gain, w_kv, cmp_pe_k, cmp_w1_k, cmp_w2_k, cmp_pe_v, cmp_w1_v, cmp_w2_v)
    return nsa_attention_block(h, B, S, mix_gain, w_q, w_o, shared)


def kernel(x, norm_mix_gain, norm_ffn_gain, ret_w_in, ret_gn_gain, ret_w_out, nsa_kv_norm_gain, nsa_w_kv, cmp_pe_k, cmp_w1_k, cmp_w2_k, cmp_pe_v, cmp_w1_v, cmp_w2_v, nsa_w_q, nsa_w_o, ffn_w_in, ffn_conv_w, ffn_conv_b, ffn_w_out, final_norm_gain):
    B, S, D = x.shape
    T = B * S
    depth = norm_mix_gain.shape[0]
    n_a = depth // 2
    h = x.reshape(T, D)
    shared = None
    for layer in range(depth):
        if layer < n_a:
            proj = norm_matmul(h, norm_mix_gain[layer], _cast(ret_w_in[layer]), tm=1024, tn=1024)
            y = retention_core(proj.reshape(B, S, -1), ret_gn_gain[layer])
            h = matmul_residual(y.reshape(T, -1), _cast(ret_w_out[layer]), h, tm=512, tn=1024)
        else:
            if layer == n_a:
                shared = nsa_shared_kv(h, B, S, nsa_kv_norm_gain, nsa_w_kv, cmp_pe_k, cmp_w1_k, cmp_w2_k,
                                       cmp_pe_v, cmp_w1_v, cmp_w2_v)
            b = layer - n_a
            h = nsa_attention_block(h, B, S, norm_mix_gain[layer], nsa_w_q[b], nsa_w_o[b], shared)
        h = conv_ffn_block(h, S, norm_ffn_gain[layer], _cast(ffn_w_in[layer]), ffn_conv_w[layer], ffn_conv_b[layer],
                           _cast(ffn_w_out[layer]), final_norm_gain, final_norm=(layer == depth - 1))
    return h.reshape(B, S, D)
```

```python
import functools

import jax
import jax.numpy as jnp
from jax import lax
from jax.experimental import pallas as pl
from jax.experimental.pallas import tpu as pltpu

F32 = jnp.float32
MXU_DTYPE = jnp.bfloat16

RMS_EPS = 1e-6
ROPE_BASE = 10000.0

RET_HEADS = 8
RET_CHUNK = 128

NSA_Q_HEADS = 16
NSA_KV_HEADS = 4
NSA_GROUP = NSA_Q_HEADS // NSA_KV_HEADS
NSA_HEAD_DIM = 128
N_BRANCH = 3
CMP_BLOCK = 32
CMP_STRIDE = 16
CMP_RATIO = CMP_BLOCK // CMP_STRIDE
SLC_BLOCK = 64
SLC_TOPK = 16
SLC_RATIO = SLC_BLOCK // CMP_STRIDE
WIN = 512
CONV_W = 3

NEG = -1e30
MASK_BIAS = -(2.0 ** 100)
PICKED = -3.0e38
SEL_FORCE = 1e30

LANE = 128
BF16_SUBLANES = 16
VMEM_LIMIT_BYTES = 56 * 1024 * 1024


def _cparams(*sem):
    return pltpu.CompilerParams(dimension_semantics=sem, vmem_limit_bytes=VMEM_LIMIT_BYTES)


def _sigmoid(x):
    return 1.0 / (1.0 + jnp.exp(-x))


def _rms_normalize(x, gain):
    ms = jnp.mean(x * x, axis=-1, keepdims=True)
    return x * lax.rsqrt(ms + RMS_EPS) * gain


def _norm_matmul_kernel(x_ref, g_ref, w_ref, o_ref, xn_ref):
    @pl.when(pl.program_id(1) == 0)
    def _():
        xn_ref[...] = _rms_normalize(x_ref[...], g_ref[...]).astype(xn_ref.dtype)

    o_ref[...] = jnp.dot(xn_ref[...], w_ref[...], preferred_element_type=F32).astype(o_ref.dtype)


def norm_matmul(h, gain, w, *, tm, tn, out_dtype=F32):
    T, D = h.shape
    N = w.shape[1]
    tm, tn = min(tm, T), min(tn, N)
    assert T % tm == 0 and N % tn == 0
    return pl.pallas_call(
        _norm_matmul_kernel,
        grid=(T // tm, N // tn),
        in_specs=[
            pl.BlockSpec((tm, D), lambda i, j: (i, 0)),
            pl.BlockSpec((1, D), lambda i, j: (0, 0)),
            pl.BlockSpec((D, tn), lambda i, j: (0, j)),
        ],
        out_specs=pl.BlockSpec((tm, tn), lambda i, j: (i, j)),
        out_shape=jax.ShapeDtypeStruct((T, N), out_dtype),
        scratch_shapes=[pltpu.VMEM((tm, D), w.dtype)],
        compiler_params=_cparams("arbitrary", "arbitrary"),
        name="norm_matmul",
    )(h, gain.reshape(1, D), w)


def _matmul_residual_kernel(y_ref, w_ref, h_ref, o_ref):
    o_ref[...] = h_ref[...] + jnp.dot(y_ref[...], w_ref[...], preferred_element_type=F32)


def matmul_residual(y, w, h, *, tm, tn):
    T, K = y.shape
    N = w.shape[1]
    tm, tn = min(tm, T), min(tn, N)
    assert T % tm == 0 and N % tn == 0
    return pl.pallas_call(
        _matmul_residual_kernel,
        grid=(T // tm, N // tn),
        in_specs=[
            pl.BlockSpec((tm, K), lambda i, j: (i, 0)),
            pl.BlockSpec((K, tn), lambda i, j: (0, j)),
            pl.BlockSpec((tm, tn), lambda i, j: (i, j)),
        ],
        out_specs=pl.BlockSpec((tm, tn), lambda i, j: (i, j)),
        out_shape=jax.ShapeDtypeStruct((T, N), F32),
        compiler_params=_cparams("arbitrary", "arbitrary"),
        name="matmul_residual",
    )(y, w, h)


def _retention_kernel(q_ref, k_ref, v_ref, g_ref, cos_ref, sin_ref, dmask_ref, qdec_ref, kdec_ref, gn_ref,
                      o_ref, state_ref, *, chunk, n_chunks, k_scale):
    @pl.when(pl.program_id(2) == 0)
    def _():
        state_ref[...] = jnp.zeros_like(state_ref)

    half = q_ref.shape[-1] // 2
    dmask = dmask_ref[0]
    qdec = qdec_ref[0]
    kdec = kdec_ref[0]
    chunk_decay = qdec[chunk - 1:chunk, :]
    gn = gn_ref[...]

    def rotate(x, cos, sin):
        x1, x2 = x[:, :half], x[:, half:]
        return jnp.concatenate([x1 * cos - x2 * sin, x1 * sin + x2 * cos], axis=-1)

    for ci in range(n_chunks):
        rows = pl.ds(ci * chunk, chunk)
        cos, sin = cos_ref[rows, :], sin_ref[rows, :]
        q = rotate(q_ref[0, rows, :], cos, sin)
        k = rotate(k_ref[0, rows, :], cos, sin) * k_scale
        v = v_ref[0, rows, :].astype(MXU_DTYPE)
        qm = q.astype(MXU_DTYPE)
        scores = lax.dot_general(qm, k.astype(MXU_DTYPE), (((1,), (1,)), ((), ())),
                                 preferred_element_type=F32) * dmask
        inner = jnp.dot(scores.astype(MXU_DTYPE), v, preferred_element_type=F32)
        state = state_ref[...]
        cross = jnp.dot(qm, state.astype(MXU_DTYPE), preferred_element_type=F32) * qdec
        k_dec_t = (k * kdec).T.astype(MXU_DTYPE)
        state_ref[...] = state * chunk_decay + jnp.dot(k_dec_t, v, preferred_element_type=F32)
        out = inner + cross
        mu = jnp.mean(out, axis=-1, keepdims=True)
        cen = out - mu
        var = jnp.mean(cen * cen, axis=-1, keepdims=True)
        normed = cen * lax.rsqrt(var + RMS_EPS) * gn
        g = g_ref[0, rows, :]
        o_ref[0, rows, :] = ((g * _sigmoid(g)) * normed).astype(o_ref.dtype)


def retention_core(proj, gn_gain, *, chunk=RET_CHUNK, tokens_per_step=512):
    B, S, P = proj.shape
    H = RET_HEADS
    dk = P // (6 * H)
    dv = 2 * dk
    tc = min(tokens_per_step, S)
    assert S % tc == 0 and tc % chunk == 0
    half = dk // 2
    pos = jnp.arange(S, dtype=F32)
    freqs = ROPE_BASE ** (-jnp.arange(half, dtype=F32) / half)
    ang = pos[:, None] * freqs[None, :]
    cos, sin = jnp.cos(ang), jnp.sin(ang)
    log_gamma = jnp.log(1.0 - 2.0 ** (-5.0 - jnp.arange(H, dtype=F32)))
    idx = jnp.arange(chunk, dtype=F32)
    diff = idx[:, None] - idx[None, :]
    dmask = jnp.where(diff >= 0, jnp.exp(jnp.maximum(diff, 0.0)[None] * log_gamma[:, None, None]), 0.0)
    qdec = jnp.exp((idx[None, :] + 1.0) * log_gamma[:, None])[:, :, None]
    kdec = jnp.exp((chunk - 1.0 - idx[None, :]) * log_gamma[:, None])[:, :, None]
    kern = functools.partial(_retention_kernel, chunk=chunk, n_chunks=tc // chunk, k_scale=dk ** -0.5)
    k_off, v_off, g_off = (H * dk) // dk, (2 * H * dk) // dv, (2 * H * dk + H * dv) // dv
    return pl.pallas_call(
        kern,
        grid=(B, H, S // tc),
        in_specs=[
            pl.BlockSpec((1, tc, dk), lambda b, h, c: (b, c, h)),
            pl.BlockSpec((1, tc, dk), lambda b, h, c: (b, c, k_off + h)),
            pl.BlockSpec((1, tc, dv), lambda b, h, c: (b, c, v_off + h)),
            pl.BlockSpec((1, tc, dv), lambda b, h, c: (b, c, g_off + h)),
            pl.BlockSpec((tc, half), lambda b, h, c: (c, 0)),
            pl.BlockSpec((tc, half), lambda b, h, c: (c, 0)),
            pl.BlockSpec((1, chunk, chunk), lambda b, h, c: (h, 0, 0)),
            pl.BlockSpec((1, chunk, 1), lambda b, h, c: (h, 0, 0)),
            pl.BlockSpec((1, chunk, 1), lambda b, h, c: (h, 0, 0)),
            pl.BlockSpec((1, dv), lambda b, h, c: (0, h)),
        ],
        out_specs=pl.BlockSpec((1, tc, dv), lambda b, h, c: (b, c, h)),
        out_shape=jax.ShapeDtypeStruct((B, S, H * dv), MXU_DTYPE),
        scratch_shapes=[pltpu.VMEM((dk, dv), F32)],
        compiler_params=_cparams("arbitrary", "arbitrary", "arbitrary"),
        name="retention_core",
    )(proj, proj, proj, proj, cos, sin, dmask, qdec, kdec, gn_gain.reshape(1, H * dv))


def _conv_ffn_kernel(x_ref, g_ref, wa_ref, wb_ref, cwa_ref, cwb_ref, cba_ref, cbb_ref, wo_ref, fg_ref,
                     o_ref, xn_ref, acc_ref, carry_a_ref, carry_b_ref, *, tiles_per_seq, final_norm):
    i, j = pl.program_id(0), pl.program_id(1)
    tm = x_ref.shape[0]

    @pl.when((i == 0) & (j == 0))
    def _():
        carry_a_ref[...] = jnp.zeros_like(carry_a_ref)
        carry_b_ref[...] = jnp.zeros_like(carry_b_ref)

    @pl.when(j == 0)
    def _():
        xn_ref[...] = _rms_normalize(x_ref[...], g_ref[...]).astype(xn_ref.dtype)
        acc_ref[...] = jnp.zeros_like(acc_ref)

    seq_start = (i % tiles_per_seq) == 0
    xn = xn_ref[...]

    def causal_conv(w_ref, cw_ref, cb_ref, carry_ref):
        u = jnp.dot(xn, w_ref[...], preferred_element_type=F32)
        prev = jnp.where(seq_start, 0.0, carry_ref[j])
        carry_ref[j] = u[tm - 8:, :]
        row = lax.broadcasted_iota(jnp.int32, u.shape, 0)
        u1 = jnp.where(row == 0, prev[7:8, :], pltpu.roll(u, 1, 0))
        u2 = jnp.where(row == 0, prev[6:7, :], jnp.where(row == 1, prev[7:8, :], pltpu.roll(u, 2, 0)))
        cw = cw_ref[...]
        return ((cb_ref[...] + u2 * cw[0:1, :]) + u1 * cw[1:2, :]) + u * cw[2:3, :]

    a = causal_conv(wa_ref, cwa_ref, cba_ref, carry_a_ref)
    b = causal_conv(wb_ref, cwb_ref, cbb_ref, carry_b_ref)
    act = ((a * _sigmoid(a)) * b).astype(wo_ref.dtype)
    acc_ref[...] += jnp.dot(act, wo_ref[...], preferred_element_type=F32)

    @pl.when(j == pl.num_programs(1) - 1)
    def _():
        out = x_ref[...] + acc_ref[...]
        if final_norm:
            out = _rms_normalize(out, fg_ref[...])
        o_ref[...] = out


def conv_ffn_block(h, seq_len, norm_gain, w_in, conv_w, conv_b, w_out, final_gain, *, final_norm, tm=512, tn=512):
    T, D = h.shape
    F = w_out.shape[0]
    tm, tn = min(tm, seq_len), min(tn, F)
    assert seq_len % tm == 0 and F % tn == 0 and tm >= 8
    nj = F // tn
    kern = functools.partial(_conv_ffn_kernel, tiles_per_seq=seq_len // tm, final_norm=final_norm)
    return pl.pallas_call(
        kern,
        grid=(T // tm, nj),
        in_specs=[
            pl.BlockSpec((tm, D), lambda i, j: (i, 0)),
            pl.BlockSpec((1, D), lambda i, j: (0, 0)),
            pl.BlockSpec((D, tn), lambda i, j: (0, j)),
            pl.BlockSpec((D, tn), lambda i, j: (0, nj + j)),
            pl.BlockSpec((CONV_W, tn), lambda i, j: (0, j)),
            pl.BlockSpec((CONV_W, tn), lambda i, j: (0, nj + j)),
            pl.BlockSpec((1, tn), lambda i, j: (0, j)),
            pl.BlockSpec((1, tn), lambda i, j: (0, nj + j)),
            pl.BlockSpec((tn, D), lambda i, j: (j, 0)),
            pl.BlockSpec((1, D), lambda i, j: (0, 0)),
        ],
        out_specs=pl.BlockSpec((tm, D), lambda i, j: (i, 0)),
        out_shape=jax.ShapeDtypeStruct((T, D), F32),
        scratch_shapes=[
            pltpu.VMEM((tm, D), w_in.dtype),
            pltpu.VMEM((tm, D), F32),
            pltpu.VMEM((nj, 8, tn), F32),
            pltpu.VMEM((nj, 8, tn), F32),
        ],
        compiler_params=_cparams("arbitrary", "arbitrary"),
        name="conv_ffn",
    )(h, norm_gain.reshape(1, D), w_in, w_in, conv_w, conv_w, conv_b.reshape(1, 2 * F), conv_b.reshape(1, 2 * F),
      w_out, final_gain.reshape(1, D))


def _compress_kernel(x_ref, pe_ref, w1_ref, w2_ref, o_ref):
    x = x_ref[0, 0]
    n, kc = x.shape
    pe = pe_ref[0]
    first = jnp.dot((x + pe[0:1, :]).astype(MXU_DTYPE), w1_ref[0, :kc, :], preferred_element_type=F32)
    second = jnp.dot((x + pe[1:2, :]).astype(MXU_DTYPE), w1_ref[0, kc:, :], preferred_element_type=F32)
    hid = first + pltpu.roll(second, n - 1, 0)
    gelu = 0.5 * hid * (1.0 + jnp.tanh(0.7978845608028654 * (hid + 0.044715 * (hid * hid * hid))))
    out = jnp.dot(gelu.astype(MXU_DTYPE), w2_ref[0], preferred_element_type=F32)
    row = lax.broadcasted_iota(jnp.int32, out.shape, 0)
    o_ref[0, 0] = jnp.where(row < n - 1, out, 0.0)


def compress_tokens(x, pe, w1, w2):
    two, G, n, kc = x.shape
    hid, d = w2.shape[1], w2.shape[2]
    return pl.pallas_call(
        _compress_kernel,
        grid=(two, G),
        in_specs=[
            pl.BlockSpec((1, 1, n, kc), lambda s, g: (s, g, 0, 0)),
            pl.BlockSpec((1, CMP_RATIO, kc), lambda s, g: (s, 0, 0)),
            pl.BlockSpec((1, CMP_RATIO * kc, hid), lambda s, g: (s, 0, 0)),
            pl.BlockSpec((1, hid, d), lambda s, g: (s, 0, 0)),
        ],
        out_specs=pl.BlockSpec((1, 1, n, d), lambda s, g: (s, g, 0, 0)),
        out_shape=jax.ShapeDtypeStruct((two, G, n, d), F32),
        compiler_params=_cparams("arbitrary", "arbitrary"),
        name="compress_tokens",
    )(x, pe, w1, w2)


def _nsa_cmp_kernel(q_ref, kc_ref, vct_ref, o_ref, sel_ref, qt_ref, *, q_scale, top_k):
    i = pl.program_id(2)
    tq = q_ref.shape[1]
    d = NSA_HEAD_DIM
    G = NSA_GROUP
    ncp = kc_ref.shape[2]
    nsel = ncp // SLC_RATIO
    t0 = i * tq
    for g in range(G):
        qt_ref[:, g * tq:(g + 1) * tq] = (q_ref[0, :, g * d:(g + 1) * d] * q_scale).T.astype(qt_ref.dtype)
    s = jnp.dot(kc_ref[0, 0], qt_ref[...], preferred_element_type=F32)
    row = lax.broadcasted_iota(jnp.int32, s.shape, 0)
    tok = t0 + (lax.broadcasted_iota(jnp.int32, s.shape, 1) & (tq - 1))
    cblk = SLC_RATIO * (row & (nsel - 1)) + (row >> (nsel.bit_length() - 1))
    s = jnp.where(cblk * CMP_STRIDE + (CMP_BLOCK - 1) <= tok, s, NEG)
    e = jnp.exp(s - jnp.max(s, axis=0, keepdims=True))
    denom = jnp.sum(e, axis=0, keepdims=True)
    p = e * jnp.where(tok[0:1, :] >= CMP_BLOCK - 1, 1.0 / denom, 0.0)
    out_t = jnp.dot(vct_ref[0, 0], p.astype(MXU_DTYPE), preferred_element_type=F32)
    imp = None
    for g in range(G):
        o_ref[0, :, g * d:(g + 1) * d] = out_t[:, g * tq:(g + 1) * tq].T
        pg = p[:, g * tq:(g + 1) * tq]
        imp = pg if imp is None else imp + pg
    ph = [imp[r * nsel:(r + 1) * nsel, :] for r in range(SLC_RATIO)]
    j = lax.broadcasted_iota(jnp.int32, (nsel, tq), 0)
    p_slc = jnp.where(j == 0, 0.0, pltpu.roll(ph[SLC_RATIO - 1], 1, 0))
    for r in range(SLC_RATIO):
        p_slc = p_slc + ph[r]
    cur = (t0 + lax.broadcasted_iota(jnp.int32, (nsel, tq), 1)) >> (SLC_BLOCK.bit_length() - 1)
    causal = j <= cur

    @pl.when(t0 < top_k * SLC_BLOCK)
    def _():
        sel_ref[0, 0] = jnp.where(causal, 0.0, MASK_BIAS).astype(sel_ref.dtype)

    @pl.when(t0 >= top_k * SLC_BLOCK)
    def _():
        forced = (j == 0) | (j == cur) | (j == cur - 1)
        score = jnp.where(causal & jnp.logical_not(forced), p_slc, PICKED)
        jf = j.astype(F32)
        picked = jnp.where(forced, 1.0, 0.0)
        for _ in range(top_k - 3):
            m = jnp.max(score, axis=0, keepdims=True)
            first = jnp.min(jnp.where(score == m, jf, float(nsel)), axis=0, keepdims=True)
            hit = jf == first
            picked = jnp.where(hit, 1.0, picked)
            score = jnp.where(hit, PICKED, score)
        sel_ref[0, 0] = jnp.where((picked > 0.0) & causal, 0.0, MASK_BIAS).astype(sel_ref.dtype)


def nsa_cmp_branch(qproj, kc, vct, *, tq=128):
    B, S, _ = qproj.shape
    Hkv, G, d = NSA_KV_HEADS, NSA_GROUP, NSA_HEAD_DIM
    ncp = kc.shape[2]
    nsel = ncp // SLC_RATIO
    top_k = min(SLC_TOPK, nsel)
    tq = min(tq, S)
    assert S % tq == 0 and nsel & (nsel - 1) == 0 and tq & (tq - 1) == 0
    assert (top_k * SLC_BLOCK) % tq == 0 and top_k >= 3
    kern = functools.partial(_nsa_cmp_kernel, q_scale=d ** -0.5, top_k=top_k)
    return pl.pallas_call(
        kern,
        grid=(B, Hkv, S // tq),
        in_specs=[
            pl.BlockSpec((1, tq, G * d), lambda b, h, i: (b, i, h)),
            pl.BlockSpec((1, 1, ncp, d), lambda b, h, i: (b, h, 0, 0)),
            pl.BlockSpec((1, 1, d, ncp), lambda b, h, i: (b, h, 0, 0)),
        ],
        out_specs=[
            pl.BlockSpec((1, tq, G * d), lambda b, h, i: (b, i, h)),
            pl.BlockSpec((1, 1, nsel, tq), lambda b, h, i: (b, h, 0, i)),
        ],
        out_shape=[
            jax.ShapeDtypeStruct((B, S, Hkv * G * d), F32),
            jax.ShapeDtypeStruct((B, Hkv, nsel, S), MXU_DTYPE),
        ],
        scratch_shapes=[pltpu.VMEM((d, G * tq), MXU_DTYPE)],
        compiler_params=_cparams("arbitrary", "arbitrary", "arbitrary"),
        name="nsa_cmp",
    )(qproj, kc, vct)


def _nsa_slc_kernel(q_ref, sel_ref, k_ref, vt_ref, o_ref, qa_ref, m_ref, acc_ref, *, q_scale, tk):
    i = pl.program_id(2)
    tq = q_ref.shape[1]
    d = NSA_HEAD_DIM
    G = NSA_GROUP
    nblk = sel_ref.shape[2]
    span = LANE * SLC_BLOCK
    n_halves = pl.cdiv(nblk, LANE)

    m_ref[...] = jnp.full_like(m_ref, PICKED)
    acc_ref[...] = jnp.zeros_like(acc_ref)
    for g in range(G):
        qa_ref[:d, g * tq:(g + 1) * tq] = (q_ref[0, :, g * d:(g + 1) * d] * q_scale).T.astype(qa_ref.dtype)

    def attend(kt_idx, diagonal):
        k0 = pl.multiple_of(kt_idx * tk, tk)
        s = jnp.dot(k_ref[0, 0, pl.ds(k0, tk), :], qa_ref[...], preferred_element_type=F32)
        if diagonal:
            col = lax.broadcasted_iota(jnp.int32, s.shape, 1)
            tok = i * tq + (col & (tq - 1))
            key = k0 + lax.broadcasted_iota(jnp.int32, s.shape, 0)
            s = jnp.where(key <= tok, s, NEG)
        m_old = m_ref[...]
        m_new = jnp.maximum(m_old, jnp.max(s, axis=0, keepdims=True))
        alpha = jnp.exp(m_old - m_new)
        p = jnp.exp(s - m_new)
        acc_ref[...] = alpha * acc_ref[...] + jnp.dot(vt_ref[0, 0, :, pl.ds(k0, tk)], p.astype(MXU_DTYPE),
                                                      preferred_element_type=F32)
        m_ref[...] = m_new

    last = ((i + 1) * tq - 1) // tk
    tiles_per_half = span // tk
    for hf in range(n_halves):
        lo = hf * tiles_per_half
        hi = jnp.minimum(last, lo + tiles_per_half)

        @pl.when(lo <= last)
        def _():
            sel_half = sel_ref[0, 0, hf * LANE:(hf + 1) * LANE, :]
            for g in range(G):
                qa_ref[d:, g * tq:(g + 1) * tq] = sel_half

            def body(kt_idx, carry):
                attend(kt_idx, False)
                return carry

            lax.fori_loop(lo, hi, body, 0)

            @pl.when(last < lo + tiles_per_half)
            def _():
                attend(last, True)

    out = acc_ref[:d, :] * (1.0 / acc_ref[d:d + 1, :])
    for g in range(G):
        o_ref[0, :, g * d:(g + 1) * d] = out[:, g * tq:(g + 1) * tq].T


def nsa_slc_branch(qproj, sel, k_aug, vt, *, tq=256, tk=1024):
    B, S, _ = qproj.shape
    Hkv, G, d = NSA_KV_HEADS, NSA_GROUP, NSA_HEAD_DIM
    nblk = sel.shape[2]
    dv = vt.shape[2]
    tq, tk = min(tq, S), min(tk, S)
    assert S % tq == 0 and S % tk == 0 and tk % tq == 0 and tq & (tq - 1) == 0
    assert nblk % LANE == 0 and (LANE * SLC_BLOCK) % tk == 0
    kern = functools.partial(_nsa_slc_kernel, q_scale=d ** -0.5, tk=tk)
    return pl.pallas_call(
        kern,
        grid=(B, Hkv, S // tq),
        in_specs=[
            pl.BlockSpec((1, tq, G * d), lambda b, h, i: (b, i, h)),
            pl.BlockSpec((1, 1, nblk, tq), lambda b, h, i: (b, h, 0, i)),
            pl.BlockSpec((1, 1, S, d + LANE), lambda b, h, i: (b, h, 0, 0)),
            pl.BlockSpec((1, 1, dv, S), lambda b, h, i: (b, h, 0, 0)),
        ],
        out_specs=pl.BlockSpec((1, tq, G * d), lambda b, h, i: (b, i, h)),
        out_shape=jax.ShapeDtypeStruct((B, S, Hkv * G * d), F32),
        scratch_shapes=[
            pltpu.VMEM((d + LANE, G * tq), MXU_DTYPE),
            pltpu.VMEM((1, G * tq), F32),
            pltpu.VMEM((dv, G * tq), F32),
        ],
        compiler_params=_cparams("arbitrary", "arbitrary", "arbitrary"),
        name="nsa_slc",
    )(qproj, sel, k_aug, vt)


def _nsa_win_kernel(q_ref, k_ref, vt_ref, o_ref, qt_ref, *, q_scale):
    i = pl.program_id(2)
    tq = q_ref.shape[1]
    d = NSA_HEAD_DIM
    G = NSA_GROUP
    nk = WIN + tq
    k0 = pl.multiple_of(i * tq, tq)
    for g in range(G):
        qt_ref[:, g * tq:(g + 1) * tq] = (q_ref[0, :, g * d:(g + 1) * d] * q_scale).T.astype(qt_ref.dtype)
    s = jnp.dot(k_ref[0, 0, pl.ds(k0, nk), :], qt_ref[...], preferred_element_type=F32)
    pos = i * tq - WIN + lax.broadcasted_iota(jnp.int32, s.shape, 0)
    tok = i * tq + (lax.broadcasted_iota(jnp.int32, s.shape, 1) & (tq - 1))
    s = jnp.where((pos <= tok) & (pos > tok - WIN) & (pos >= 0), s, NEG)
    e = jnp.exp(s - jnp.max(s, axis=0, keepdims=True))
    p = e * (1.0 / jnp.sum(e, axis=0, keepdims=True))
    out_t = jnp.dot(vt_ref[0, 0, :, pl.ds(k0, nk)], p.astype(MXU_DTYPE), preferred_element_type=F32)
    for g in range(G):
        o_ref[0, :, g * d:(g + 1) * d] = out_t[:, g * tq:(g + 1) * tq].T


def nsa_win_branch(qproj, k_pad, vt_pad, *, tq=256):
    B, S, _ = qproj.shape
    Hkv, G, d = NSA_KV_HEADS, NSA_GROUP, NSA_HEAD_DIM
    tq = min(tq, S)
    assert S % tq == 0 and tq % LANE == 0 and tq & (tq - 1) == 0
    kern = functools.partial(_nsa_win_kernel, q_scale=d ** -0.5)
    return pl.pallas_call(
        kern,
        grid=(B, Hkv, S // tq),
        in_specs=[
            pl.BlockSpec((1, tq, G * d), lambda b, h, i: (b, i, h)),
            pl.BlockSpec((1, 1, WIN + S, d), lambda b, h, i: (b, h, 0, 0)),
            pl.BlockSpec((1, 1, d, WIN + S), lambda b, h, i: (b, h, 0, 0)),
        ],
        out_specs=pl.BlockSpec((1, tq, G * d), lambda b, h, i: (b, i, h)),
        out_shape=jax.ShapeDtypeStruct((B, S, Hkv * G * d), F32),
        scratch_shapes=[pltpu.VMEM((d, G * tq), MXU_DTYPE)],
        compiler_params=_cparams("arbitrary", "arbitrary", "arbitrary"),
        name="nsa_win",
    )(qproj, k_pad, vt_pad)


def _nsa_out_kernel(oc_ref, os_ref, ow_ref, gate_ref, w_ref, h_ref, o_ref, merged_ref):
    d = NSA_HEAD_DIM
    gates = _sigmoid(gate_ref[...])
    for hq in range(NSA_Q_HEADS):
        cols = slice(hq * d, (hq + 1) * d)
        c = N_BRANCH * hq
        merged = (gates[:, c:c + 1] * oc_ref[:, cols] + gates[:, c + 1:c + 2] * os_ref[:, cols]
                  + gates[:, c + 2:c + 3] * ow_ref[:, cols])
        merged_ref[:, cols] = merged.astype(merged_ref.dtype)
    o_ref[...] = h_ref[...] + jnp.dot(merged_ref[...], w_ref[...], preferred_element_type=F32)


def nsa_out(o_cmp, o_slc, o_win, qproj, w_o, h, *, tm=256):
    T, HD = o_cmp.shape
    D = w_o.shape[1]
    tm = min(tm, T)
    assert T % tm == 0 and HD % LANE == 0
    gate_blk = HD // LANE
    row = lambda i: (i, 0)
    return pl.pallas_call(
        _nsa_out_kernel,
        grid=(T // tm,),
        in_specs=[
            pl.BlockSpec((tm, HD), row),
            pl.BlockSpec((tm, HD), row),
            pl.BlockSpec((tm, HD), row),
            pl.BlockSpec((tm, LANE), lambda i: (i, gate_blk)),
            pl.BlockSpec((HD, D), lambda i: (0, 0)),
            pl.BlockSpec((tm, D), row),
        ],
        out_specs=pl.BlockSpec((tm, D), row),
        out_shape=jax.ShapeDtypeStruct((T, D), F32),
        scratch_shapes=[pltpu.VMEM((tm, HD), w_o.dtype)],
        compiler_params=_cparams("arbitrary"),
        name="nsa_out",
    )(o_cmp, o_slc, o_win, qproj, w_o, h)


def _phase_major(x, nsel):
    B, n, H, d = x.shape
    return x.reshape(B, nsel, SLC_RATIO, H, d).transpose(0, 3, 2, 1, 4).reshape(B, H, n, d)


def _cast(w):
    return w.astype(MXU_DTYPE)


def nsa_shared_kv(h, B, S, kv_gain, w_kv, cmp_pe_k, cmp_w1_k, cmp_w2_k, cmp_pe_v, cmp_w1_v, cmp_w2_v):
    Hkv, d = NSA_KV_HEADS, NSA_HEAD_DIM
    kv = norm_matmul(h, kv_gain, _cast(w_kv), tm=1024, tn=1024)
    kv = kv.reshape(B, S, 2 * N_BRANCH, Hkv, d)
    n = S // CMP_STRIDE
    nsel = S // SLC_BLOCK
    xc = kv[:, :, 0:2].reshape(B, n, CMP_STRIDE, 2, Hkv, d).transpose(3, 0, 4, 1, 2, 5)
    xc = xc.reshape(2, B * Hkv, n, CMP_STRIDE * d)
    pe = jnp.stack([cmp_pe_k, cmp_pe_v]).reshape(2, CMP_RATIO, CMP_STRIDE * d)
    cmp = compress_tokens(xc, pe, _cast(jnp.stack([cmp_w1_k, cmp_w1_v])), _cast(jnp.stack([cmp_w2_k, cmp_w2_v])))
    cmp = cmp.reshape(2, B, Hkv, n, d).transpose(0, 1, 3, 2, 4)
    kc = _cast(_phase_major(cmp[0], nsel))
    vct = _cast(_phase_major(cmp[1], nsel).transpose(0, 1, 3, 2))
    heads_first = lambda t: t.transpose(0, 2, 1, 3)
    dims_first = lambda t: t.transpose(0, 2, 3, 1)
    blk = (jnp.arange(S) // SLC_BLOCK) % LANE
    onehot = (blk[:, None] == jnp.arange(LANE)[None, :]).astype(MXU_DTYPE)
    k_slc = jnp.concatenate([_cast(heads_first(kv[:, :, 2])), jnp.broadcast_to(onehot, (B, Hkv, S, LANE))], axis=3)
    vt_slc = jnp.concatenate([_cast(dims_first(kv[:, :, 3])), jnp.ones((B, Hkv, BF16_SUBLANES, S), MXU_DTYPE)],
                             axis=2)
    k_win = jnp.pad(_cast(heads_first(kv[:, :, 4])), ((0, 0), (0, 0), (WIN, 0), (0, 0)))
    vt_win = jnp.pad(_cast(dims_first(kv[:, :, 5])), ((0, 0), (0, 0), (0, 0), (WIN, 0)))
    return kc, vct, k_slc, vt_slc, k_win, vt_win


def nsa_attention_block(h, B, S, mix_gain, w_q, w_o, shared):
    kc, vct, k_slc, vt_slc, k_win, vt_win = shared
    T = B * S
    nq = NSA_Q_HEADS * NSA_HEAD_DIM
    nqp = nq + LANE
    w_q = jnp.pad(w_q, ((0, 0), (0, nqp - w_q.shape[1])))
    qproj = norm_matmul(h, mix_gain, _cast(w_q), tm=512, tn=nqp)
    qp3 = qproj.reshape(B, S, nqp)
    o_cmp, sel = nsa_cmp_branch(qp3, kc, vct)
    o_slc = nsa_slc_branch(qp3, sel, k_slc, vt_slc)
    o_win = nsa_win_branch(qp3, k_win, vt_win)
    return nsa_out(o_cmp.reshape(T, nq), o_slc.reshape(T, nq), o_win.reshape(T, nq), qproj, _cast(w_o), h)


def nsa_mixer(h, B, S, mix_gain, kv_gain, w_kv, cmp_pe_k, cmp_w1_k, cmp_w2_k, cmp_pe_v, cmp_w1_v, cmp_w2_v, w_q, w_o):
    shared = nsa_shared_kv(h, B, S, kv_gain, w_kv, cmp_pe_k, cmp_w1_k, cmp_w2_k, cmp_pe_v, cmp_w1_v, cmp_w2_v)
    return nsa_attention_block(h, B, S, mix_gain, w_q, w_o, shared)


def kernel(x, norm_mix_gain, norm_ffn_gain, ret_w_in, ret_gn_gain, ret_w_out, nsa_kv_norm_gain, nsa_w_kv, cmp_pe_k, cmp_w1_k, cmp_w2_k, cmp_pe_v, cmp_w1_v, cmp_w2_v, nsa_w_q, nsa_w_o, ffn_w_in, ffn_conv_w, ffn_conv_b, ffn_w_out, final_norm_gain):
    B, S, D = x.shape
    T = B * S
    depth = norm_mix_gain.shape[0]
    n_a = depth // 2
    h = x.reshape(T, D)
    shared = None
    for layer in range(depth):
        if layer < n_a:
            proj = norm_matmul(h, norm_mix_gain[layer], _cast(ret_w_in[layer]), tm=1024, tn=1024)
            y = retention_core(proj.reshape(B, S, -1), ret_gn_gain[layer])
            h = matmul_residual(y.reshape(T, -1), _cast(ret_w_out[layer]), h, tm=512, tn=1024)
        else:
            if layer == n_a:
                shared = nsa_shared_kv(h, B, S, nsa_kv_norm_gain, nsa_w_kv, cmp_pe_k, cmp_w1_k, cmp_w2_k,
                                       cmp_pe_v, cmp_w1_v, cmp_w2_v)
            b = layer - n_a
            h = nsa_attention_block(h, B, S, norm_mix_gain[layer], nsa_w_q[b], nsa_w_o[b], shared)
        h = conv_ffn_block(h, S, norm_ffn_gain[layer], _cast(ffn_w_in[layer]), ffn_conv_w[layer], ffn_conv_b[layer],
                           _cast(ffn_w_out[layer]), final_norm_gain, final_norm=(layer == depth - 1))
    return h.reshape(B, S, D)
```

```python
import functools

import jax
import jax.numpy as jnp
from jax import lax
from jax.experimental import pallas as pl
from jax.experimental.pallas import tpu as pltpu

F32 = jnp.float32
MXU_DTYPE = jnp.bfloat16

RMS_EPS = 1e-6
ROPE_BASE = 10000.0

RET_HEADS = 8
RET_CHUNK = 128

NSA_Q_HEADS = 16
NSA_KV_HEADS = 4
NSA_GROUP = NSA_Q_HEADS // NSA_KV_HEADS
NSA_HEAD_DIM = 128
N_BRANCH = 3
CMP_BLOCK = 32
CMP_STRIDE = 16
CMP_RATIO = CMP_BLOCK // CMP_STRIDE
SLC_BLOCK = 64
SLC_TOPK = 16
SLC_RATIO = SLC_BLOCK // CMP_STRIDE
WIN = 512
CONV_W = 3

NEG = -1e30
MASK_BIAS = -(2.0 ** 100)
PICKED = -3.0e38
SEL_FORCE = 1e30

LANE = 128
BF16_SUBLANES = 16
VMEM_LIMIT_BYTES = 56 * 1024 * 1024


def _cparams(*sem):
    return pltpu.CompilerParams(dimension_semantics=sem, vmem_limit_bytes=VMEM_LIMIT_BYTES)


def _sigmoid(x):
    return 1.0 / (1.0 + jnp.exp(-x))


def _rms_normalize(x, gain):
    ms = jnp.mean(x * x, axis=-1, keepdims=True)
    return x * lax.rsqrt(ms + RMS_EPS) * gain


def _norm_matmul_kernel(x_ref, g_ref, w_ref, o_ref, xn_ref):
    @pl.when(pl.program_id(1) == 0)
    def _():
        xn_ref[...] = _rms_normalize(x_ref[...], g_ref[...]).astype(xn_ref.dtype)

    o_ref[...] = jnp.dot(xn_ref[...], w_ref[...], preferred_element_type=F32).astype(o_ref.dtype)


def norm_matmul(h, gain, w, *, tm, tn, out_dtype=F32):
    T, D = h.shape
    N = w.shape[1]
    tm, tn = min(tm, T), min(tn, N)
    assert T % tm == 0 and N % tn == 0
    return pl.pallas_call(
        _norm_matmul_kernel,
        grid=(T // tm, N // tn),
        in_specs=[
            pl.BlockSpec((tm, D), lambda i, j: (i, 0)),
            pl.BlockSpec((1, D), lambda i, j: (0, 0)),
            pl.BlockSpec((D, tn), lambda i, j: (0, j)),
        ],
        out_specs=pl.BlockSpec((tm, tn), lambda i, j: (i, j)),
        out_shape=jax.ShapeDtypeStruct((T, N), out_dtype),
        scratch_shapes=[pltpu.VMEM((tm, D), w.dtype)],
        compiler_params=_cparams("arbitrary", "arbitrary"),
        name="norm_matmul",
    )(h, gain.reshape(1, D), w)


def _matmul_residual_kernel(y_ref, w_ref, h_ref, o_ref):
    o_ref[...] = h_ref[...] + jnp.dot(y_ref[...], w_ref[...], preferred_element_type=F32)


def matmul_residual(y, w, h, *, tm, tn):
    T, K = y.shape
    N = w.shape[1]
    tm, tn = min(tm, T), min(tn, N)
    assert T % tm == 0 and N % tn == 0
    return pl.pallas_call(
        _matmul_residual_kernel,
        grid=(T // tm, N // tn),
        in_specs=[
            pl.BlockSpec((tm, K), lambda i, j: (i, 0)),
            pl.BlockSpec((K, tn), lambda i, j: (0, j)),
            pl.BlockSpec((tm, tn), lambda i, j: (i, j)),
        ],
        out_specs=pl.BlockSpec((tm, tn), lambda i, j: (i, j)),
        out_shape=jax.ShapeDtypeStruct((T, N), F32),
        compiler_params=_cparams("arbitrary", "arbitrary"),
        name="matmul_residual",
    )(y, w, h)


def _retention_kernel(q_ref, k_ref, v_ref, g_ref, cos_ref, sin_ref, dmask_ref, qdec_ref, kdec_ref, gn_ref,
                      o_ref, state_ref, *, chunk, n_chunks, k_scale):
    @pl.when(pl.program_id(2) == 0)
    def _():
        state_ref[...] = jnp.zeros_like(state_ref)

    half = q_ref.shape[-1] // 2
    dmask = dmask_ref[0]
    qdec = qdec_ref[0]
    kdec = kdec_ref[0]
    chunk_decay = qdec[chunk - 1:chunk, :]
    gn = gn_ref[...]

    def rotate(x, cos, sin):
        x1, x2 = x[:, :half], x[:, half:]
        return jnp.concatenate([x1 * cos - x2 * sin, x1 * sin + x2 * cos], axis=-1)

    for ci in range(n_chunks):
        rows = pl.ds(ci * chunk, chunk)
        cos, sin = cos_ref[rows, :], sin_ref[rows, :]
        q = rotate(q_ref[0, rows, :], cos, sin)
        k = rotate(k_ref[0, rows, :], cos, sin) * k_scale
        v = v_ref[0, rows, :].astype(MXU_DTYPE)
        qm = q.astype(MXU_DTYPE)
        scores = lax.dot_general(qm, k.astype(MXU_DTYPE), (((1,), (1,)), ((), ())),
                                 preferred_element_type=F32) * dmask
        inner = jnp.dot(scores.astype(MXU_DTYPE), v, preferred_element_type=F32)
        state = state_ref[...]
        cross = jnp.dot(qm, state.astype(MXU_DTYPE), preferred_element_type=F32) * qdec
        k_dec_t = (k * kdec).T.astype(MXU_DTYPE)
        state_ref[...] = state * chunk_decay + jnp.dot(k_dec_t, v, preferred_element_type=F32)
        out = inner + cross
        mu = jnp.mean(out, axis=-1, keepdims=True)
        cen = out - mu
        var = jnp.mean(cen * cen, axis=-1, keepdims=True)
        normed = cen * lax.rsqrt(var + RMS_EPS) * gn
        g = g_ref[0, rows, :]
        o_ref[0, rows, :] = ((g * _sigmoid(g)) * normed).astype(o_ref.dtype)


def retention_core(proj, gn_gain, *, chunk=RET_CHUNK, tokens_per_step=512):
    B, S, P = proj.shape
    H = RET_HEADS
    dk = P // (6 * H)
    dv = 2 * dk
    tc = min(tokens_per_step, S)
    assert S % tc == 0 and tc % chunk == 0
    half = dk // 2
    pos = jnp.arange(S, dtype=F32)
    freqs = ROPE_BASE ** (-jnp.arange(half, dtype=F32) / half)
    ang = pos[:, None] * freqs[None, :]
    cos, sin = jnp.cos(ang), jnp.sin(ang)
    log_gamma = jnp.log(1.0 - 2.0 ** (-5.0 - jnp.arange(H, dtype=F32)))
    idx = jnp.arange(chunk, dtype=F32)
    diff = idx[:, None] - idx[None, :]
    dmask = jnp.where(diff >= 0, jnp.exp(jnp.maximum(diff, 0.0)[None] * log_gamma[:, None, None]), 0.0)
    qdec = jnp.exp((idx[None, :] + 1.0) * log_gamma[:, None])[:, :, None]
    kdec = jnp.exp((chunk - 1.0 - idx[None, :]) * log_gamma[:, None])[:, :, None]
    kern = functools.partial(_retention_kernel, chunk=chunk, n_chunks=tc // chunk, k_scale=dk ** -0.5)
    k_off, v_off, g_off = (H * dk) // dk, (2 * H * dk) // dv, (2 * H * dk + H * dv) // dv
    return pl.pallas_call(
        kern,
        grid=(B, H, S // tc),
        in_specs=[
            pl.BlockSpec((1, tc, dk), lambda b, h, c: (b, c, h)),
            pl.BlockSpec((1, tc, dk), lambda b, h, c: (b, c, k_off + h)),
            pl.BlockSpec((1, tc, dv), lambda b, h, c: (b, c, v_off + h)),
            pl.BlockSpec((1, tc, dv), lambda b, h, c: (b, c, g_off + h)),
            pl.BlockSpec((tc, half), lambda b, h, c: (c, 0)),
            pl.BlockSpec((tc, half), lambda b, h, c: (c, 0)),
            pl.BlockSpec((1, chunk, chunk), lambda b, h, c: (h, 0, 0)),
            pl.BlockSpec((1, chunk, 1), lambda b, h, c: (h, 0, 0)),
            pl.BlockSpec((1, chunk, 1), lambda b, h, c: (h, 0, 0)),
            pl.BlockSpec((1, dv), lambda b, h, c: (0, h)),
        ],
        out_specs=pl.BlockSpec((1, tc, dv), lambda b, h, c: (b, c, h)),
        out_shape=jax.ShapeDtypeStruct((B, S, H * dv), MXU_DTYPE),
        scratch_shapes=[pltpu.VMEM((dk, dv), F32)],
        compiler_params=_cparams("arbitrary", "arbitrary", "arbitrary"),
        name="retention_core",
    )(proj, proj, proj, proj, cos, sin, dmask, qdec, kdec, gn_gain.reshape(1, H * dv))


def _conv_ffn_kernel(x_ref, g_ref, wa_ref, wb_ref, cwa_ref, cwb_ref, cba_ref, cbb_ref, wo_ref, fg_ref,
                     o_ref, xn_ref, acc_ref, carry_a_ref, carry_b_ref, *, tiles_per_seq, final_norm):
    i, j = pl.program_id(0), pl.program_id(1)
    tm = x_ref.shape[0]

    @pl.when((i == 0) & (j == 0))
    def _():
        carry_a_ref[...] = jnp.zeros_like(carry_a_ref)
        carry_b_ref[...] = jnp.zeros_like(carry_b_ref)

    @pl.when(j == 0)
    def _():
        xn_ref[...] = _rms_normalize(x_ref[...], g_ref[...]).astype(xn_ref.dtype)
        acc_ref[...] = jnp.zeros_like(acc_ref)

    seq_start = (i % tiles_per_seq) == 0
    xn = xn_ref[...]

    def causal_conv(w_ref, cw_ref, cb_ref, carry_ref):
        u = jnp.dot(xn, w_ref[...], preferred_element_type=F32)
        prev = jnp.where(seq_start, 0.0, carry_ref[j])
        carry_ref[j] = u[tm - 8:, :]
        row = lax.broadcasted_iota(jnp.int32, u.shape, 0)
        u1 = jnp.where(row == 0, prev[7:8, :], pltpu.roll(u, 1, 0))
        u2 = jnp.where(row == 0, prev[6:7, :], jnp.where(row == 1, prev[7:8, :], pltpu.roll(u, 2, 0)))
        cw = cw_ref[...]
        return ((cb_ref[...] + u2 * cw[0:1, :]) + u1 * cw[1:2, :]) + u * cw[2:3, :]

    a = causal_conv(wa_ref, cwa_ref, cba_ref, carry_a_ref)
    b = causal_conv(wb_ref, cwb_ref, cbb_ref, carry_b_ref)
    act = ((a * _sigmoid(a)) * b).astype(wo_ref.dtype)
    acc_ref[...] += jnp.dot(act, wo_ref[...], preferred_element_type=F32)

    @pl.when(j == pl.num_programs(1) - 1)
    def _():
        out = x_ref[...] + acc_ref[...]
        if final_norm:
            out = _rms_normalize(out, fg_ref[...])
        o_ref[...] = out


def conv_ffn_block(h, seq_len, norm_gain, w_in, conv_w, conv_b, w_out, final_gain, *, final_norm, tm=512, tn=512):
    T, D = h.shape
    F = w_out.shape[0]
    tm, tn = min(tm, seq_len), min(tn, F)
    assert seq_len % tm == 0 and F % tn == 0 and tm >= 8
    nj = F // tn
    kern = functools.partial(_conv_ffn_kernel, tiles_per_seq=seq_len // tm, final_norm=final_norm)
    return pl.pallas_call(
        kern,
        grid=(T // tm, nj),
        in_specs=[
            pl.BlockSpec((tm, D), lambda i, j: (i, 0)),
            pl.BlockSpec((1, D), lambda i, j: (0, 0)),
            pl.BlockSpec((D, tn), lambda i, j: (0, j)),
            pl.BlockSpec((D, tn), lambda i, j: (0, nj + j)),
            pl.BlockSpec((CONV_W, tn), lambda i, j: (0, j)),
            pl.BlockSpec((CONV_W, tn), lambda i, j: (0, nj + j)),
            pl.BlockSpec((1, tn), lambda i, j: (0, j)),
            pl.BlockSpec((1, tn), lambda i, j: (0, nj + j)),
            pl.BlockSpec((tn, D), lambda i, j: (j, 0)),
            pl.BlockSpec((1, D), lambda i, j: (0, 0)),
        ],
        out_specs=pl.BlockSpec((tm, D), lambda i, j: (i, 0)),
        out_shape=jax.ShapeDtypeStruct((T, D), F32),
        scratch_shapes=[
            pltpu.VMEM((tm, D), w_in.dtype),
            pltpu.VMEM((tm, D), F32),
            pltpu.VMEM((nj, 8, tn), F32),
            pltpu.VMEM((nj, 8, tn), F32),
        ],
        compiler_params=_cparams("arbitrary", "arbitrary"),
        name="conv_ffn",
    )(h, norm_gain.reshape(1, D), w_in, w_in, conv_w, conv_w, conv_b.reshape(1, 2 * F), conv_b.reshape(1, 2 * F),
      w_out, final_gain.reshape(1, D))


def _kv_proj_kernel(x_ref, g_ref, w_ref, cmp_ref, kslc_ref, kwin_ref, vt_ref, *, seq_len):
    tm = x_ref.shape[0]
    d = NSA_HEAD_DIM
    Hkv = NSA_KV_HEADS
    xn = _rms_normalize(x_ref[...], g_ref[...]).astype(w_ref.dtype)
    res = jnp.dot(xn, w_ref[...], preferred_element_type=F32)
    col = lambda branch, hh: res[:, (branch * Hkv + hh) * d:(branch * Hkv + hh + 1) * d]
    pos = (pl.program_id(0) * tm) % seq_len + lax.broadcasted_iota(jnp.int32, (tm, LANE), 0)
    blk = (pos >> (SLC_BLOCK.bit_length() - 1)) & (LANE - 1)
    onehot = jnp.where(blk == lax.broadcasted_iota(jnp.int32, (tm, LANE), 1), 1.0, 0.0).astype(kslc_ref.dtype)
    ones = jnp.ones((BF16_SUBLANES, tm), vt_ref.dtype)
    for hh in range(Hkv):
        cmp_ref[hh] = col(0, hh)
        cmp_ref[Hkv + hh] = col(1, hh)
        kslc_ref[hh, :, :d] = col(2, hh).astype(kslc_ref.dtype)
        kslc_ref[hh, :, d:] = onehot
        kwin_ref[hh] = col(4, hh).astype(kwin_ref.dtype)
        vt_ref[hh, :d, :] = col(3, hh).T.astype(vt_ref.dtype)
        vt_ref[hh, d:, :] = ones
        vt_ref[Hkv + hh, :d, :] = col(5, hh).T.astype(vt_ref.dtype)
        vt_ref[Hkv + hh, d:, :] = ones


def kv_projection(h, seq_len, gain, w_kv, *, tm=256):
    T, D = h.shape
    Hkv, d = NSA_KV_HEADS, NSA_HEAD_DIM
    N = w_kv.shape[1]
    tm = min(tm, seq_len)
    assert seq_len % tm == 0 and N == 2 * N_BRANCH * Hkv * d and tm % LANE == 0
    row3 = lambda i: (0, i, 0)
    return pl.pallas_call(
        functools.partial(_kv_proj_kernel, seq_len=seq_len),
        grid=(T // tm,),
        in_specs=[
            pl.BlockSpec((tm, D), lambda i: (i, 0)),
            pl.BlockSpec((1, D), lambda i: (0, 0)),
            pl.BlockSpec((D, N), lambda i: (0, 0)),
        ],
        out_specs=[
            pl.BlockSpec((2 * Hkv, tm, d), row3),
            pl.BlockSpec((Hkv, tm, d + LANE), row3),
            pl.BlockSpec((Hkv, tm, d), row3),
            pl.BlockSpec((2 * Hkv, d + BF16_SUBLANES, tm), lambda i: (0, 0, i)),
        ],
        out_shape=[
            jax.ShapeDtypeStruct((2 * Hkv, T, d), F32),
            jax.ShapeDtypeStruct((Hkv, T, d + LANE), MXU_DTYPE),
            jax.ShapeDtypeStruct((Hkv, T, d), MXU_DTYPE),
            jax.ShapeDtypeStruct((2 * Hkv, d + BF16_SUBLANES, T), MXU_DTYPE),
        ],
        compiler_params=_cparams("arbitrary"),
        name="kv_projection",
    )(h, gain.reshape(1, D), w_kv)


def _compress_kernel(x_ref, pe_ref, w1_ref, w2_ref, o_ref):
    d = x_ref.shape[2]
    n = o_ref.shape[2]
    first = jnp.zeros((n, w1_ref.shape[2]), F32)
    second = jnp.zeros((n, w1_ref.shape[2]), F32)
    for p in range(CMP_STRIDE):
        xp = x_ref[0, pl.ds(p, n, stride=CMP_STRIDE), :]
        q = CMP_STRIDE + p
        first = first + jnp.dot((xp + pe_ref[0, p:p + 1, :]).astype(MXU_DTYPE), w1_ref[0, p * d:(p + 1) * d, :],
                                preferred_element_type=F32)
        second = second + jnp.dot((xp + pe_ref[0, q:q + 1, :]).astype(MXU_DTYPE), w1_ref[0, q * d:(q + 1) * d, :],
                                  preferred_element_type=F32)
    hid = first + pltpu.roll(second, n - 1, 0)
    gelu = 0.5 * hid * (1.0 + jnp.tanh(0.7978845608028654 * (hid + 0.044715 * (hid * hid * hid))))
    out = jnp.dot(gelu.astype(MXU_DTYPE), w2_ref[0], preferred_element_type=F32)
    row = lax.broadcasted_iota(jnp.int32, out.shape, 0)
    o_ref[0, 0] = jnp.where(row < n - 1, out, 0.0)


def compress_tokens(x, batch, pe, w1, w2):
    H2, T, d = x.shape
    Hkv = H2 // 2
    S = T // batch
    n = S // CMP_STRIDE
    hid = w2.shape[1]
    return pl.pallas_call(
        _compress_kernel,
        grid=(2, batch, Hkv),
        in_specs=[
            pl.BlockSpec((1, S, d), lambda s, b, hh: (s * Hkv + hh, b, 0)),
            pl.BlockSpec((1, CMP_BLOCK, d), lambda s, b, hh: (s, 0, 0)),
            pl.BlockSpec((1, CMP_BLOCK * d, hid), lambda s, b, hh: (s, 0, 0)),
            pl.BlockSpec((1, hid, d), lambda s, b, hh: (s, 0, 0)),
        ],
        out_specs=pl.BlockSpec((1, 1, n, d), lambda s, b, hh: (s, b * Hkv + hh, 0, 0)),
        out_shape=jax.ShapeDtypeStruct((2, batch * Hkv, n, d), F32),
        compiler_params=_cparams("arbitrary", "arbitrary", "arbitrary"),
        name="compress_tokens",
    )(x, pe, w1, w2)


def _nsa_cmp_kernel(q_ref, kc_ref, vct_ref, o_ref, sel_ref, qt_ref, *, q_scale, top_k):
    i = pl.program_id(2)
    tq = q_ref.shape[1]
    d = NSA_HEAD_DIM
    G = NSA_GROUP
    ncp = kc_ref.shape[2]
    nsel = ncp // SLC_RATIO
    t0 = i * tq
    for g in range(G):
        qt_ref[:, g * tq:(g + 1) * tq] = (q_ref[0, :, g * d:(g + 1) * d] * q_scale).T.astype(qt_ref.dtype)
    row = lax.broadcasted_iota(jnp.int32, (ncp, tq), 0)
    tok = t0 + lax.broadcasted_iota(jnp.int32, (ncp, tq), 1)
    cblk = SLC_RATIO * (row & (nsel - 1)) + (row >> (nsel.bit_length() - 1))
    bias = jnp.where(cblk * CMP_STRIDE + (CMP_BLOCK - 1) <= tok, 0.0, NEG)
    s = jnp.dot(kc_ref[0, 0], qt_ref[...], preferred_element_type=F32) + jnp.concatenate([bias] * G, axis=1)
    e = jnp.exp(s - jnp.max(s, axis=0, keepdims=True))
    denom = jnp.sum(e, axis=0, keepdims=True)
    any_valid = jnp.concatenate([tok[0:1, :] >= CMP_BLOCK - 1] * G, axis=1)
    p = e * jnp.where(any_valid, 1.0 / denom, 0.0)
    out_t = jnp.dot(vct_ref[0, 0], p.astype(MXU_DTYPE), preferred_element_type=F32)
    imp = None
    for g in range(G):
        o_ref[0, :, g * d:(g + 1) * d] = out_t[:, g * tq:(g + 1) * tq].T
        pg = p[:, g * tq:(g + 1) * tq]
        imp = pg if imp is None else imp + pg
    ph = [imp[r * nsel:(r + 1) * nsel, :] for r in range(SLC_RATIO)]
    j = lax.broadcasted_iota(jnp.int32, (nsel, tq), 0)
    p_slc = jnp.where(j == 0, 0.0, pltpu.roll(ph[SLC_RATIO - 1], 1, 0))
    for r in range(SLC_RATIO):
        p_slc = p_slc + ph[r]
    cur = (t0 + lax.broadcasted_iota(jnp.int32, (nsel, tq), 1)) >> (SLC_BLOCK.bit_length() - 1)
    causal = j <= cur

    @pl.when(t0 < top_k * SLC_BLOCK)
    def _():
        sel_ref[0, 0] = jnp.where(causal, 0.0, MASK_BIAS).astype(sel_ref.dtype)

    @pl.when(t0 >= top_k * SLC_BLOCK)
    def _():
        forced = (j == 0) | (j == cur) | (j == cur - 1)
        score = jnp.where(causal & jnp.logical_not(forced), p_slc, PICKED)
        jf = j.astype(F32)
        picked = jnp.where(forced, 1.0, 0.0)
        for _ in range(top_k - 3):
            m = jnp.max(score, axis=0, keepdims=True)
            first = jnp.min(jnp.where(score == m, jf, float(nsel)), axis=0, keepdims=True)
            hit = jf == first
            picked = jnp.where(hit, 1.0, picked)
            score = jnp.where(hit, PICKED, score)
        sel_ref[0, 0] = jnp.where((picked > 0.0) & causal, 0.0, MASK_BIAS).astype(sel_ref.dtype)


def nsa_cmp_branch(qproj, kc, vct, *, tq=128):
    B, S, _ = qproj.shape
    Hkv, G, d = NSA_KV_HEADS, NSA_GROUP, NSA_HEAD_DIM
    ncp = kc.shape[2]
    nsel = ncp // SLC_RATIO
    top_k = min(SLC_TOPK, nsel)
    tq = min(tq, S)
    assert S % tq == 0 and nsel & (nsel - 1) == 0 and tq & (tq - 1) == 0
    assert (top_k * SLC_BLOCK) % tq == 0 and top_k >= 3
    kern = functools.partial(_nsa_cmp_kernel, q_scale=d ** -0.5, top_k=top_k)
    return pl.pallas_call(
        kern,
        grid=(B, Hkv, S // tq),
        in_specs=[
            pl.BlockSpec((1, tq, G * d), lambda b, h, i: (b, i, h)),
            pl.BlockSpec((1, 1, ncp, d), lambda b, h, i: (b, h, 0, 0)),
            pl.BlockSpec((1, 1, d, ncp), lambda b, h, i: (b, h, 0, 0)),
        ],
        out_specs=[
            pl.BlockSpec((1, tq, G * d), lambda b, h, i: (b, i, h)),
            pl.BlockSpec((1, 1, nsel, tq), lambda b, h, i: (b, h, 0, i)),
        ],
        out_shape=[
            jax.ShapeDtypeStruct((B, S, Hkv * G * d), F32),
            jax.ShapeDtypeStruct((B, Hkv, nsel, S), MXU_DTYPE),
        ],
        scratch_shapes=[pltpu.VMEM((d, G * tq), MXU_DTYPE)],
        compiler_params=_cparams("arbitrary", "arbitrary", "arbitrary"),
        name="nsa_cmp",
    )(qproj, kc, vct)


def _nsa_slc_kernel(q_ref, sel_ref, k_ref, vt_ref, o_ref, qa_ref, m_ref, acc_ref, *, q_scale, tk):
    i = pl.program_id(2)
    tq = q_ref.shape[1]
    d = NSA_HEAD_DIM
    G = NSA_GROUP
    nblk = sel_ref.shape[2]
    span = LANE * SLC_BLOCK
    n_halves = pl.cdiv(nblk, LANE)

    m_ref[...] = jnp.full_like(m_ref, PICKED)
    acc_ref[...] = jnp.zeros_like(acc_ref)
    for g in range(G):
        qa_ref[:d, g * tq:(g + 1) * tq] = (q_ref[0, :, g * d:(g + 1) * d] * q_scale).T.astype(qa_ref.dtype)

    def attend(kt_idx, diagonal):
        k0 = pl.multiple_of(kt_idx * tk, tk)
        s = jnp.dot(k_ref[0, pl.ds(k0, tk), :], qa_ref[...], preferred_element_type=F32)
        if diagonal:
            col = lax.broadcasted_iota(jnp.int32, s.shape, 1)
            tok = i * tq + (col & (tq - 1))
            key = k0 + lax.broadcasted_iota(jnp.int32, s.shape, 0)
            s = jnp.where(key <= tok, s, NEG)
        m_old = m_ref[...]
        m_new = jnp.maximum(m_old, jnp.max(s, axis=0, keepdims=True))
        alpha = jnp.exp(m_old - m_new)
        p = jnp.exp(s - m_new)
        acc_ref[...] = alpha * acc_ref[...] + jnp.dot(vt_ref[0, :, pl.ds(k0, tk)], p.astype(MXU_DTYPE),
                                                      preferred_element_type=F32)
        m_ref[...] = m_new

    last = ((i + 1) * tq - 1) // tk
    tiles_per_half = span // tk
    for hf in range(n_halves):
        lo = hf * tiles_per_half
        hi = jnp.minimum(last, lo + tiles_per_half)

        @pl.when(lo <= last)
        def _():
            sel_half = sel_ref[0, 0, hf * LANE:(hf + 1) * LANE, :]
            for g in range(G):
                qa_ref[d:, g * tq:(g + 1) * tq] = sel_half

            def body(kt_idx, carry):
                attend(kt_idx, False)
                return carry

            lax.fori_loop(lo, hi, body, 0)

            @pl.when(last < lo + tiles_per_half)
            def _():
                attend(last, True)

    out = acc_ref[:d, :] * (1.0 / acc_ref[d:d + 1, :])
    for g in range(G):
        o_ref[0, :, g * d:(g + 1) * d] = out[:, g * tq:(g + 1) * tq].T


def nsa_slc_branch(qproj, sel, k_aug, vt, *, tq=256, tk=1024):
    B, S, _ = qproj.shape
    Hkv, G, d = NSA_KV_HEADS, NSA_GROUP, NSA_HEAD_DIM
    nblk = sel.shape[2]
    dv = vt.shape[1]
    tq, tk = min(tq, S), min(tk, S)
    assert S % tq == 0 and S % tk == 0 and tk % tq == 0 and tq & (tq - 1) == 0
    assert nblk % LANE == 0 and (LANE * SLC_BLOCK) % tk == 0
    kern = functools.partial(_nsa_slc_kernel, q_scale=d ** -0.5, tk=tk)
    return pl.pallas_call(
        kern,
        grid=(B, Hkv, S // tq),
        in_specs=[
            pl.BlockSpec((1, tq, G * d), lambda b, h, i: (b, i, h)),
            pl.BlockSpec((1, 1, nblk, tq), lambda b, h, i: (b, h, 0, i)),
            pl.BlockSpec((1, S, d + LANE), lambda b, h, i: (h, b, 0)),
            pl.BlockSpec((1, dv, S), lambda b, h, i: (h, 0, b)),
        ],
        out_specs=pl.BlockSpec((1, tq, G * d), lambda b, h, i: (b, i, h)),
        out_shape=jax.ShapeDtypeStruct((B, S, Hkv * G * d), F32),
        scratch_shapes=[
            pltpu.VMEM((d + LANE, G * tq), MXU_DTYPE),
            pltpu.VMEM((1, G * tq), F32),
            pltpu.VMEM((dv, G * tq), F32),
        ],
        compiler_params=_cparams("arbitrary", "arbitrary", "arbitrary"),
        name="nsa_slc",
    )(qproj, sel, k_aug, vt)


def _nsa_win_kernel(q_ref, k_ref, vt_ref, o_ref, qt_ref, *, q_scale):
    i = pl.program_id(2)
    tq = q_ref.shape[1]
    d = NSA_HEAD_DIM
    G = NSA_GROUP
    nk = WIN + tq
    k0 = pl.multiple_of(jnp.maximum(i * tq - WIN, 0), tq)
    for g in range(G):
        qt_ref[:, g * tq:(g + 1) * tq] = (q_ref[0, :, g * d:(g + 1) * d] * q_scale).T.astype(qt_ref.dtype)
    s = jnp.dot(k_ref[0, pl.ds(k0, nk), :], qt_ref[...], preferred_element_type=F32)
    pos = k0 + lax.broadcasted_iota(jnp.int32, s.shape, 0)
    tok = i * tq + (lax.broadcasted_iota(jnp.int32, s.shape, 1) & (tq - 1))
    s = jnp.where((pos <= tok) & (pos > tok - WIN), s, NEG)
    e = jnp.exp(s - jnp.max(s, axis=0, keepdims=True))
    p = e * (1.0 / jnp.sum(e, axis=0, keepdims=True))
    out_t = jnp.dot(vt_ref[0, :d, pl.ds(k0, nk)], p.astype(MXU_DTYPE), preferred_element_type=F32)
    for g in range(G):
        o_ref[0, :, g * d:(g + 1) * d] = out_t[:, g * tq:(g + 1) * tq].T


def nsa_win_branch(qproj, k_win, vt, *, tq=256):
    B, S, _ = qproj.shape
    Hkv, G, d = NSA_KV_HEADS, NSA_GROUP, NSA_HEAD_DIM
    dv = vt.shape[1]
    tq = min(tq, S)
    assert S % tq == 0 and tq % LANE == 0 and tq & (tq - 1) == 0 and WIN % tq == 0 and S >= WIN + tq
    kern = functools.partial(_nsa_win_kernel, q_scale=d ** -0.5)
    return pl.pallas_call(
        kern,
        grid=(B, Hkv, S // tq),
        in_specs=[
            pl.BlockSpec((1, tq, G * d), lambda b, h, i: (b, i, h)),
            pl.BlockSpec((1, S, d), lambda b, h, i: (h, b, 0)),
            pl.BlockSpec((1, dv, S), lambda b, h, i: (Hkv + h, 0, b)),
        ],
        out_specs=pl.BlockSpec((1, tq, G * d), lambda b, h, i: (b, i, h)),
        out_shape=jax.ShapeDtypeStruct((B, S, Hkv * G * d), F32),
        scratch_shapes=[pltpu.VMEM((d, G * tq), MXU_DTYPE)],
        compiler_params=_cparams("arbitrary", "arbitrary", "arbitrary"),
        name="nsa_win",
    )(qproj, k_win, vt)


def _nsa_out_kernel(oc_ref, os_ref, ow_ref, gate_ref, w_ref, h_ref, o_ref, merged_ref):
    d = NSA_HEAD_DIM
    gates = _sigmoid(gate_ref[...])
    for hq in range(NSA_Q_HEADS):
        cols = slice(hq * d, (hq + 1) * d)
        c = N_BRANCH * hq
        merged = (gates[:, c:c + 1] * oc_ref[:, cols] + gates[:, c + 1:c + 2] * os_ref[:, cols]
                  + gates[:, c + 2:c + 3] * ow_ref[:, cols])
        merged_ref[:, cols] = merged.astype(merged_ref.dtype)
    o_ref[...] = h_ref[...] + jnp.dot(merged_ref[...], w_ref[...], preferred_element_type=F32)


def nsa_out(o_cmp, o_slc, o_win, qproj, w_o, h, *, tm=256):
    T, HD = o_cmp.shape
    D = w_o.shape[1]
    tm = min(tm, T)
    assert T % tm == 0 and HD % LANE == 0
    gate_blk = HD // LANE
    row = lambda i: (i, 0)
    return pl.pallas_call(
        _nsa_out_kernel,
        grid=(T // tm,),
        in_specs=[
            pl.BlockSpec((tm, HD), row),
            pl.BlockSpec((tm, HD), row),
            pl.BlockSpec((tm, HD), row),
            pl.BlockSpec((tm, LANE), lambda i: (i, gate_blk)),
            pl.BlockSpec((HD, D), lambda i: (0, 0)),
            pl.BlockSpec((tm, D), row),
        ],
        out_specs=pl.BlockSpec((tm, D), row),
        out_shape=jax.ShapeDtypeStruct((T, D), F32),
        scratch_shapes=[pltpu.VMEM((tm, HD), w_o.dtype)],
        compiler_params=_cparams("arbitrary"),
        name="nsa_out",
    )(o_cmp, o_slc, o_win, qproj, w_o, h)


def _phase_major(x, nsel):
    B, n, H, d = x.shape
    return x.reshape(B, nsel, SLC_RATIO, H, d).transpose(0, 3, 2, 1, 4).reshape(B, H, n, d)


def _cast(w):
    return w.astype(MXU_DTYPE)


def nsa_shared_kv(h, B, S, kv_gain, w_kv, cmp_pe_k, cmp_w1_k, cmp_w2_k, cmp_pe_v, cmp_w1_v, cmp_w2_v):
    Hkv, d = NSA_KV_HEADS, NSA_HEAD_DIM
    cmp_in, k_slc, k_win, vt = kv_projection(h, S, kv_gain, _cast(w_kv))
    n = S // CMP_STRIDE
    nsel = S // SLC_BLOCK
    cmp = compress_tokens(cmp_in, B, jnp.stack([cmp_pe_k, cmp_pe_v]), _cast(jnp.stack([cmp_w1_k, cmp_w1_v])),
                          _cast(jnp.stack([cmp_w2_k, cmp_w2_v])))
    cmp = cmp.reshape(2, B, Hkv, n, d).transpose(0, 1, 3, 2, 4)
    kc = _cast(_phase_major(cmp[0], nsel))
    vct = _cast(_phase_major(cmp[1], nsel).transpose(0, 1, 3, 2))
    return kc, vct, k_slc, k_win, vt


def nsa_attention_block(h, B, S, mix_gain, w_q, w_o, shared):
    kc, vct, k_slc, k_win, vt = shared
    T = B * S
    nq = NSA_Q_HEADS * NSA_HEAD_DIM
    nqp = nq + LANE
    w_q = jnp.pad(w_q, ((0, 0), (0, nqp - w_q.shape[1])))
    qproj = norm_matmul(h, mix_gain, _cast(w_q), tm=512, tn=nqp)
    qp3 = qproj.reshape(B, S, nqp)
    o_cmp, sel = nsa_cmp_branch(qp3, kc, vct)
    o_slc = nsa_slc_branch(qp3, sel, k_slc, vt)
    o_win = nsa_win_branch(qp3, k_win, vt)
    return nsa_out(o_cmp.reshape(T, nq), o_slc.reshape(T, nq), o_win.reshape(T, nq), qproj, _cast(w_o), h)


def nsa_mixer(h, B, S, mix_gain, kv_gain, w_kv, cmp_pe_k, cmp_w1_k, cmp_w2_k, cmp_pe_v, cmp_w1_v, cmp_w2_v, w_q, w_o):
    shared = nsa_shared_kv(h, B, S, kv_gain, w_kv, cmp_pe_k, cmp_w1_k, cmp_w2_k, cmp_pe_v, cmp_w1_v, cmp_w2_v)
    return nsa_attention_block(h, B, S, mix_gain, w_q, w_o, shared)


def kernel(x, norm_mix_gain, norm_ffn_gain, ret_w_in, ret_gn_gain, ret_w_out, nsa_kv_norm_gain, nsa_w_kv, cmp_pe_k, cmp_w1_k, cmp_w2_k, cmp_pe_v, cmp_w1_v, cmp_w2_v, nsa_w_q, nsa_w_o, ffn_w_in, ffn_conv_w, ffn_conv_b, ffn_w_out, final_norm_gain):
    B, S, D = x.shape
    T = B * S
    depth = norm_mix_gain.shape[0]
    n_a = depth // 2
    h = x.reshape(T, D)
    shared = None
    for layer in range(depth):
        if layer < n_a:
            proj = norm_matmul(h, norm_mix_gain[layer], _cast(ret_w_in[layer]), tm=1024, tn=1024)
            y = retention_core(proj.reshape(B, S, -1), ret_gn_gain[layer])
            h = matmul_residual(y.reshape(T, -1), _cast(ret_w_out[layer]), h, tm=512, tn=1024)
        else:
            if layer == n_a:
                shared = nsa_shared_kv(h, B, S, nsa_kv_norm_gain, nsa_w_kv, cmp_pe_k, cmp_w1_k, cmp_w2_k,
                                       cmp_pe_v, cmp_w1_v, cmp_w2_v)
            b = layer - n_a
            h = nsa_attention_block(h, B, S, norm_mix_gain[layer], nsa_w_q[b], nsa_w_o[b], shared)
        h = conv_ffn_block(h, S, norm_ffn_gain[layer], _cast(ffn_w_in[layer]), ffn_conv_w[layer], ffn_conv_b[layer],
                           _cast(ffn_w_out[layer]), final_norm_gain, final_norm=(layer == depth - 1))
    return h.reshape(B, S, D)
```

```python
import functools

import jax
import jax.numpy as jnp
from jax import lax
from jax.experimental import pallas as pl
from jax.experimental.pallas import tpu as pltpu

F32 = jnp.float32
MXU_DTYPE = jnp.bfloat16

RMS_EPS = 1e-6
ROPE_BASE = 10000.0

RET_HEADS = 8
RET_CHUNK = 128

NSA_Q_HEADS = 16
NSA_KV_HEADS = 4
NSA_GROUP = NSA_Q_HEADS // NSA_KV_HEADS
NSA_HEAD_DIM = 128
N_BRANCH = 3
CMP_BLOCK = 32
CMP_STRIDE = 16
CMP_RATIO = CMP_BLOCK // CMP_STRIDE
SLC_BLOCK = 64
SLC_TOPK = 16
SLC_RATIO = SLC_BLOCK // CMP_STRIDE
WIN = 512
CONV_W = 3

NEG = -1e30
MASK_BIAS = -(2.0 ** 100)
PICKED = -3.0e38
SEL_FORCE = 1e30

LANE = 128
BF16_SUBLANES = 16
VMEM_LIMIT_BYTES = 56 * 1024 * 1024


def _cparams(*sem):
    return pltpu.CompilerParams(dimension_semantics=sem, vmem_limit_bytes=VMEM_LIMIT_BYTES)


def _sigmoid(x):
    return 1.0 / (1.0 + jnp.exp(-x))


def _rms_normalize(x, gain):
    ms = jnp.mean(x * x, axis=-1, keepdims=True)
    return x * lax.rsqrt(ms + RMS_EPS) * gain


def _norm_matmul_kernel(x_ref, g_ref, w_ref, o_ref, xn_ref):
    @pl.when(pl.program_id(1) == 0)
    def _():
        xn_ref[...] = _rms_normalize(x_ref[...], g_ref[...]).astype(xn_ref.dtype)

    o_ref[...] = jnp.dot(xn_ref[...], w_ref[...], preferred_element_type=F32).astype(o_ref.dtype)


def norm_matmul(h, gain, w, *, tm, tn, out_dtype=F32):
    T, D = h.shape
    N = w.shape[1]
    tm, tn = min(tm, T), min(tn, N)
    assert T % tm == 0 and N % tn == 0
    return pl.pallas_call(
        _norm_matmul_kernel,
        grid=(T // tm, N // tn),
        in_specs=[
            pl.BlockSpec((tm, D), lambda i, j: (i, 0)),
            pl.BlockSpec((1, D), lambda i, j: (0, 0)),
            pl.BlockSpec((D, tn), lambda i, j: (0, j)),
        ],
        out_specs=pl.BlockSpec((tm, tn), lambda i, j: (i, j)),
        out_shape=jax.ShapeDtypeStruct((T, N), out_dtype),
        scratch_shapes=[pltpu.VMEM((tm, D), w.dtype)],
        compiler_params=_cparams("arbitrary", "arbitrary"),
        name="norm_matmul",
    )(h, gain.reshape(1, D), w)


def _matmul_residual_kernel(y_ref, w_ref, h_ref, o_ref):
    o_ref[...] = h_ref[...] + jnp.dot(y_ref[...], w_ref[...], preferred_element_type=F32)


def matmul_residual(y, w, h, *, tm, tn):
    T, K = y.shape
    N = w.shape[1]
    tm, tn = min(tm, T), min(tn, N)
    assert T % tm == 0 and N % tn == 0
    return pl.pallas_call(
        _matmul_residual_kernel,
        grid=(T // tm, N // tn),
        in_specs=[
            pl.BlockSpec((tm, K), lambda i, j: (i, 0)),
            pl.BlockSpec((K, tn), lambda i, j: (0, j)),
            pl.BlockSpec((tm, tn), lambda i, j: (i, j)),
        ],
        out_specs=pl.BlockSpec((tm, tn), lambda i, j: (i, j)),
        out_shape=jax.ShapeDtypeStruct((T, N), F32),
        compiler_params=_cparams("arbitrary", "arbitrary"),
        name="matmul_residual",
    )(y, w, h)


def _retention_kernel(q_ref, k_ref, v_ref, g_ref, cos_ref, sin_ref, dmask_ref, qdec_ref, kdec_ref, gn_ref,
                      o_ref, state_ref, *, chunk, n_chunks, k_scale):
    @pl.when(pl.program_id(2) == 0)
    def _():
        state_ref[...] = jnp.zeros_like(state_ref)

    half = q_ref.shape[-1] // 2
    dmask = dmask_ref[0]
    qdec = qdec_ref[0]
    kdec = kdec_ref[0]
    chunk_decay = qdec[chunk - 1:chunk, :]
    gn = gn_ref[...]

    def rotate(x, cos, sin):
        x1, x2 = x[:, :half], x[:, half:]
        return jnp.concatenate([x1 * cos - x2 * sin, x1 * sin + x2 * cos], axis=-1)

    for ci in range(n_chunks):
        rows = pl.ds(ci * chunk, chunk)
        cos, sin = cos_ref[rows, :], sin_ref[rows, :]
        q = rotate(q_ref[0, rows, :], cos, sin)
        k = rotate(k_ref[0, rows, :], cos, sin) * k_scale
        v = v_ref[0, rows, :].astype(MXU_DTYPE)
        qm = q.astype(MXU_DTYPE)
        scores = lax.dot_general(qm, k.astype(MXU_DTYPE), (((1,), (1,)), ((), ())),
                                 preferred_element_type=F32) * dmask
        inner = jnp.dot(scores.astype(MXU_DTYPE), v, preferred_element_type=F32)
        state = state_ref[...]
        cross = jnp.dot(qm, state.astype(MXU_DTYPE), preferred_element_type=F32) * qdec
        k_dec_t = (k * kdec).T.astype(MXU_DTYPE)
        state_ref[...] = state * chunk_decay + jnp.dot(k_dec_t, v, preferred_element_type=F32)
        out = inner + cross
        mu = jnp.mean(out, axis=-1, keepdims=True)
        cen = out - mu
        var = jnp.mean(cen * cen, axis=-1, keepdims=True)
        normed = cen * lax.rsqrt(var + RMS_EPS) * gn
        g = g_ref[0, rows, :]
        o_ref[0, rows, :] = ((g * _sigmoid(g)) * normed).astype(o_ref.dtype)


def retention_core(proj, gn_gain, *, chunk=RET_CHUNK, tokens_per_step=512):
    B, S, P = proj.shape
    H = RET_HEADS
    dk = P // (6 * H)
    dv = 2 * dk
    tc = min(tokens_per_step, S)
    assert S % tc == 0 and tc % chunk == 0
    half = dk // 2
    pos = jnp.arange(S, dtype=F32)
    freqs = ROPE_BASE ** (-jnp.arange(half, dtype=F32) / half)
    ang = pos[:, None] * freqs[None, :]
    cos, sin = jnp.cos(ang), jnp.sin(ang)
    log_gamma = jnp.log(1.0 - 2.0 ** (-5.0 - jnp.arange(H, dtype=F32)))
    idx = jnp.arange(chunk, dtype=F32)
    diff = idx[:, None] - idx[None, :]
    dmask = jnp.where(diff >= 0, jnp.exp(jnp.maximum(diff, 0.0)[None] * log_gamma[:, None, None]), 0.0)
    qdec = jnp.exp((idx[None, :] + 1.0) * log_gamma[:, None])[:, :, None]
    kdec = jnp.exp((chunk - 1.0 - idx[None, :]) * log_gamma[:, None])[:, :, None]
    kern = functools.partial(_retention_kernel, chunk=chunk, n_chunks=tc // chunk, k_scale=dk ** -0.5)
    k_off, v_off, g_off = (H * dk) // dk, (2 * H * dk) // dv, (2 * H * dk + H * dv) // dv
    return pl.pallas_call(
        kern,
        grid=(B, H, S // tc),
        in_specs=[
            pl.BlockSpec((1, tc, dk), lambda b, h, c: (b, c, h)),
            pl.BlockSpec((1, tc, dk), lambda b, h, c: (b, c, k_off + h)),
            pl.BlockSpec((1, tc, dv), lambda b, h, c: (b, c, v_off + h)),
            pl.BlockSpec((1, tc, dv), lambda b, h, c: (b, c, g_off + h)),
            pl.BlockSpec((tc, half), lambda b, h, c: (c, 0)),
            pl.BlockSpec((tc, half), lambda b, h, c: (c, 0)),
            pl.BlockSpec((1, chunk, chunk), lambda b, h, c: (h, 0, 0)),
            pl.BlockSpec((1, chunk, 1), lambda b, h, c: (h, 0, 0)),
            pl.BlockSpec((1, chunk, 1), lambda b, h, c: (h, 0, 0)),
            pl.BlockSpec((1, dv), lambda b, h, c: (0, h)),
        ],
        out_specs=pl.BlockSpec((1, tc, dv), lambda b, h, c: (b, c, h)),
        out_shape=jax.ShapeDtypeStruct((B, S, H * dv), MXU_DTYPE),
        scratch_shapes=[pltpu.VMEM((dk, dv), F32)],
        compiler_params=_cparams("arbitrary", "arbitrary", "arbitrary"),
        name="retention_core",
    )(proj, proj, proj, proj, cos, sin, dmask, qdec, kdec, gn_gain.reshape(1, H * dv))


def _conv_ffn_kernel(x_ref, g_ref, wa_ref, wb_ref, cwa_ref, cwb_ref, cba_ref, cbb_ref, wo_ref, fg_ref,
                     o_ref, xn_ref, acc_ref, carry_a_ref, carry_b_ref, *, tiles_per_seq, final_norm):
    i, j = pl.program_id(0), pl.program_id(1)
    tm = x_ref.shape[0]

    @pl.when((i == 0) & (j == 0))
    def _():
        carry_a_ref[...] = jnp.zeros_like(carry_a_ref)
        carry_b_ref[...] = jnp.zeros_like(carry_b_ref)

    @pl.when(j == 0)
    def _():
        xn_ref[...] = _rms_normalize(x_ref[...], g_ref[...]).astype(xn_ref.dtype)
        acc_ref[...] = jnp.zeros_like(acc_ref)

    seq_start = (i % tiles_per_seq) == 0
    xn = xn_ref[...]

    def causal_conv(w_ref, cw_ref, cb_ref, carry_ref):
        u = jnp.dot(xn, w_ref[...], preferred_element_type=F32)
        prev = jnp.where(seq_start, 0.0, carry_ref[j])
        carry_ref[j] = u[tm - 8:, :]
        row = lax.broadcasted_iota(jnp.int32, u.shape, 0)
        u1 = jnp.where(row == 0, prev[7:8, :], pltpu.roll(u, 1, 0))
        u2 = jnp.where(row == 0, prev[6:7, :], jnp.where(row == 1, prev[7:8, :], pltpu.roll(u, 2, 0)))
        cw = cw_ref[...]
        return ((cb_ref[...] + u2 * cw[0:1, :]) + u1 * cw[1:2, :]) + u * cw[2:3, :]

    a = causal_conv(wa_ref, cwa_ref, cba_ref, carry_a_ref)
    b = causal_conv(wb_ref, cwb_ref, cbb_ref, carry_b_ref)
    act = ((a * _sigmoid(a)) * b).astype(wo_ref.dtype)
    acc_ref[...] += jnp.dot(act, wo_ref[...], preferred_element_type=F32)

    @pl.when(j == pl.num_programs(1) - 1)
    def _():
        out = x_ref[...] + acc_ref[...]
        if final_norm:
            out = _rms_normalize(out, fg_ref[...])
        o_ref[...] = out


def conv_ffn_block(h, seq_len, norm_gain, w_in, conv_w, conv_b, w_out, final_gain, *, final_norm, tm=512, tn=512):
    T, D = h.shape
    F = w_out.shape[0]
    tm, tn = min(tm, seq_len), min(tn, F)
    assert seq_len % tm == 0 and F % tn == 0 and tm >= 8
    nj = F // tn
    kern = functools.partial(_conv_ffn_kernel, tiles_per_seq=seq_len // tm, final_norm=final_norm)
    return pl.pallas_call(
        kern,
        grid=(T // tm, nj),
        in_specs=[
            pl.BlockSpec((tm, D), lambda i, j: (i, 0)),
            pl.BlockSpec((1, D), lambda i, j: (0, 0)),
            pl.BlockSpec((D, tn), lambda i, j: (0, j)),
            pl.BlockSpec((D, tn), lambda i, j: (0, nj + j)),
            pl.BlockSpec((CONV_W, tn), lambda i, j: (0, j)),
            pl.BlockSpec((CONV_W, tn), lambda i, j: (0, nj + j)),
            pl.BlockSpec((1, tn), lambda i, j: (0, j)),
            pl.BlockSpec((1, tn), lambda i, j: (0, nj + j)),
            pl.BlockSpec((tn, D), lambda i, j: (j, 0)),
            pl.BlockSpec((1, D), lambda i, j: (0, 0)),
        ],
        out_specs=pl.BlockSpec((tm, D), lambda i, j: (i, 0)),
        out_shape=jax.ShapeDtypeStruct((T, D), F32),
        scratch_shapes=[
            pltpu.VMEM((tm, D), w_in.dtype),
            pltpu.VMEM((tm, D), F32),
            pltpu.VMEM((nj, 8, tn), F32),
            pltpu.VMEM((nj, 8, tn), F32),
        ],
        compiler_params=_cparams("arbitrary", "arbitrary"),
        name="conv_ffn",
    )(h, norm_gain.reshape(1, D), w_in, w_in, conv_w, conv_w, conv_b.reshape(1, 2 * F), conv_b.reshape(1, 2 * F),
      w_out, final_gain.reshape(1, D))


def _kv_proj_kernel(x_ref, g_ref, w_ref, cmp_ref, kslc_ref, kwin_ref, vt_ref, *, seq_len):
    tm = x_ref.shape[0]
    d = NSA_HEAD_DIM
    Hkv = NSA_KV_HEADS
    xn = _rms_normalize(x_ref[...], g_ref[...]).astype(w_ref.dtype)
    res = jnp.dot(xn, w_ref[...], preferred_element_type=F32)
    col = lambda branch, hh: res[:, (branch * Hkv + hh) * d:(branch * Hkv + hh + 1) * d]
    pos = (pl.program_id(0) * tm) % seq_len + lax.broadcasted_iota(jnp.int32, (tm, LANE), 0)
    blk = (pos >> (SLC_BLOCK.bit_length() - 1)) & (LANE - 1)
    onehot = jnp.where(blk == lax.broadcasted_iota(jnp.int32, (tm, LANE), 1), 1.0, 0.0).astype(kslc_ref.dtype)
    ones = jnp.ones((BF16_SUBLANES, tm), vt_ref.dtype)
    for hh in range(Hkv):
        cmp_ref[hh] = col(0, hh)
        cmp_ref[Hkv + hh] = col(1, hh)
        kslc_ref[hh, :, :d] = col(2, hh).astype(kslc_ref.dtype)
        kslc_ref[hh, :, d:] = onehot
        kwin_ref[hh] = col(4, hh).astype(kwin_ref.dtype)
        vt_ref[hh, :d, :] = col(3, hh).T.astype(vt_ref.dtype)
        vt_ref[hh, d:, :] = ones
        vt_ref[Hkv + hh, :d, :] = col(5, hh).T.astype(vt_ref.dtype)
        vt_ref[Hkv + hh, d:, :] = ones


def kv_projection(h, seq_len, gain, w_kv, *, tm=256):
    T, D = h.shape
    Hkv, d = NSA_KV_HEADS, NSA_HEAD_DIM
    N = w_kv.shape[1]
    tm = min(tm, seq_len)
    assert seq_len % tm == 0 and N == 2 * N_BRANCH * Hkv * d and tm % LANE == 0
    row3 = lambda i: (0, i, 0)
    return pl.pallas_call(
        functools.partial(_kv_proj_kernel, seq_len=seq_len),
        grid=(T // tm,),
        in_specs=[
            pl.BlockSpec((tm, D), lambda i: (i, 0)),
            pl.BlockSpec((1, D), lambda i: (0, 0)),
            pl.BlockSpec((D, N), lambda i: (0, 0)),
        ],
        out_specs=[
            pl.BlockSpec((2 * Hkv, tm, d), row3),
            pl.BlockSpec((Hkv, tm, d + LANE), row3),
            pl.BlockSpec((Hkv, tm, d), row3),
            pl.BlockSpec((2 * Hkv, d + BF16_SUBLANES, tm), lambda i: (0, 0, i)),
        ],
        out_shape=[
            jax.ShapeDtypeStruct((2 * Hkv, T, d), F32),
            jax.ShapeDtypeStruct((Hkv, T, d + LANE), MXU_DTYPE),
            jax.ShapeDtypeStruct((Hkv, T, d), MXU_DTYPE),
            jax.ShapeDtypeStruct((2 * Hkv, d + BF16_SUBLANES, T), MXU_DTYPE),
        ],
        compiler_params=_cparams("arbitrary"),
        name="kv_projection",
    )(h, gain.reshape(1, D), w_kv)


def _compress_kernel(x_ref, pe_ref, w1_ref, w2_ref, o_ref):
    d = x_ref.shape[2]
    n = o_ref.shape[2]
    first = jnp.zeros((n, w1_ref.shape[2]), F32)
    second = jnp.zeros((n, w1_ref.shape[2]), F32)
    for p in range(CMP_STRIDE):
        xp = x_ref[0, pl.ds(p, n, stride=CMP_STRIDE), :]
        q = CMP_STRIDE + p
        first = first + jnp.dot((xp + pe_ref[0, p:p + 1, :]).astype(MXU_DTYPE), w1_ref[0, p * d:(p + 1) * d, :],
                                preferred_element_type=F32)
        second = second + jnp.dot((xp + pe_ref[0, q:q + 1, :]).astype(MXU_DTYPE), w1_ref[0, q * d:(q + 1) * d, :],
                                  preferred_element_type=F32)
    hid = first + pltpu.roll(second, n - 1, 0)
    gelu = 0.5 * hid * (1.0 + jnp.tanh(0.7978845608028654 * (hid + 0.044715 * (hid * hid * hid))))
    out = jnp.dot(gelu.astype(MXU_DTYPE), w2_ref[0], preferred_element_type=F32)
    row = lax.broadcasted_iota(jnp.int32, out.shape, 0)
    o_ref[0, 0] = jnp.where(row < n - 1, out, 0.0)


def compress_tokens(x, batch, pe, w1, w2):
    H2, T, d = x.shape
    Hkv = H2 // 2
    S = T // batch
    n = S // CMP_STRIDE
    hid = w2.shape[1]
    return pl.pallas_call(
        _compress_kernel,
        grid=(2, batch, Hkv),
        in_specs=[
            pl.BlockSpec((1, S, d), lambda s, b, hh: (s * Hkv + hh, b, 0)),
            pl.BlockSpec((1, CMP_BLOCK, d), lambda s, b, hh: (s, 0, 0)),
            pl.BlockSpec((1, CMP_BLOCK * d, hid), lambda s, b, hh: (s, 0, 0)),
            pl.BlockSpec((1, hid, d), lambda s, b, hh: (s, 0, 0)),
        ],
        out_specs=pl.BlockSpec((1, 1, n, d), lambda s, b, hh: (s, b * Hkv + hh, 0, 0)),
        out_shape=jax.ShapeDtypeStruct((2, batch * Hkv, n, d), F32),
        compiler_params=_cparams("arbitrary", "arbitrary", "arbitrary"),
        name="compress_tokens",
    )(x, pe, w1, w2)


def _nsa_cmp_kernel(q_ref, kc_ref, vct_ref, o_ref, sel_ref, qt_ref, *, q_scale, top_k):
    i = pl.program_id(2)
    tq = q_ref.shape[1]
    d = NSA_HEAD_DIM
    G = NSA_GROUP
    ncp = kc_ref.shape[2]
    nsel = ncp // SLC_RATIO
    t0 = i * tq
    for g in range(G):
        qt_ref[:, g * tq:(g + 1) * tq] = (q_ref[0, :, g * d:(g + 1) * d] * q_scale).T.astype(qt_ref.dtype)
    row = lax.broadcasted_iota(jnp.int32, (ncp, tq), 0)
    tok = t0 + lax.broadcasted_iota(jnp.int32, (ncp, tq), 1)
    cblk = SLC_RATIO * (row & (nsel - 1)) + (row >> (nsel.bit_length() - 1))
    bias = jnp.where(cblk * CMP_STRIDE + (CMP_BLOCK - 1) <= tok, 0.0, NEG)
    s = jnp.dot(kc_ref[0, 0], qt_ref[...], preferred_element_type=F32) + jnp.concatenate([bias] * G, axis=1)
    e = jnp.exp(s - jnp.max(s, axis=0, keepdims=True))
    denom = jnp.sum(e, axis=0, keepdims=True)
    any_valid = jnp.concatenate([tok[0:1, :] >= CMP_BLOCK - 1] * G, axis=1)
    p = e * jnp.where(any_valid, 1.0 / denom, 0.0)
    out_t = jnp.dot(vct_ref[0, 0], p.astype(MXU_DTYPE), preferred_element_type=F32)
    imp = None
    for g in range(G):
        o_ref[0, :, g * d:(g + 1) * d] = out_t[:, g * tq:(g + 1) * tq].T
        pg = p[:, g * tq:(g + 1) * tq]
        imp = pg if imp is None else imp + pg
    ph = [imp[r * nsel:(r + 1) * nsel, :] for r in range(SLC_RATIO)]
    j = lax.broadcasted_iota(jnp.int32, (nsel, tq), 0)
    p_slc = jnp.where(j == 0, 0.0, pltpu.roll(ph[SLC_RATIO - 1], 1, 0))
    for r in range(SLC_RATIO):
        p_slc = p_slc + ph[r]
    cur = (t0 + lax.broadcasted_iota(jnp.int32, (nsel, tq), 1)) >> (SLC_BLOCK.bit_length() - 1)
    causal = j <= cur

    @pl.when(t0 < top_k * SLC_BLOCK)
    def _():
        sel_ref[0, 0] = jnp.where(causal, 0.0, MASK_BIAS).astype(sel_ref.dtype)

    @pl.when(t0 >= top_k * SLC_BLOCK)
    def _():
        forced = (j == 0) | (j == cur) | (j == cur - 1)
        score = jnp.where(causal & jnp.logical_not(forced), p_slc, PICKED)
        jf = j.astype(F32)
        picked = jnp.where(forced, 1.0, 0.0)
        for _ in range(top_k - 3):
            m = jnp.max(score, axis=0, keepdims=True)
            first = jnp.min(jnp.where(score == m, jf, float(nsel)), axis=0, keepdims=True)
            hit = jf == first
            picked = jnp.where(hit, 1.0, picked)
            score = jnp.where(hit, PICKED, score)
        sel_ref[0, 0] = jnp.where((picked > 0.0) & causal, 0.0, MASK_BIAS).astype(sel_ref.dtype)


def nsa_cmp_branch(qproj, kc, vct, *, tq=128):
    B, S, _ = qproj.shape
    Hkv, G, d = NSA_KV_HEADS, NSA_GROUP, NSA_HEAD_DIM
    ncp = kc.shape[2]
    nsel = ncp // SLC_RATIO
    top_k = min(SLC_TOPK, nsel)
    tq = min(tq, S)
    assert S % tq == 0 and nsel & (nsel - 1) == 0 and tq & (tq - 1) == 0
    assert (top_k * SLC_BLOCK) % tq == 0 and top_k >= 3
    kern = functools.partial(_nsa_cmp_kernel, q_scale=d ** -0.5, top_k=top_k)
    return pl.pallas_call(
        kern,
        grid=(B, Hkv, S // tq),
        in_specs=[
            pl.BlockSpec((1, tq, G * d), lambda b, h, i: (b, i, h)),
            pl.BlockSpec((1, 1, ncp, d), lambda b, h, i: (b, h, 0, 0)),
            pl.BlockSpec((1, 1, d, ncp), lambda b, h, i: (b, h, 0, 0)),
        ],
        out_specs=[
            pl.BlockSpec((1, tq, G * d), lambda b, h, i: (b, i, h)),
            pl.BlockSpec((1, 1, nsel, tq), lambda b, h, i: (b, h, 0, i)),
        ],
        out_shape=[
            jax.ShapeDtypeStruct((B, S, Hkv * G * d), F32),
            jax.ShapeDtypeStruct((B, Hkv, nsel, S), MXU_DTYPE),
        ],
        scratch_shapes=[pltpu.VMEM((d, G * tq), MXU_DTYPE)],
        compiler_params=_cparams("arbitrary", "arbitrary", "arbitrary"),
        name="nsa_cmp",
    )(qproj, kc, vct)


def _nsa_slc_kernel(q_ref, sel_ref, k_ref, vt_ref, o_ref, qa_ref, m_ref, acc_ref, s_ref, *, q_scale, tk):
    i = pl.program_id(2)
    tq = q_ref.shape[1]
    d = NSA_HEAD_DIM
    G = NSA_GROUP
    nblk = sel_ref.shape[2]
    span = LANE * SLC_BLOCK
    n_halves = pl.cdiv(nblk, LANE)

    m_ref[...] = jnp.full_like(m_ref, PICKED)
    acc_ref[...] = jnp.zeros_like(acc_ref)
    for g in range(G):
        qa_ref[:d, g * tq:(g + 1) * tq] = (q_ref[0, :, g * d:(g + 1) * d] * q_scale).T.astype(qa_ref.dtype)

    def scores(kt_idx):
        k0 = pl.multiple_of(kt_idx * tk, tk)
        return jnp.dot(k_ref[0, pl.ds(k0, tk), :], qa_ref[...], preferred_element_type=F32)

    def accumulate(kt_idx, s):
        k0 = pl.multiple_of(kt_idx * tk, tk)
        m_old = m_ref[...]
        m_new = jnp.maximum(m_old, jnp.max(s, axis=0, keepdims=True))
        alpha = jnp.exp(m_old - m_new)
        p = jnp.exp(s - m_new)
        acc_ref[...] = alpha * acc_ref[...] + jnp.dot(vt_ref[0, :, pl.ds(k0, tk)], p.astype(MXU_DTYPE),
                                                      preferred_element_type=F32)
        m_ref[...] = m_new

    last = ((i + 1) * tq - 1) // tk
    tiles_per_half = span // tk
    for hf in range(n_halves):
        lo = hf * tiles_per_half
        hi = jnp.minimum(last, lo + tiles_per_half)

        @pl.when(lo <= last)
        def _():
            sel_half = sel_ref[0, 0, hf * LANE:(hf + 1) * LANE, :]
            for g in range(G):
                qa_ref[d:, g * tq:(g + 1) * tq] = sel_half
            n = hi - lo

            @pl.when(n > 0)
            def _():
                s_ref[0] = scores(lo)

            def pair(pi, carry):
                t = lo + 2 * pi
                s_ref[1] = scores(t + 1)
                accumulate(t, s_ref[0])
                s_ref[0] = scores(t + 2)
                accumulate(t + 1, s_ref[1])
                return carry

            lax.fori_loop(0, n >> 1, pair, 0)

            @pl.when((n & 1) == 1)
            def _():
                accumulate(hi - 1, s_ref[0])

            @pl.when(last < lo + tiles_per_half)
            def _():
                s = scores(last)
                col = lax.broadcasted_iota(jnp.int32, s.shape, 1)
                tok = i * tq + (col & (tq - 1))
                key = last * tk + lax.broadcasted_iota(jnp.int32, s.shape, 0)
                accumulate(last, jnp.where(key <= tok, s, NEG))

    out = acc_ref[:d, :] * (1.0 / acc_ref[d:d + 1, :])
    for g in range(G):
        o_ref[0, :, g * d:(g + 1) * d] = out[:, g * tq:(g + 1) * tq].T


def nsa_slc_branch(qproj, sel, k_aug, vt, *, tq=256, tk=1024):
    B, S, _ = qproj.shape
    Hkv, G, d = NSA_KV_HEADS, NSA_GROUP, NSA_HEAD_DIM
    nblk = sel.shape[2]
    dv = vt.shape[1]
    tq, tk = min(tq, S), min(tk, S)
    assert S % tq == 0 and S % tk == 0 and tk % tq == 0 and tq & (tq - 1) == 0
    assert nblk % LANE == 0 and (LANE * SLC_BLOCK) % tk == 0
    kern = functools.partial(_nsa_slc_kernel, q_scale=d ** -0.5, tk=tk)
    return pl.pallas_call(
        kern,
        grid=(B, Hkv, S // tq),
        in_specs=[
            pl.BlockSpec((1, tq, G * d), lambda b, h, i: (b, i, h)),
            pl.BlockSpec((1, 1, nblk, tq), lambda b, h, i: (b, h, 0, i)),
            pl.BlockSpec((1, S, d + LANE), lambda b, h, i: (h, b, 0)),
            pl.BlockSpec((1, dv, S), lambda b, h, i: (h, 0, b)),
        ],
        out_specs=pl.BlockSpec((1, tq, G * d), lambda b, h, i: (b, i, h)),
        out_shape=jax.ShapeDtypeStruct((B, S, Hkv * G * d), F32),
        scratch_shapes=[
            pltpu.VMEM((d + LANE, G * tq), MXU_DTYPE),
            pltpu.VMEM((1, G * tq), F32),
            pltpu.VMEM((dv, G * tq), F32),
            pltpu.VMEM((2, tk, G * tq), F32),
        ],
        compiler_params=_cparams("arbitrary", "arbitrary", "arbitrary"),
        name="nsa_slc",
    )(qproj, sel, k_aug, vt)


def _nsa_win_kernel(q_ref, k_ref, vt_ref, o_ref, qt_ref, *, q_scale):
    i = pl.program_id(2)
    tq = q_ref.shape[1]
    d = NSA_HEAD_DIM
    G = NSA_GROUP
    nk = WIN + tq
    k0 = pl.multiple_of(jnp.maximum(i * tq - WIN, 0), tq)
    for g in range(G):
        qt_ref[:, g * tq:(g + 1) * tq] = (q_ref[0, :, g * d:(g + 1) * d] * q_scale).T.astype(qt_ref.dtype)
    s = jnp.dot(k_ref[0, pl.ds(k0, nk), :], qt_ref[...], preferred_element_type=F32)
    pos = k0 + lax.broadcasted_iota(jnp.int32, s.shape, 0)
    tok = i * tq + (lax.broadcasted_iota(jnp.int32, s.shape, 1) & (tq - 1))
    s = jnp.where((pos <= tok) & (pos > tok - WIN), s, NEG)
    e = jnp.exp(s - jnp.max(s, axis=0, keepdims=True))
    p = e * (1.0 / jnp.sum(e, axis=0, keepdims=True))
    out_t = jnp.dot(vt_ref[0, :d, pl.ds(k0, nk)], p.astype(MXU_DTYPE), preferred_element_type=F32)
    for g in range(G):
        o_ref[0, :, g * d:(g + 1) * d] = out_t[:, g * tq:(g + 1) * tq].T


def nsa_win_branch(qproj, k_win, vt, *, tq=256):
    B, S, _ = qproj.shape
    Hkv, G, d = NSA_KV_HEADS, NSA_GROUP, NSA_HEAD_DIM
    dv = vt.shape[1]
    tq = min(tq, S)
    assert S % tq == 0 and tq % LANE == 0 and tq & (tq - 1) == 0 and WIN % tq == 0 and S >= WIN + tq
    kern = functools.partial(_nsa_win_kernel, q_scale=d ** -0.5)
    return pl.pallas_call(
        kern,
        grid=(B, Hkv, S // tq),
        in_specs=[
            pl.BlockSpec((1, tq, G * d), lambda b, h, i: (b, i, h)),
            pl.BlockSpec((1, S, d), lambda b, h, i: (h, b, 0)),
            pl.BlockSpec((1, dv, S), lambda b, h, i: (Hkv + h, 0, b)),
        ],
        out_specs=pl.BlockSpec((1, tq, G * d), lambda b, h, i: (b, i, h)),
        out_shape=jax.ShapeDtypeStruct((B, S, Hkv * G * d), F32),
        scratch_shapes=[pltpu.VMEM((d, G * tq), MXU_DTYPE)],
        compiler_params=_cparams("arbitrary", "arbitrary", "arbitrary"),
        name="nsa_win",
    )(qproj, k_win, vt)


def _nsa_out_kernel(oc_ref, os_ref, ow_ref, gate_ref, w_ref, h_ref, o_ref, merged_ref):
    d = NSA_HEAD_DIM
    gates = _sigmoid(gate_ref[...])
    for hq in range(NSA_Q_HEADS):
        cols = slice(hq * d, (hq + 1) * d)
        c = N_BRANCH * hq
        merged = (gates[:, c:c + 1] * oc_ref[:, cols] + gates[:, c + 1:c + 2] * os_ref[:, cols]
                  + gates[:, c + 2:c + 3] * ow_ref[:, cols])
        merged_ref[:, cols] = merged.astype(merged_ref.dtype)
    o_ref[...] = h_ref[...] + jnp.dot(merged_ref[...], w_ref[...], preferred_element_type=F32)


def nsa_out(o_cmp, o_slc, o_win, qproj, w_o, h, *, tm=256):
    T, HD = o_cmp.shape
    D = w_o.shape[1]
    tm = min(tm, T)
    assert T % tm == 0 and HD % LANE == 0
    gate_blk = HD // LANE
    row = lambda i: (i, 0)
    return pl.pallas_call(
        _nsa_out_kernel,
        grid=(T // tm,),
        in_specs=[
            pl.BlockSpec((tm, HD), row),
            pl.BlockSpec((tm, HD), row),
            pl.BlockSpec((tm, HD), row),
            pl.BlockSpec((tm, LANE), lambda i: (i, gate_blk)),
            pl.BlockSpec((HD, D), lambda i: (0, 0)),
            pl.BlockSpec((tm, D), row),
        ],
        out_specs=pl.BlockSpec((tm, D), row),
        out_shape=jax.ShapeDtypeStruct((T, D), F32),
        scratch_shapes=[pltpu.VMEM((tm, HD), w_o.dtype)],
        compiler_params=_cparams("arbitrary"),
        name="nsa_out",
    )(o_cmp, o_slc, o_win, qproj, w_o, h)


def _phase_major(x, nsel):
    B, n, H, d = x.shape
    return x.reshape(B, nsel, SLC_RATIO, H, d).transpose(0, 3, 2, 1, 4).reshape(B, H, n, d)


def _cast(w):
    return w.astype(MXU_DTYPE)


def nsa_shared_kv(h, B, S, kv_gain, w_kv, cmp_pe_k, cmp_w1_k, cmp_w2_k, cmp_pe_v, cmp_w1_v, cmp_w2_v):
    Hkv, d = NSA_KV_HEADS, NSA_HEAD_DIM
    cmp_in, k_slc, k_win, vt = kv_projection(h, S, kv_gain, _cast(w_kv))
    n = S // CMP_STRIDE
    nsel = S // SLC_BLOCK
    cmp = compress_tokens(cmp_in, B, jnp.stack([cmp_pe_k, cmp_pe_v]), _cast(jnp.stack([cmp_w1_k, cmp_w1_v])),
                          _cast(jnp.stack([cmp_w2_k, cmp_w2_v])))
    cmp = cmp.reshape(2, B, Hkv, n, d).transpose(0, 1, 3, 2, 4)
    kc = _cast(_phase_major(cmp[0], nsel))
    vct = _cast(_phase_major(cmp[1], nsel).transpose(0, 1, 3, 2))
    return kc, vct, k_slc, k_win, vt


def nsa_attention_block(h, B, S, mix_gain, w_q, w_o, shared):
    kc, vct, k_slc, k_win, vt = shared
    T = B * S
    nq = NSA_Q_HEADS * NSA_HEAD_DIM
    nqp = nq + LANE
    w_q = jnp.pad(w_q, ((0, 0), (0, nqp - w_q.shape[1])))
    qproj = norm_matmul(h, mix_gain, _cast(w_q), tm=512, tn=nqp)
    qp3 = qproj.reshape(B, S, nqp)
    o_cmp, sel = nsa_cmp_branch(qp3, kc, vct)
    o_slc = nsa_slc_branch(qp3, sel, k_slc, vt)
    o_win = nsa_win_branch(qp3, k_win, vt)
    return nsa_out(o_cmp.reshape(T, nq), o_slc.reshape(T, nq), o_win.reshape(T, nq), qproj, _cast(w_o), h)


def nsa_mixer(h, B, S, mix_gain, kv_gain, w_kv, cmp_pe_k, cmp_w1_k, cmp_w2_k, cmp_pe_v, cmp_w1_v, cmp_w2_v, w_q, w_o):
    shared = nsa_shared_kv(h, B, S, kv_gain, w_kv, cmp_pe_k, cmp_w1_k, cmp_w2_k, cmp_pe_v, cmp_w1_v, cmp_w2_v)
    return nsa_attention_block(h, B, S, mix_gain, w_q, w_o, shared)


def kernel(x, norm_mix_gain, norm_ffn_gain, ret_w_in, ret_gn_gain, ret_w_out, nsa_kv_norm_gain, nsa_w_kv, cmp_pe_k, cmp_w1_k, cmp_w2_k, cmp_pe_v, cmp_w1_v, cmp_w2_v, nsa_w_q, nsa_w_o, ffn_w_in, ffn_conv_w, ffn_conv_b, ffn_w_out, final_norm_gain):
    B, S, D = x.shape
    T = B * S
    depth = norm_mix_gain.shape[0]
    n_a = depth // 2
    h = x.reshape(T, D)
    shared = None
    for layer in range(depth):
        if layer < n_a:
            proj = norm_matmul(h, norm_mix_gain[layer], _cast(ret_w_in[layer]), tm=1024, tn=1024)
            y = retention_core(proj.reshape(B, S, -1), ret_gn_gain[layer])
            h = matmul_residual(y.reshape(T, -1), _cast(ret_w_out[layer]), h, tm=512, tn=1024)
        else:
            if layer == n_a:
                shared = nsa_shared_kv(h, B, S, nsa_kv_norm_gain, nsa_w_kv, cmp_pe_k, cmp_w1_k, cmp_w2_k,
                                       cmp_pe_v, cmp_w1_v, cmp_w2_v)
            b = layer - n_a
            h = nsa_attention_block(h, B, S, norm_mix_gain[layer], nsa_w_q[b], nsa_w_o[b], shared)
        h = conv_ffn_block(h, S, norm_ffn_gain[layer], _cast(ffn_w_in[layer]), ffn_conv_w[layer], ffn_conv_b[layer],
                           _cast(ffn_w_out[layer]), final_norm_gain, final_norm=(layer == depth - 1))
    return h.reshape(B, S, D)
```

```python
import functools

import jax
import jax.numpy as jnp
from jax import lax
from jax.experimental import pallas as pl
from jax.experimental.pallas import tpu as pltpu

F32 = jnp.float32
MXU_DTYPE = jnp.bfloat16

RMS_EPS = 1e-6
ROPE_BASE = 10000.0

RET_HEADS = 8
RET_CHUNK = 128

NSA_Q_HEADS = 16
NSA_KV_HEADS = 4
NSA_GROUP = NSA_Q_HEADS // NSA_KV_HEADS
NSA_HEAD_DIM = 128
N_BRANCH = 3
CMP_BLOCK = 32
CMP_STRIDE = 16
CMP_RATIO = CMP_BLOCK // CMP_STRIDE
SLC_BLOCK = 64
SLC_TOPK = 16
SLC_RATIO = SLC_BLOCK // CMP_STRIDE
WIN = 512
CONV_W = 3

NEG = -1e30
MASK_BIAS = -(2.0 ** 100)
PICKED = -3.0e38
SEL_FORCE = 1e30

LANE = 128
BF16_SUBLANES = 16
VMEM_LIMIT_BYTES = 56 * 1024 * 1024


def _cparams(*sem):
    return pltpu.CompilerParams(dimension_semantics=sem, vmem_limit_bytes=VMEM_LIMIT_BYTES)


def _sigmoid(x):
    return 1.0 / (1.0 + jnp.exp(-x))


def _rms_normalize(x, gain):
    ms = jnp.mean(x * x, axis=-1, keepdims=True)
    return x * lax.rsqrt(ms + RMS_EPS) * gain


def _norm_matmul_kernel(x_ref, g_ref, w_ref, o_ref, xn_ref):
    @pl.when(pl.program_id(1) == 0)
    def _():
        xn_ref[...] = _rms_normalize(x_ref[...], g_ref[...]).astype(xn_ref.dtype)

    o_ref[...] = jnp.dot(xn_ref[...], w_ref[...], preferred_element_type=F32).astype(o_ref.dtype)


def norm_matmul(h, gain, w, *, tm, tn, out_dtype=F32):
    T, D = h.shape
    N = w.shape[1]
    tm, tn = min(tm, T), min(tn, N)
    assert T % tm == 0 and N % tn == 0
    return pl.pallas_call(
        _norm_matmul_kernel,
        grid=(T // tm, N // tn),
        in_specs=[
            pl.BlockSpec((tm, D), lambda i, j: (i, 0)),
            pl.BlockSpec((1, D), lambda i, j: (0, 0)),
            pl.BlockSpec((D, tn), lambda i, j: (0, j)),
        ],
        out_specs=pl.BlockSpec((tm, tn), lambda i, j: (i, j)),
        out_shape=jax.ShapeDtypeStruct((T, N), out_dtype),
        scratch_shapes=[pltpu.VMEM((tm, D), w.dtype)],
        compiler_params=_cparams("arbitrary", "arbitrary"),
        name="norm_matmul",
    )(h, gain.reshape(1, D), w)


def _matmul_residual_kernel(y_ref, w_ref, h_ref, o_ref):
    o_ref[...] = h_ref[...] + jnp.dot(y_ref[...], w_ref[...], preferred_element_type=F32)


def matmul_residual(y, w, h, *, tm, tn):
    T, K = y.shape
    N = w.shape[1]
    tm, tn = min(tm, T), min(tn, N)
    assert T % tm == 0 and N % tn == 0
    return pl.pallas_call(
        _matmul_residual_kernel,
        grid=(T // tm, N // tn),
        in_specs=[
            pl.BlockSpec((tm, K), lambda i, j: (i, 0)),
            pl.BlockSpec((K, tn), lambda i, j: (0, j)),
            pl.BlockSpec((tm, tn), lambda i, j: (i, j)),
        ],
        out_specs=pl.BlockSpec((tm, tn), lambda i, j: (i, j)),
        out_shape=jax.ShapeDtypeStruct((T, N), F32),
        compiler_params=_cparams("arbitrary", "arbitrary"),
        name="matmul_residual",
    )(y, w, h)


def _retention_kernel(q_ref, k_ref, v_ref, g_ref, cos_ref, sin_ref, dmask_ref, qdec_ref, kdec_ref, gn_ref,
                      o_ref, state_ref, *, chunk, n_chunks, k_scale):
    @pl.when(pl.program_id(2) == 0)
    def _():
        state_ref[...] = jnp.zeros_like(state_ref)

    half = q_ref.shape[-1] // 2
    dmask = dmask_ref[0]
    qdec = qdec_ref[0]
    kdec = kdec_ref[0]
    chunk_decay = qdec[chunk - 1:chunk, :]
    gn = gn_ref[...]

    def rotate(x, cos, sin):
        x1, x2 = x[:, :half], x[:, half:]
        return jnp.concatenate([x1 * cos - x2 * sin, x1 * sin + x2 * cos], axis=-1)

    for ci in range(n_chunks):
        rows = pl.ds(ci * chunk, chunk)
        cos, sin = cos_ref[rows, :], sin_ref[rows, :]
        q = rotate(q_ref[0, rows, :], cos, sin)
        k = rotate(k_ref[0, rows, :], cos, sin) * k_scale
        v = v_ref[0, rows, :].astype(MXU_DTYPE)
        qm = q.astype(MXU_DTYPE)
        scores = lax.dot_general(qm, k.astype(MXU_DTYPE), (((1,), (1,)), ((), ())),
                                 preferred_element_type=F32) * dmask
        inner = jnp.dot(scores.astype(MXU_DTYPE), v, preferred_element_type=F32)
        state = state_ref[...]
        cross = jnp.dot(qm, state.astype(MXU_DTYPE), preferred_element_type=F32) * qdec
        k_dec_t = (k * kdec).T.astype(MXU_DTYPE)
        state_ref[...] = state * chunk_decay + jnp.dot(k_dec_t, v, preferred_element_type=F32)
        out = inner + cross
        mu = jnp.mean(out, axis=-1, keepdims=True)
        cen = out - mu
        var = jnp.mean(cen * cen, axis=-1, keepdims=True)
        normed = cen * lax.rsqrt(var + RMS_EPS) * gn
        g = g_ref[0, rows, :]
        o_ref[0, rows, :] = ((g * _sigmoid(g)) * normed).astype(o_ref.dtype)


def retention_core(proj, gn_gain, *, chunk=RET_CHUNK, tokens_per_step=512):
    B, S, P = proj.shape
    H = RET_HEADS
    dk = P // (6 * H)
    dv = 2 * dk
    tc = min(tokens_per_step, S)
    assert S % tc == 0 and tc % chunk == 0
    half = dk // 2
    pos = jnp.arange(S, dtype=F32)
    freqs = ROPE_BASE ** (-jnp.arange(half, dtype=F32) / half)
    ang = pos[:, None] * freqs[None, :]
    cos, sin = jnp.cos(ang), jnp.sin(ang)
    log_gamma = jnp.log(1.0 - 2.0 ** (-5.0 - jnp.arange(H, dtype=F32)))
    idx = jnp.arange(chunk, dtype=F32)
    diff = idx[:, None] - idx[None, :]
    dmask = jnp.where(diff >= 0, jnp.exp(jnp.maximum(diff, 0.0)[None] * log_gamma[:, None, None]), 0.0)
    qdec = jnp.exp((idx[None, :] + 1.0) * log_gamma[:, None])[:, :, None]
    kdec = jnp.exp((chunk - 1.0 - idx[None, :]) * log_gamma[:, None])[:, :, None]
    kern = functools.partial(_retention_kernel, chunk=chunk, n_chunks=tc // chunk, k_scale=dk ** -0.5)
    k_off, v_off, g_off = (H * dk) // dk, (2 * H * dk) // dv, (2 * H * dk + H * dv) // dv
    return pl.pallas_call(
        kern,
        grid=(B, H, S // tc),
        in_specs=[
            pl.BlockSpec((1, tc, dk), lambda b, h, c: (b, c, h)),
            pl.BlockSpec((1, tc, dk), lambda b, h, c: (b, c, k_off + h)),
            pl.BlockSpec((1, tc, dv), lambda b, h, c: (b, c, v_off + h)),
            pl.BlockSpec((1, tc, dv), lambda b, h, c: (b, c, g_off + h)),
            pl.BlockSpec((tc, half), lambda b, h, c: (c, 0)),
            pl.BlockSpec((tc, half), lambda b, h, c: (c, 0)),
            pl.BlockSpec((1, chunk, chunk), lambda b, h, c: (h, 0, 0)),
            pl.BlockSpec((1, chunk, 1), lambda b, h, c: (h, 0, 0)),
            pl.BlockSpec((1, chunk, 1), lambda b, h, c: (h, 0, 0)),
            pl.BlockSpec((1, dv), lambda b, h, c: (0, h)),
        ],
        out_specs=pl.BlockSpec((1, tc, dv), lambda b, h, c: (b, c, h)),
        out_shape=jax.ShapeDtypeStruct((B, S, H * dv), MXU_DTYPE),
        scratch_shapes=[pltpu.VMEM((dk, dv), F32)],
        compiler_params=_cparams("arbitrary", "arbitrary", "arbitrary"),
        name="retention_core",
    )(proj, proj, proj, proj, cos, sin, dmask, qdec, kdec, gn_gain.reshape(1, H * dv))


def _conv_ffn_kernel(x_ref, g_ref, wa_ref, wb_ref, cwa_ref, cwb_ref, cba_ref, cbb_ref, wo_ref, fg_ref,
                     o_ref, xn_ref, acc_ref, act_ref, carry_a_ref, carry_b_ref, *, nj, tiles_per_seq, final_norm):
    i, j = pl.program_id(0), pl.program_id(1)
    tm = x_ref.shape[0]

    @pl.when((i == 0) & (j == 0))
    def _():
        carry_a_ref[...] = jnp.zeros_like(carry_a_ref)
        carry_b_ref[...] = jnp.zeros_like(carry_b_ref)

    @pl.when(j == 0)
    def _():
        xn_ref[...] = _rms_normalize(x_ref[...], g_ref[...]).astype(xn_ref.dtype)
        acc_ref[...] = jnp.zeros_like(acc_ref)

    seq_start = (i % tiles_per_seq) == 0

    def causal_conv(w_ref, cw_ref, cb_ref, carry_ref):
        u = jnp.dot(xn_ref[...], w_ref[...], preferred_element_type=F32)
        prev = jnp.where(seq_start, 0.0, carry_ref[j])
        carry_ref[j] = u[tm - 8:, :]
        row = lax.broadcasted_iota(jnp.int32, u.shape, 0)
        u1 = jnp.where(row == 0, prev[7:8, :], pltpu.roll(u, 1, 0))
        u2 = jnp.where(row == 0, prev[6:7, :], jnp.where(row == 1, prev[7:8, :], pltpu.roll(u, 2, 0)))
        cw = cw_ref[...]
        return ((cb_ref[...] + u2 * cw[0:1, :]) + u1 * cw[1:2, :]) + u * cw[2:3, :]

    def hidden_block(slot):
        a = causal_conv(wa_ref, cwa_ref, cba_ref, carry_a_ref)
        b = causal_conv(wb_ref, cwb_ref, cbb_ref, carry_b_ref)
        act_ref[slot] = ((a * _sigmoid(a)) * b).astype(act_ref.dtype)

    def project(slot):
        acc_ref[...] += jnp.dot(act_ref[slot], wo_ref[...], preferred_element_type=F32)

    @pl.when(j == 0)
    def _():
        hidden_block(0)

    for parity in range(2):
        @pl.when((j > 0) & (j < nj) & ((j & 1) == parity))
        def _():
            project(1 - parity)
            hidden_block(parity)

    @pl.when(j == nj)
    def _():
        project((nj - 1) % 2)
        out = x_ref[...] + acc_ref[...]
        if final_norm:
            out = _rms_normalize(out, fg_ref[...])
        o_ref[...] = out


def conv_ffn_block(h, seq_len, norm_gain, w_in, conv_w, conv_b, w_out, final_gain, *, final_norm, tm=512, tn=512):
    T, D = h.shape
    F = w_out.shape[0]
    tm, tn = min(tm, seq_len), min(tn, F)
    assert seq_len % tm == 0 and F % tn == 0 and tm >= 8
    nj = F // tn
    kern = functools.partial(_conv_ffn_kernel, nj=nj, tiles_per_seq=seq_len // tm, final_norm=final_norm)
    blk = lambda j: jnp.minimum(j, nj - 1)
    prev = lambda j: jnp.maximum(j - 1, 0)
    return pl.pallas_call(
        kern,
        grid=(T // tm, nj + 1),
        in_specs=[
            pl.BlockSpec((tm, D), lambda i, j: (i, 0)),
            pl.BlockSpec((1, D), lambda i, j: (0, 0)),
            pl.BlockSpec((D, tn), lambda i, j: (0, blk(j))),
            pl.BlockSpec((D, tn), lambda i, j: (0, nj + blk(j))),
            pl.BlockSpec((CONV_W, tn), lambda i, j: (0, blk(j))),
            pl.BlockSpec((CONV_W, tn), lambda i, j: (0, nj + blk(j))),
            pl.BlockSpec((1, tn), lambda i, j: (0, blk(j))),
            pl.BlockSpec((1, tn), lambda i, j: (0, nj + blk(j))),
            pl.BlockSpec((tn, D), lambda i, j: (prev(j), 0)),
            pl.BlockSpec((1, D), lambda i, j: (0, 0)),
        ],
        out_specs=pl.BlockSpec((tm, D), lambda i, j: (i, 0)),
        out_shape=jax.ShapeDtypeStruct((T, D), F32),
        scratch_shapes=[
            pltpu.VMEM((tm, D), w_in.dtype),
            pltpu.VMEM((tm, D), F32),
            pltpu.VMEM((2, tm, tn), w_out.dtype),
            pltpu.VMEM((nj, 8, tn), F32),
            pltpu.VMEM((nj, 8, tn), F32),
        ],
        compiler_params=_cparams("arbitrary", "arbitrary"),
        name="conv_ffn",
    )(h, norm_gain.reshape(1, D), w_in, w_in, conv_w, conv_w, conv_b.reshape(1, 2 * F), conv_b.reshape(1, 2 * F),
      w_out, final_gain.reshape(1, D))


def _kv_proj_kernel(x_ref, g_ref, w_ref, cmp_ref, kslc_ref, kwin_ref, vt_ref, *, seq_len):
    tm = x_ref.shape[0]
    d = NSA_HEAD_DIM
    Hkv = NSA_KV_HEADS
    xn = _rms_normalize(x_ref[...], g_ref[...]).astype(w_ref.dtype)
    res = jnp.dot(xn, w_ref[...], preferred_element_type=F32)
    col = lambda branch, hh: res[:, (branch * Hkv + hh) * d:(branch * Hkv + hh + 1) * d]
    pos = (pl.program_id(0) * tm) % seq_len + lax.broadcasted_iota(jnp.int32, (tm, LANE), 0)
    blk = (pos >> (SLC_BLOCK.bit_length() - 1)) & (LANE - 1)
    onehot = jnp.where(blk == lax.broadcasted_iota(jnp.int32, (tm, LANE), 1), 1.0, 0.0).astype(kslc_ref.dtype)
    ones = jnp.ones((BF16_SUBLANES, tm), vt_ref.dtype)
    for hh in range(Hkv):
        cmp_ref[hh] = col(0, hh)
        cmp_ref[Hkv + hh] = col(1, hh)
        kslc_ref[hh, :, :d] = col(2, hh).astype(kslc_ref.dtype)
        kslc_ref[hh, :, d:] = onehot
        kwin_ref[hh] = col(4, hh).astype(kwin_ref.dtype)
        vt_ref[hh, :d, :] = col(3, hh).T.astype(vt_ref.dtype)
        vt_ref[hh, d:, :] = ones
        vt_ref[Hkv + hh, :d, :] = col(5, hh).T.astype(vt_ref.dtype)
        vt_ref[Hkv + hh, d:, :] = ones


def kv_projection(h, seq_len, gain, w_kv, *, tm=256):
    T, D = h.shape
    Hkv, d = NSA_KV_HEADS, NSA_HEAD_DIM
    N = w_kv.shape[1]
    tm = min(tm, seq_len)
    assert seq_len % tm == 0 and N == 2 * N_BRANCH * Hkv * d and tm % LANE == 0
    row3 = lambda i: (0, i, 0)
    return pl.pallas_call(
        functools.partial(_kv_proj_kernel, seq_len=seq_len),
        grid=(T // tm,),
        in_specs=[
            pl.BlockSpec((tm, D), lambda i: (i, 0)),
            pl.BlockSpec((1, D), lambda i: (0, 0)),
            pl.BlockSpec((D, N), lambda i: (0, 0)),
        ],
        out_specs=[
            pl.BlockSpec((2 * Hkv, tm, d), row3),
            pl.BlockSpec((Hkv, tm, d + LANE), row3),
            pl.BlockSpec((Hkv, tm, d), row3),
            pl.BlockSpec((2 * Hkv, d + BF16_SUBLANES, tm), lambda i: (0, 0, i)),
        ],
        out_shape=[
            jax.ShapeDtypeStruct((2 * Hkv, T, d), F32),
            jax.ShapeDtypeStruct((Hkv, T, d + LANE), MXU_DTYPE),
            jax.ShapeDtypeStruct((Hkv, T, d), MXU_DTYPE),
            jax.ShapeDtypeStruct((2 * Hkv, d + BF16_SUBLANES, T), MXU_DTYPE),
        ],
        compiler_params=_cparams("arbitrary"),
        name="kv_projection",
    )(h, gain.reshape(1, D), w_kv)


def _compress_kernel(x_ref, pe_ref, w1_ref, w2_ref, o_ref):
    d = x_ref.shape[2]
    n = o_ref.shape[2]
    first = jnp.zeros((n, w1_ref.shape[2]), F32)
    second = jnp.zeros((n, w1_ref.shape[2]), F32)
    for p in range(CMP_STRIDE):
        xp = x_ref[0, pl.ds(p, n, stride=CMP_STRIDE), :]
        q = CMP_STRIDE + p
        first = first + jnp.dot((xp + pe_ref[0, p:p + 1, :]).astype(MXU_DTYPE), w1_ref[0, p * d:(p + 1) * d, :],
                                preferred_element_type=F32)
        second = second + jnp.dot((xp + pe_ref[0, q:q + 1, :]).astype(MXU_DTYPE), w1_ref[0, q * d:(q + 1) * d, :],
                                  preferred_element_type=F32)
    hid = first + pltpu.roll(second, n - 1, 0)
    gelu = 0.5 * hid * (1.0 + jnp.tanh(0.7978845608028654 * (hid + 0.044715 * (hid * hid * hid))))
    out = jnp.dot(gelu.astype(MXU_DTYPE), w2_ref[0], preferred_element_type=F32)
    row = lax.broadcasted_iota(jnp.int32, out.shape, 0)
    o_ref[0, 0] = jnp.where(row < n - 1, out, 0.0)


def compress_tokens(x, batch, pe, w1, w2):
    H2, T, d = x.shape
    Hkv = H2 // 2
    S = T // batch
    n = S // CMP_STRIDE
    hid = w2.shape[1]
    return pl.pallas_call(
        _compress_kernel,
        grid=(2, batch, Hkv),
        in_specs=[
            pl.BlockSpec((1, S, d), lambda s, b, hh: (s * Hkv + hh, b, 0)),
            pl.BlockSpec((1, CMP_BLOCK, d), lambda s, b, hh: (s, 0, 0)),
            pl.BlockSpec((1, CMP_BLOCK * d, hid), lambda s, b, hh: (s, 0, 0)),
            pl.BlockSpec((1, hid, d), lambda s, b, hh: (s, 0, 0)),
        ],
        out_specs=pl.BlockSpec((1, 1, n, d), lambda s, b, hh: (s, b * Hkv + hh, 0, 0)),
        out_shape=jax.ShapeDtypeStruct((2, batch * Hkv, n, d), F32),
        compiler_params=_cparams("arbitrary", "arbitrary", "arbitrary"),
        name="compress_tokens",
    )(x, pe, w1, w2)


def _nsa_cmp_kernel(q_ref, kc_ref, vct_ref, o_ref, sel_ref, qt_ref, *, q_scale, top_k):
    i = pl.program_id(2)
    tq = q_ref.shape[1]
    d = NSA_HEAD_DIM
    G = NSA_GROUP
    ncp = kc_ref.shape[2]
    nsel = ncp // SLC_RATIO
    t0 = i * tq
    for g in range(G):
        qt_ref[:, g * tq:(g + 1) * tq] = (q_ref[0, :, g * d:(g + 1) * d] * q_scale).T.astype(qt_ref.dtype)
    row = lax.broadcasted_iota(jnp.int32, (ncp, tq), 0)
    tok = t0 + lax.broadcasted_iota(jnp.int32, (ncp, tq), 1)
    cblk = SLC_RATIO * (row & (nsel - 1)) + (row >> (nsel.bit_length() - 1))
    bias = jnp.where(cblk * CMP_STRIDE + (CMP_BLOCK - 1) <= tok, 0.0, NEG)
    s = jnp.dot(kc_ref[0, 0], qt_ref[...], preferred_element_type=F32) + jnp.concatenate([bias] * G, axis=1)
    e = jnp.exp(s - jnp.max(s, axis=0, keepdims=True))
    denom = jnp.sum(e, axis=0, keepdims=True)
    any_valid = jnp.concatenate([tok[0:1, :] >= CMP_BLOCK - 1] * G, axis=1)
    p = e * jnp.where(any_valid, 1.0 / denom, 0.0)
    out_t = jnp.dot(vct_ref[0, 0], p.astype(MXU_DTYPE), preferred_element_type=F32)
    imp = None
    for g in range(G):
        o_ref[0, :, g * d:(g + 1) * d] = out_t[:, g * tq:(g + 1) * tq].T
        pg = p[:, g * tq:(g + 1) * tq]
        imp = pg if imp is None else imp + pg
    ph = [imp[r * nsel:(r + 1) * nsel, :] for r in range(SLC_RATIO)]
    j = lax.broadcasted_iota(jnp.int32, (nsel, tq), 0)
    p_slc = jnp.where(j == 0, 0.0, pltpu.roll(ph[SLC_RATIO - 1], 1, 0))
    for r in range(SLC_RATIO):
        p_slc = p_slc + ph[r]
    cur = (t0 + lax.broadcasted_iota(jnp.int32, (nsel, tq), 1)) >> (SLC_BLOCK.bit_length() - 1)
    causal = j <= cur

    @pl.when(t0 < top_k * SLC_BLOCK)
    def _():
        sel_ref[0, 0] = jnp.where(causal, 0.0, MASK_BIAS).astype(sel_ref.dtype)

    @pl.when(t0 >= top_k * SLC_BLOCK)
    def _():
        forced = (j == 0) | (j == cur) | (j == cur - 1)
        score = jnp.where(causal & jnp.logical_not(forced), p_slc, PICKED)
        jf = j.astype(F32)
        for _ in range(top_k - 3):
            m = jnp.max(score, axis=0, keepdims=True)
            first = jnp.min(jnp.where(score == m, jf, float(nsel)), axis=0, keepdims=True)
            score = jnp.where(jf == first, PICKED, score)
        sel_ref[0, 0] = jnp.where((score == PICKED) & causal, 0.0, MASK_BIAS).astype(sel_ref.dtype)


def nsa_cmp_branch(qproj, kc, vct, *, tq=128):
    B, S, _ = qproj.shape
    Hkv, G, d = NSA_KV_HEADS, NSA_GROUP, NSA_HEAD_DIM
    ncp = kc.shape[2]
    nsel = ncp // SLC_RATIO
    top_k = min(SLC_TOPK, nsel)
    tq = min(tq, S)
    assert S % tq == 0 and nsel & (nsel - 1) == 0 and tq & (tq - 1) == 0
    assert (top_k * SLC_BLOCK) % tq == 0 and top_k >= 3
    kern = functools.partial(_nsa_cmp_kernel, q_scale=d ** -0.5, top_k=top_k)
    return pl.pallas_call(
        kern,
        grid=(B, Hkv, S // tq),
        in_specs=[
            pl.BlockSpec((1, tq, G * d), lambda b, h, i: (b, i, h)),
            pl.BlockSpec((1, 1, ncp, d), lambda b, h, i: (b, h, 0, 0)),
            pl.BlockSpec((1, 1, d, ncp), lambda b, h, i: (b, h, 0, 0)),
        ],
        out_specs=[
            pl.BlockSpec((1, tq, G * d), lambda b, h, i: (b, i, h)),
            pl.BlockSpec((1, 1, nsel, tq), lambda b, h, i: (b, h, 0, i)),
        ],
        out_shape=[
            jax.ShapeDtypeStruct((B, S, Hkv * G * d), F32),
            jax.ShapeDtypeStruct((B, Hkv, nsel, S), MXU_DTYPE),
        ],
        scratch_shapes=[pltpu.VMEM((d, G * tq), MXU_DTYPE)],
        compiler_params=_cparams("arbitrary", "arbitrary", "arbitrary"),
        name="nsa_cmp",
    )(qproj, kc, vct)


def _nsa_slc_kernel(q_ref, sel_ref, k_ref, vt_ref, o_ref, qa_ref, m_ref, acc_ref, s_ref, *, q_scale, tk):
    i = pl.program_id(2)
    tq = q_ref.shape[1]
    d = NSA_HEAD_DIM
    G = NSA_GROUP
    nblk = sel_ref.shape[2]
    span = LANE * SLC_BLOCK
    n_halves = pl.cdiv(nblk, LANE)

    m_ref[...] = jnp.full_like(m_ref, PICKED)
    acc_ref[...] = jnp.zeros_like(acc_ref)
    for g in range(G):
        qa_ref[:d, g * tq:(g + 1) * tq] = (q_ref[0, :, g * d:(g + 1) * d] * q_scale).T.astype(qa_ref.dtype)

    def scores(kt_idx):
        k0 = pl.multiple_of(kt_idx * tk, tk)
        return jnp.dot(k_ref[0, pl.ds(k0, tk), :], qa_ref[...], preferred_element_type=F32)

    def accumulate(kt_idx, s):
        k0 = pl.multiple_of(kt_idx * tk, tk)
        m_old = m_ref[...]
        m_new = jnp.maximum(m_old, jnp.max(s, axis=0, keepdims=True))
        alpha = jnp.exp(m_old - m_new)
        p = jnp.exp(s - m_new)
        acc_ref[...] = alpha * acc_ref[...] + jnp.dot(vt_ref[0, :, pl.ds(k0, tk)], p.astype(MXU_DTYPE),
                                                      preferred_element_type=F32)
        m_ref[...] = m_new

    last = ((i + 1) * tq - 1) // tk
    tiles_per_half = span // tk
    for hf in range(n_halves):
        lo = hf * tiles_per_half
        hi = jnp.minimum(last, lo + tiles_per_half)

        @pl.when(lo <= last)
        def _():
            sel_half = sel_ref[0, 0, hf * LANE:(hf + 1) * LANE, :]
            for g in range(G):
                qa_ref[d:, g * tq:(g + 1) * tq] = sel_half
            n = hi - lo
            odd = (n & 1) == 1
            diag_here = last < lo + tiles_per_half

            def accumulate_diagonal(s):
                col = lax.broadcasted_iota(jnp.int32, s.shape, 1)
                tok = i * tq + (col & (tq - 1))
                key = last * tk + lax.broadcasted_iota(jnp.int32, s.shape, 0)
                accumulate(last, jnp.where(key <= tok, s, NEG))

            s_ref[0] = scores(lo)

            def pair(pi, carry):
                t = lo + 2 * pi
                s_ref[1] = scores(t + 1)
                accumulate(t, s_ref[0])
                s_ref[0] = scores(t + 2)
                accumulate(t + 1, s_ref[1])
                return carry

            lax.fori_loop(0, n >> 1, pair, 0)

            @pl.when(odd & diag_here)
            def _():
                s_ref[1] = scores(last)
                accumulate(hi - 1, s_ref[0])
                accumulate_diagonal(s_ref[1])

            @pl.when(odd & jnp.logical_not(diag_here))
            def _():
                accumulate(hi - 1, s_ref[0])

            @pl.when(jnp.logical_not(odd) & diag_here)
            def _():
                accumulate_diagonal(s_ref[0])

    out = acc_ref[:d, :] * (1.0 / acc_ref[d:d + 1, :])
    for g in range(G):
        o_ref[0, :, g * d:(g + 1) * d] = out[:, g * tq:(g + 1) * tq].T


def nsa_slc_branch(qproj, sel, k_aug, vt, *, tq=256, tk=1024):
    B, S, _ = qproj.shape
    Hkv, G, d = NSA_KV_HEADS, NSA_GROUP, NSA_HEAD_DIM
    nblk = sel.shape[2]
    dv = vt.shape[1]
    tq, tk = min(tq, S), min(tk, S)
    assert S % tq == 0 and S % tk == 0 and tk % tq == 0 and tq & (tq - 1) == 0
    assert nblk % LANE == 0 and (LANE * SLC_BLOCK) % tk == 0
    kern = functools.partial(_nsa_slc_kernel, q_scale=d ** -0.5, tk=tk)
    return pl.pallas_call(
        kern,
        grid=(B, Hkv, S // tq),
        in_specs=[
            pl.BlockSpec((1, tq, G * d), lambda b, h, i: (b, i, h)),
            pl.BlockSpec((1, 1, nblk, tq), lambda b, h, i: (b, h, 0, i)),
            pl.BlockSpec((1, S, d + LANE), lambda b, h, i: (h, b, 0)),
            pl.BlockSpec((1, dv, S), lambda b, h, i: (h, 0, b)),
        ],
        out_specs=pl.BlockSpec((1, tq, G * d), lambda b, h, i: (b, i, h)),
        out_shape=jax.ShapeDtypeStruct((B, S, Hkv * G * d), F32),
        scratch_shapes=[
            pltpu.VMEM((d + LANE, G * tq), MXU_DTYPE),
            pltpu.VMEM((1, G * tq), F32),
            pltpu.VMEM((dv, G * tq), F32),
            pltpu.VMEM((2, tk, G * tq), F32),
        ],
        compiler_params=_cparams("arbitrary", "arbitrary", "arbitrary"),
        name="nsa_slc",
    )(qproj, sel, k_aug, vt)


def _nsa_win_kernel(q_ref, k_ref, vt_ref, bias_ref, o_ref, qt_ref, *, q_scale):
    i = pl.program_id(2)
    tq = q_ref.shape[1]
    d = NSA_HEAD_DIM
    G = NSA_GROUP
    nk = WIN + tq
    k0 = pl.multiple_of(jnp.maximum(i * tq - WIN, 0), tq)
    for g in range(G):
        qt_ref[:, g * tq:(g + 1) * tq] = (q_ref[0, :, g * d:(g + 1) * d] * q_scale).T.astype(qt_ref.dtype)
    s = (jnp.dot(k_ref[0, pl.ds(k0, nk), :], qt_ref[...], preferred_element_type=F32)
         + jnp.concatenate([bias_ref[0]] * G, axis=1))
    e = jnp.exp(s - jnp.max(s, axis=0, keepdims=True))
    p = e * (1.0 / jnp.sum(e, axis=0, keepdims=True))
    out_t = jnp.dot(vt_ref[0, :d, pl.ds(k0, nk)], p.astype(MXU_DTYPE), preferred_element_type=F32)
    for g in range(G):
        o_ref[0, :, g * d:(g + 1) * d] = out_t[:, g * tq:(g + 1) * tq].T


def nsa_win_branch(qproj, k_win, vt, *, tq=256):
    B, S, _ = qproj.shape
    Hkv, G, d = NSA_KV_HEADS, NSA_GROUP, NSA_HEAD_DIM
    dv = vt.shape[1]
    tq = min(tq, S)
    assert S % tq == 0 and tq % LANE == 0 and tq & (tq - 1) == 0 and WIN % tq == 0 and S >= WIN + tq
    kern = functools.partial(_nsa_win_kernel, q_scale=d ** -0.5)
    n_clipped = WIN // tq
    t0 = jnp.minimum(jnp.arange(n_clipped + 1) * tq, WIN)[:, None, None]
    pos = (t0 - jnp.minimum(t0, WIN)) + jnp.arange(WIN + tq)[None, :, None]
    tok = t0 + jnp.arange(tq)[None, None, :]
    bias = jnp.where((pos <= tok) & (pos > tok - WIN), 0.0, NEG).astype(F32)
    return pl.pallas_call(
        kern,
        grid=(B, Hkv, S // tq),
        in_specs=[
            pl.BlockSpec((1, tq, G * d), lambda b, h, i: (b, i, h)),
            pl.BlockSpec((1, S, d), lambda b, h, i: (h, b, 0)),
            pl.BlockSpec((1, dv, S), lambda b, h, i: (Hkv + h, 0, b)),
            pl.BlockSpec((1, WIN + tq, tq), lambda b, h, i: (jnp.minimum(i, n_clipped), 0, 0)),
        ],
        out_specs=pl.BlockSpec((1, tq, G * d), lambda b, h, i: (b, i, h)),
        out_shape=jax.ShapeDtypeStruct((B, S, Hkv * G * d), F32),
        scratch_shapes=[pltpu.VMEM((d, G * tq), MXU_DTYPE)],
        compiler_params=_cparams("arbitrary", "arbitrary", "arbitrary"),
        name="nsa_win",
    )(qproj, k_win, vt, bias)


def _nsa_out_kernel(oc_ref, os_ref, ow_ref, gate_ref, w_ref, h_ref, o_ref, merged_ref):
    d = NSA_HEAD_DIM
    gates = _sigmoid(gate_ref[...])
    for hq in range(NSA_Q_HEADS):
        cols = slice(hq * d, (hq + 1) * d)
        c = N_BRANCH * hq
        merged = (gates[:, c:c + 1] * oc_ref[:, cols] + gates[:, c + 1:c + 2] * os_ref[:, cols]
                  + gates[:, c + 2:c + 3] * ow_ref[:, cols])
        merged_ref[:, cols] = merged.astype(merged_ref.dtype)
    o_ref[...] = h_ref[...] + jnp.dot(merged_ref[...], w_ref[...], preferred_element_type=F32)


def nsa_out(o_cmp, o_slc, o_win, qproj, w_o, h, *, tm=256):
    T, HD = o_cmp.shape
    D = w_o.shape[1]
    tm = min(tm, T)
    assert T % tm == 0 and HD % LANE == 0
    gate_blk = HD // LANE
    row = lambda i: (i, 0)
    return pl.pallas_call(
        _nsa_out_kernel,
        grid=(T // tm,),
        in_specs=[
            pl.BlockSpec((tm, HD), row),
            pl.BlockSpec((tm, HD), row),
            pl.BlockSpec((tm, HD), row),
            pl.BlockSpec((tm, LANE), lambda i: (i, gate_blk)),
            pl.BlockSpec((HD, D), lambda i: (0, 0)),
            pl.BlockSpec((tm, D), row),
        ],
        out_specs=pl.BlockSpec((tm, D), row),
        out_shape=jax.ShapeDtypeStruct((T, D), F32),
        scratch_shapes=[pltpu.VMEM((tm, HD), w_o.dtype)],
        compiler_params=_cparams("arbitrary"),
        name="nsa_out",
    )(o_cmp, o_slc, o_win, qproj, w_o, h)


def _phase_major(x, nsel):
    B, n, H, d = x.shape
    return x.reshape(B, nsel, SLC_RATIO, H, d).transpose(0, 3, 2, 1, 4).reshape(B, H, n, d)


def _cast(w):
    return w.astype(MXU_DTYPE)


def nsa_shared_kv(h, B, S, kv_gain, w_kv, cmp_pe_k, cmp_w1_k, cmp_w2_k, cmp_pe_v, cmp_w1_v, cmp_w2_v):
    Hkv, d = NSA_KV_HEADS, NSA_HEAD_DIM
    cmp_in, k_slc, k_win, vt = kv_projection(h, S, kv_gain, _cast(w_kv))
    n = S // CMP_STRIDE
    nsel = S // SLC_BLOCK
    cmp = compress_tokens(cmp_in, B, jnp.stack([cmp_pe_k, cmp_pe_v]), _cast(jnp.stack([cmp_w1_k, cmp_w1_v])),
                          _cast(jnp.stack([cmp_w2_k, cmp_w2_v])))
    cmp = cmp.reshape(2, B, Hkv, n, d).transpose(0, 1, 3, 2, 4)
    kc = _cast(_phase_major(cmp[0], nsel))
    vct = _cast(_phase_major(cmp[1], nsel).transpose(0, 1, 3, 2))
    return kc, vct, k_slc, k_win, vt


def nsa_attention_block(h, B, S, mix_gain, w_q, w_o, shared):
    kc, vct, k_slc, k_win, vt = shared
    T = B * S
    nq = NSA_Q_HEADS * NSA_HEAD_DIM
    nqp = nq + LANE
    w_q = jnp.pad(w_q, ((0, 0), (0, nqp - w_q.shape[1])))
    qproj = norm_matmul(h, mix_gain, _cast(w_q), tm=512, tn=nqp)
    qp3 = qproj.reshape(B, S, nqp)
    o_cmp, sel = nsa_cmp_branch(qp3, kc, vct)
    o_slc = nsa_slc_branch(qp3, sel, k_slc, vt)
    o_win = nsa_win_branch(qp3, k_win, vt)
    return nsa_out(o_cmp.reshape(T, nq), o_slc.reshape(T, nq), o_win.reshape(T, nq), qproj, _cast(w_o), h)


def nsa_mixer(h, B, S, mix_gain, kv_gain, w_kv, cmp_pe_k, cmp_w1_k, cmp_w2_k, cmp_pe_v, cmp_w1_v, cmp_w2_v, w_q, w_o):
    shared = nsa_shared_kv(h, B, S, kv_gain, w_kv, cmp_pe_k, cmp_w1_k, cmp_w2_k, cmp_pe_v, cmp_w1_v, cmp_w2_v)
    return nsa_attention_block(h, B, S, mix_gain, w_q, w_o, shared)


def kernel(x, norm_mix_gain, norm_ffn_gain, ret_w_in, ret_gn_gain, ret_w_out, nsa_kv_norm_gain, nsa_w_kv, cmp_pe_k, cmp_w1_k, cmp_w2_k, cmp_pe_v, cmp_w1_v, cmp_w2_v, nsa_w_q, nsa_w_o, ffn_w_in, ffn_conv_w, ffn_conv_b, ffn_w_out, final_norm_gain):
    B, S, D = x.shape
    T = B * S
    depth = norm_mix_gain.shape[0]
    n_a = depth // 2
    h = x.reshape(T, D)
    shared = None
    for layer in range(depth):
        if layer < n_a:
            proj = norm_matmul(h, norm_mix_gain[layer], _cast(ret_w_in[layer]), tm=1024, tn=1024)
            y = retention_core(proj.reshape(B, S, -1), ret_gn_gain[layer])
            h = matmul_residual(y.reshape(T, -1), _cast(ret_w_out[layer]), h, tm=512, tn=1024)
        else:
            if layer == n_a:
                shared = nsa_shared_kv(h, B, S, nsa_kv_norm_gain, nsa_w_kv, cmp_pe_k, cmp_w1_k, cmp_w2_k,
                                       cmp_pe_v, cmp_w1_v, cmp_w2_v)
            b = layer - n_a
            h = nsa_attention_block(h, B, S, norm_mix_gain[layer], nsa_w_q[b], nsa_w_o[b], shared)
        h = conv_ffn_block(h, S, norm_ffn_gain[layer], _cast(ffn_w_in[layer]), ffn_conv_w[layer], ffn_conv_b[layer],
                           _cast(ffn_w_out[layer]), final_norm_gain, final_norm=(layer == depth - 1))
    return h.reshape(B, S, D)
```

```python
import functools

import jax
import jax.numpy as jnp
from jax import lax
from jax.experimental import pallas as pl
from jax.experimental.pallas import tpu as pltpu

F32 = jnp.float32
MXU_DTYPE = jnp.bfloat16

RMS_EPS = 1e-6
ROPE_BASE = 10000.0

RET_HEADS = 8
RET_CHUNK = 128

NSA_Q_HEADS = 16
NSA_KV_HEADS = 4
NSA_GROUP = NSA_Q_HEADS // NSA_KV_HEADS
NSA_HEAD_DIM = 128
N_BRANCH = 3
CMP_BLOCK = 32
CMP_STRIDE = 16
CMP_RATIO = CMP_BLOCK // CMP_STRIDE
SLC_BLOCK = 64
SLC_TOPK = 16
SLC_RATIO = SLC_BLOCK // CMP_STRIDE
WIN = 512
CONV_W = 3

NEG = -1e30
MASK_BIAS = -(2.0 ** 100)
PICKED = -3.0e38
SEL_FORCE = 1e30

LANE = 128
BF16_SUBLANES = 16
VMEM_LIMIT_BYTES = 56 * 1024 * 1024


def _cparams(*sem):
    return pltpu.CompilerParams(dimension_semantics=sem, vmem_limit_bytes=VMEM_LIMIT_BYTES)


def _sigmoid(x):
    return 1.0 / (1.0 + jnp.exp(-x))


def _rms_normalize(x, gain):
    ms = jnp.mean(x * x, axis=-1, keepdims=True)
    return x * lax.rsqrt(ms + RMS_EPS) * gain


def _norm_matmul_kernel(x_ref, g_ref, w_ref, o_ref, xn_ref):
    @pl.when(pl.program_id(1) == 0)
    def _():
        xn_ref[...] = _rms_normalize(x_ref[...], g_ref[...]).astype(xn_ref.dtype)

    o_ref[...] = jnp.dot(xn_ref[...], w_ref[...], preferred_element_type=F32).astype(o_ref.dtype)


def norm_matmul(h, gain, w, *, tm, tn, out_dtype=F32):
    T, D = h.shape
    N = w.shape[1]
    tm, tn = min(tm, T), min(tn, N)
    assert T % tm == 0 and N % tn == 0
    return pl.pallas_call(
        _norm_matmul_kernel,
        grid=(T // tm, N // tn),
        in_specs=[
            pl.BlockSpec((tm, D), lambda i, j: (i, 0)),
            pl.BlockSpec((1, D), lambda i, j: (0, 0)),
            pl.BlockSpec((D, tn), lambda i, j: (0, j)),
        ],
        out_specs=pl.BlockSpec((tm, tn), lambda i, j: (i, j)),
        out_shape=jax.ShapeDtypeStruct((T, N), out_dtype),
        scratch_shapes=[pltpu.VMEM((tm, D), w.dtype)],
        compiler_params=_cparams("arbitrary", "arbitrary"),
        name="norm_matmul",
    )(h, gain.reshape(1, D), w)


def _matmul_residual_kernel(y_ref, w_ref, h_ref, o_ref):
    o_ref[...] = h_ref[...] + jnp.dot(y_ref[...], w_ref[...], preferred_element_type=F32)


def matmul_residual(y, w, h, *, tm, tn):
    T, K = y.shape
    N = w.shape[1]
    tm, tn = min(tm, T), min(tn, N)
    assert T % tm == 0 and N % tn == 0
    return pl.pallas_call(
        _matmul_residual_kernel,
        grid=(T // tm, N // tn),
        in_specs=[
            pl.BlockSpec((tm, K), lambda i, j: (i, 0)),
            pl.BlockSpec((K, tn), lambda i, j: (0, j)),
            pl.BlockSpec((tm, tn), lambda i, j: (i, j)),
        ],
        out_specs=pl.BlockSpec((tm, tn), lambda i, j: (i, j)),
        out_shape=jax.ShapeDtypeStruct((T, N), F32),
        compiler_params=_cparams("arbitrary", "arbitrary"),
        name="matmul_residual",
    )(y, w, h)


def _retention_kernel(q_ref, k_ref, v_ref, g_ref, cos_ref, sin_ref, dmask_ref, qdec_ref, kdec_ref, gn_ref,
                      o_ref, state_ref, *, chunk, n_chunks, k_scale):
    @pl.when(pl.program_id(2) == 0)
    def _():
        state_ref[...] = jnp.zeros_like(state_ref)

    half = q_ref.shape[-1] // 2
    dmask = dmask_ref[0]
    qdec = qdec_ref[0]
    kdec = kdec_ref[0]
    chunk_decay = qdec[chunk - 1:chunk, :]
    gn = gn_ref[...]

    def rotate(x, cos, sin):
        x1, x2 = x[:, :half], x[:, half:]
        return jnp.concatenate([x1 * cos - x2 * sin, x1 * sin + x2 * cos], axis=-1)

    for ci in range(n_chunks):
        rows = pl.ds(ci * chunk, chunk)
        cos, sin = cos_ref[rows, :], sin_ref[rows, :]
        q = rotate(q_ref[0, rows, :], cos, sin)
        k = rotate(k_ref[0, rows, :], cos, sin) * k_scale
        v = v_ref[0, rows, :].astype(MXU_DTYPE)
        qm = q.astype(MXU_DTYPE)
        scores = lax.dot_general(qm, k.astype(MXU_DTYPE), (((1,), (1,)), ((), ())),
                                 preferred_element_type=F32) * dmask
        inner = jnp.dot(scores.astype(MXU_DTYPE), v, preferred_element_type=F32)
        state = state_ref[...]
        cross = jnp.dot(qm, state.astype(MXU_DTYPE), preferred_element_type=F32) * qdec
        k_dec_t = (k * kdec).T.astype(MXU_DTYPE)
        state_ref[...] = state * chunk_decay + jnp.dot(k_dec_t, v, preferred_element_type=F32)
        out = inner + cross
        mu = jnp.mean(out, axis=-1, keepdims=True)
        cen = out - mu
        var = jnp.mean(cen * cen, axis=-1, keepdims=True)
        normed = cen * lax.rsqrt(var + RMS_EPS) * gn
        g = g_ref[0, rows, :]
        o_ref[0, rows, :] = ((g * _sigmoid(g)) * normed).astype(o_ref.dtype)


def retention_core(proj, gn_gain, *, chunk=2 * RET_CHUNK, tokens_per_step=512):
    B, S, P = proj.shape
    H = RET_HEADS
    dk = P // (6 * H)
    dv = 2 * dk
    tc = min(tokens_per_step, S)
    assert S % tc == 0 and tc % chunk == 0
    half = dk // 2
    pos = jnp.arange(S, dtype=F32)
    freqs = ROPE_BASE ** (-jnp.arange(half, dtype=F32) / half)
    ang = pos[:, None] * freqs[None, :]
    cos, sin = jnp.cos(ang), jnp.sin(ang)
    log_gamma = jnp.log(1.0 - 2.0 ** (-5.0 - jnp.arange(H, dtype=F32)))
    idx = jnp.arange(chunk, dtype=F32)
    diff = idx[:, None] - idx[None, :]
    dmask = jnp.where(diff >= 0, jnp.exp(jnp.maximum(diff, 0.0)[None] * log_gamma[:, None, None]), 0.0)
    qdec = jnp.exp((idx[None, :] + 1.0) * log_gamma[:, None])[:, :, None]
    kdec = jnp.exp((chunk - 1.0 - idx[None, :]) * log_gamma[:, None])[:, :, None]
    kern = functools.partial(_retention_kernel, chunk=chunk, n_chunks=tc // chunk, k_scale=dk ** -0.5)
    k_off, v_off, g_off = (H * dk) // dk, (2 * H * dk) // dv, (2 * H * dk + H * dv) // dv
    return pl.pallas_call(
        kern,
        grid=(B, H, S // tc),
        in_specs=[
            pl.BlockSpec((1, tc, dk), lambda b, h, c: (b, c, h)),
            pl.BlockSpec((1, tc, dk), lambda b, h, c: (b, c, k_off + h)),
            pl.BlockSpec((1, tc, dv), lambda b, h, c: (b, c, v_off + h)),
            pl.BlockSpec((1, tc, dv), lambda b, h, c: (b, c, g_off + h)),
            pl.BlockSpec((tc, half), lambda b, h, c: (c, 0)),
            pl.BlockSpec((tc, half), lambda b, h, c: (c, 0)),
            pl.BlockSpec((1, chunk, chunk), lambda b, h, c: (h, 0, 0)),
            pl.BlockSpec((1, chunk, 1), lambda b, h, c: (h, 0, 0)),
            pl.BlockSpec((1, chunk, 1), lambda b, h, c: (h, 0, 0)),
            pl.BlockSpec((1, dv), lambda b, h, c: (0, h)),
        ],
        out_specs=pl.BlockSpec((1, tc, dv), lambda b, h, c: (b, c, h)),
        out_shape=jax.ShapeDtypeStruct((B, S, H * dv), MXU_DTYPE),
        scratch_shapes=[pltpu.VMEM((dk, dv), F32)],
        compiler_params=_cparams("arbitrary", "arbitrary", "arbitrary"),
        name="retention_core",
    )(proj, proj, proj, proj, cos, sin, dmask, qdec, kdec, gn_gain.reshape(1, H * dv))


def _conv_ffn_kernel(x_ref, g_ref, wa_ref, wb_ref, cwa_ref, cwb_ref, cba_ref, cbb_ref, wo_ref, fg_ref,
                     o_ref, xn_ref, acc_ref, carry_a_ref, carry_b_ref, *, tiles_per_seq, final_norm):
    i, j = pl.program_id(0), pl.program_id(1)
    tm = x_ref.shape[0]

    @pl.when((i == 0) & (j == 0))
    def _():
        carry_a_ref[...] = jnp.zeros_like(carry_a_ref)
        carry_b_ref[...] = jnp.zeros_like(carry_b_ref)

    @pl.when(j == 0)
    def _():
        xn_ref[...] = _rms_normalize(x_ref[...], g_ref[...]).astype(xn_ref.dtype)
        acc_ref[...] = jnp.zeros_like(acc_ref)

    seq_start = (i % tiles_per_seq) == 0

    def causal_conv(w_ref, cw_ref, cb_ref, carry_ref):
        u = jnp.dot(xn_ref[...], w_ref[...], preferred_element_type=F32)
        prev = jnp.where(seq_start, 0.0, carry_ref[j])
        carry_ref[j] = u[tm - 8:, :]
        row = lax.broadcasted_iota(jnp.int32, u.shape, 0)
        u1 = jnp.where(row == 0, prev[7:8, :], pltpu.roll(u, 1, 0))
        u2 = jnp.where(row == 0, prev[6:7, :], jnp.where(row == 1, prev[7:8, :], pltpu.roll(u, 2, 0)))
        cw = cw_ref[...]
        return ((cb_ref[...] + u2 * cw[0:1, :]) + u1 * cw[1:2, :]) + u * cw[2:3, :]

    a = causal_conv(wa_ref, cwa_ref, cba_ref, carry_a_ref)
    b = causal_conv(wb_ref, cwb_ref, cbb_ref, carry_b_ref)
    act = ((a * _sigmoid(a)) * b).astype(wo_ref.dtype)
    acc_ref[...] += jnp.dot(act, wo_ref[...], preferred_element_type=F32)

    @pl.when(j == pl.num_programs(1) - 1)
    def _():
        out = x_ref[...] + acc_ref[...]
        if final_norm:
            out = _rms_normalize(out, fg_ref[...])
        o_ref[...] = out


def conv_ffn_block(h, seq_len, norm_gain, w_in, conv_w, conv_b, w_out, final_gain, *, final_norm, tm=512, tn=512):
    T, D = h.shape
    F = w_out.shape[0]
    tm, tn = min(tm, seq_len), min(tn, F)
    assert seq_len % tm == 0 and F % tn == 0 and tm >= 8
    nj = F // tn
    kern = functools.partial(_conv_ffn_kernel, tiles_per_seq=seq_len // tm, final_norm=final_norm)
    return pl.pallas_call(
        kern,
        grid=(T // tm, nj),
        in_specs=[
            pl.BlockSpec((tm, D), lambda i, j: (i, 0)),
            pl.BlockSpec((1, D), lambda i, j: (0, 0)),
            pl.BlockSpec((D, tn), lambda i, j: (0, j)),
            pl.BlockSpec((D, tn), lambda i, j: (0, nj + j)),
            pl.BlockSpec((CONV_W, tn), lambda i, j: (0, j)),
            pl.BlockSpec((CONV_W, tn), lambda i, j: (0, nj + j)),
            pl.BlockSpec((1, tn), lambda i, j: (0, j)),
            pl.BlockSpec((1, tn), lambda i, j: (0, nj + j)),
            pl.BlockSpec((tn, D), lambda i, j: (j, 0)),
            pl.BlockSpec((1, D), lambda i, j: (0, 0)),
        ],
        out_specs=pl.BlockSpec((tm, D), lambda i, j: (i, 0)),
        out_shape=jax.ShapeDtypeStruct((T, D), F32),
        scratch_shapes=[
            pltpu.VMEM((tm, D), w_in.dtype),
            pltpu.VMEM((tm, D), F32),
            pltpu.VMEM((nj, 8, tn), F32),
            pltpu.VMEM((nj, 8, tn), F32),
        ],
        compiler_params=_cparams("arbitrary", "arbitrary"),
        name="conv_ffn",
    )(h, norm_gain.reshape(1, D), w_in, w_in, conv_w, conv_w, conv_b.reshape(1, 2 * F), conv_b.reshape(1, 2 * F),
      w_out, final_gain.reshape(1, D))


def _kv_proj_kernel(x_ref, g_ref, w_ref, cmp_ref, kslc_ref, kwin_ref, vt_ref, *, seq_len):
    tm = x_ref.shape[0]
    d = NSA_HEAD_DIM
    Hkv = NSA_KV_HEADS
    xn = _rms_normalize(x_ref[...], g_ref[...]).astype(w_ref.dtype)
    res = jnp.dot(xn, w_ref[...], preferred_element_type=F32)
    col = lambda branch, hh: res[:, (branch * Hkv + hh) * d:(branch * Hkv + hh + 1) * d]
    pos = (pl.program_id(0) * tm) % seq_len + lax.broadcasted_iota(jnp.int32, (tm, LANE), 0)
    blk = (pos >> (SLC_BLOCK.bit_length() - 1)) & (LANE - 1)
    onehot = jnp.where(blk == lax.broadcasted_iota(jnp.int32, (tm, LANE), 1), 1.0, 0.0).astype(kslc_ref.dtype)
    ones = jnp.ones((BF16_SUBLANES, tm), vt_ref.dtype)
    for hh in range(Hkv):
        cmp_ref[hh] = col(0, hh)
        cmp_ref[Hkv + hh] = col(1, hh)
        kslc_ref[hh, :, :d] = col(2, hh).astype(kslc_ref.dtype)
        kslc_ref[hh, :, d:] = onehot
        kwin_ref[hh] = col(4, hh).astype(kwin_ref.dtype)
        vt_ref[hh, :d, :] = col(3, hh).T.astype(vt_ref.dtype)
        vt_ref[hh, d:, :] = ones
        vt_ref[Hkv + hh, :d, :] = col(5, hh).T.astype(vt_ref.dtype)
        vt_ref[Hkv + hh, d:, :] = ones


def kv_projection(h, seq_len, gain, w_kv, *, tm=256):
    T, D = h.shape
    Hkv, d = NSA_KV_HEADS, NSA_HEAD_DIM
    N = w_kv.shape[1]
    tm = min(tm, seq_len)
    assert seq_len % tm == 0 and N == 2 * N_BRANCH * Hkv * d and tm % LANE == 0
    row3 = lambda i: (0, i, 0)
    return pl.pallas_call(
        functools.partial(_kv_proj_kernel, seq_len=seq_len),
        grid=(T // tm,),
        in_specs=[
            pl.BlockSpec((tm, D), lambda i: (i, 0)),
            pl.BlockSpec((1, D), lambda i: (0, 0)),
            pl.BlockSpec((D, N), lambda i: (0, 0)),
        ],
        out_specs=[
            pl.BlockSpec((2 * Hkv, tm, d), row3),
            pl.BlockSpec((Hkv, tm, d + LANE), row3),
            pl.BlockSpec((Hkv, tm, d), row3),
            pl.BlockSpec((2 * Hkv, d + BF16_SUBLANES, tm), lambda i: (0, 0, i)),
        ],
        out_shape=[
            jax.ShapeDtypeStruct((2 * Hkv, T, d), F32),
            jax.ShapeDtypeStruct((Hkv, T, d + LANE), MXU_DTYPE),
            jax.ShapeDtypeStruct((Hkv, T, d), MXU_DTYPE),
            jax.ShapeDtypeStruct((2 * Hkv, d + BF16_SUBLANES, T), MXU_DTYPE),
        ],
        compiler_params=_cparams("arbitrary"),
        name="kv_projection",
    )(h, gain.reshape(1, D), w_kv)


def _q_proj_kernel(x_ref, g_ref, w_ref, qt_ref, gate_ref, *, q_scale):
    d = NSA_HEAD_DIM
    xn = _rms_normalize(x_ref[...], g_ref[...]).astype(w_ref.dtype)
    res = jnp.dot(xn, w_ref[...], preferred_element_type=F32)
    for hq in range(NSA_Q_HEADS):
        qt_ref[hq] = (res[:, hq * d:(hq + 1) * d] * q_scale).T.astype(qt_ref.dtype)
    gate_ref[...] = res[:, NSA_Q_HEADS * d:]


def q_projection(h, gain, w_q, *, tm=256):
    T, D = h.shape
    d = NSA_HEAD_DIM
    N = w_q.shape[1]
    tm = min(tm, T)
    assert T % tm == 0 and N == NSA_Q_HEADS * d + LANE and tm % LANE == 0
    return pl.pallas_call(
        functools.partial(_q_proj_kernel, q_scale=d ** -0.5),
        grid=(T // tm,),
        in_specs=[
            pl.BlockSpec((tm, D), lambda i: (i, 0)),
            pl.BlockSpec((1, D), lambda i: (0, 0)),
            pl.BlockSpec((D, N), lambda i: (0, 0)),
        ],
        out_specs=[
            pl.BlockSpec((NSA_Q_HEADS, d, tm), lambda i: (0, 0, i)),
            pl.BlockSpec((tm, LANE), lambda i: (i, 0)),
        ],
        out_shape=[
            jax.ShapeDtypeStruct((NSA_Q_HEADS, d, T), MXU_DTYPE),
            jax.ShapeDtypeStruct((T, LANE), F32),
        ],
        compiler_params=_cparams("arbitrary"),
        name="q_projection",
    )(h, gain.reshape(1, D), w_q)


def _compress_kernel(x_ref, pe_ref, w1_ref, w2_ref, o_ref):
    d = x_ref.shape[2]
    n = o_ref.shape[2]
    first = jnp.zeros((n, w1_ref.shape[2]), F32)
    second = jnp.zeros((n, w1_ref.shape[2]), F32)
    for p in range(CMP_STRIDE):
        xp = x_ref[0, pl.ds(p, n, stride=CMP_STRIDE), :]
        q = CMP_STRIDE + p
        first = first + jnp.dot((xp + pe_ref[0, p:p + 1, :]).astype(MXU_DTYPE), w1_ref[0, p * d:(p + 1) * d, :],
                                preferred_element_type=F32)
        second = second + jnp.dot((xp + pe_ref[0, q:q + 1, :]).astype(MXU_DTYPE), w1_ref[0, q * d:(q + 1) * d, :],
                                  preferred_element_type=F32)
    hid = first + pltpu.roll(second, n - 1, 0)
    gelu = 0.5 * hid * (1.0 + jnp.tanh(0.7978845608028654 * (hid + 0.044715 * (hid * hid * hid))))
    out = jnp.dot(gelu.astype(MXU_DTYPE), w2_ref[0], preferred_element_type=F32)
    row = lax.broadcasted_iota(jnp.int32, out.shape, 0)
    o_ref[0, 0] = jnp.where(row < n - 1, out, 0.0)


def compress_tokens(x, batch, pe, w1, w2):
    H2, T, d = x.shape
    Hkv = H2 // 2
    S = T // batch
    n = S // CMP_STRIDE
    hid = w2.shape[1]
    return pl.pallas_call(
        _compress_kernel,
        grid=(2, batch, Hkv),
        in_specs=[
            pl.BlockSpec((1, S, d), lambda s, b, hh: (s * Hkv + hh, b, 0)),
            pl.BlockSpec((1, CMP_BLOCK, d), lambda s, b, hh: (s, 0, 0)),
            pl.BlockSpec((1, CMP_BLOCK * d, hid), lambda s, b, hh: (s, 0, 0)),
            pl.BlockSpec((1, hid, d), lambda s, b, hh: (s, 0, 0)),
        ],
        out_specs=pl.BlockSpec((1, 1, n, d), lambda s, b, hh: (s, b * Hkv + hh, 0, 0)),
        out_shape=jax.ShapeDtypeStruct((2, batch * Hkv, n, d), F32),
        compiler_params=_cparams("arbitrary", "arbitrary", "arbitrary"),
        name="compress_tokens",
    )(x, pe, w1, w2)


def _nsa_cmp_kernel(q_ref, kc_ref, vct_ref, o_ref, sel_ref, qt_ref, *, top_k):
    i = pl.program_id(2)
    tq = q_ref.shape[2]
    d = NSA_HEAD_DIM
    G = NSA_GROUP
    ncp = kc_ref.shape[2]
    nsel = ncp // SLC_RATIO
    t0 = i * tq
    for g in range(G):
        qt_ref[:, g * tq:(g + 1) * tq] = q_ref[g]
    row = lax.broadcasted_iota(jnp.int32, (ncp, tq), 0)
    tok = t0 + lax.broadcasted_iota(jnp.int32, (ncp, tq), 1)
    cblk = SLC_RATIO * (row & (nsel - 1)) + (row >> (nsel.bit_length() - 1))
    bias = jnp.where(cblk * CMP_STRIDE + (CMP_BLOCK - 1) <= tok, 0.0, NEG)
    s = jnp.dot(kc_ref[0, 0], qt_ref[...], preferred_element_type=F32) + jnp.concatenate([bias] * G, axis=1)
    e = jnp.exp(s - jnp.max(s, axis=0, keepdims=True))
    denom = jnp.sum(e, axis=0, keepdims=True)
    any_valid = jnp.concatenate([tok[0:1, :] >= CMP_BLOCK - 1] * G, axis=1)
    p = e * jnp.where(any_valid, 1.0 / denom, 0.0)
    out_t = jnp.dot(vct_ref[0, 0], p.astype(MXU_DTYPE), preferred_element_type=F32)
    imp = None
    for g in range(G):
        o_ref[0, :, g * d:(g + 1) * d] = out_t[:, g * tq:(g + 1) * tq].T
        pg = p[:, g * tq:(g + 1) * tq]
        imp = pg if imp is None else imp + pg
    ph = [imp[r * nsel:(r + 1) * nsel, :] for r in range(SLC_RATIO)]
    j = lax.broadcasted_iota(jnp.int32, (nsel, tq), 0)
    p_slc = jnp.where(j == 0, 0.0, pltpu.roll(ph[SLC_RATIO - 1], 1, 0))
    for r in range(SLC_RATIO):
        p_slc = p_slc + ph[r]
    cur = (t0 + lax.broadcasted_iota(jnp.int32, (nsel, tq), 1)) >> (SLC_BLOCK.bit_length() - 1)
    causal = j <= cur

    @pl.when(t0 < top_k * SLC_BLOCK)
    def _():
        sel_ref[0, 0] = jnp.where(causal, 0.0, MASK_BIAS).astype(sel_ref.dtype)

    @pl.when(t0 >= top_k * SLC_BLOCK)
    def _():
        forced = (j == 0) | (j == cur) | (j == cur - 1)
        score = jnp.where(causal & jnp.logical_not(forced), p_slc, PICKED)
        jf = j.astype(F32)
        for _ in range(top_k - 3):
            m = jnp.max(score, axis=0, keepdims=True)
            first = jnp.min(jnp.where(score == m, jf, float(nsel)), axis=0, keepdims=True)
            score = jnp.where(jf == first, PICKED, score)
        sel_ref[0, 0] = jnp.where((score == PICKED) & causal, 0.0, MASK_BIAS).astype(sel_ref.dtype)


def nsa_cmp_branch(qt, B, kc, vct, *, tq=128):
    S = qt.shape[2] // B
    Hkv, G, d = NSA_KV_HEADS, NSA_GROUP, NSA_HEAD_DIM
    ncp = kc.shape[2]
    nsel = ncp // SLC_RATIO
    top_k = min(SLC_TOPK, nsel)
    tq = min(tq, S)
    assert S % tq == 0 and nsel & (nsel - 1) == 0 and tq & (tq - 1) == 0
    assert (top_k * SLC_BLOCK) % tq == 0 and top_k >= 3
    kern = functools.partial(_nsa_cmp_kernel, top_k=top_k)
    return pl.pallas_call(
        kern,
        grid=(B, Hkv, S // tq),
        in_specs=[
            pl.BlockSpec((G, d, tq), lambda b, h, i: (h, 0, b * (S // tq) + i)),
            pl.BlockSpec((1, 1, ncp, d), lambda b, h, i: (b, h, 0, 0)),
            pl.BlockSpec((1, 1, d, ncp), lambda b, h, i: (b, h, 0, 0)),
        ],
        out_specs=[
            pl.BlockSpec((1, tq, G * d), lambda b, h, i: (b, i, h)),
            pl.BlockSpec((1, 1, nsel, tq), lambda b, h, i: (b, h, 0, i)),
        ],
        out_shape=[
            jax.ShapeDtypeStruct((B, S, Hkv * G * d), F32),
            jax.ShapeDtypeStruct((B, Hkv, nsel, S), MXU_DTYPE),
        ],
        scratch_shapes=[pltpu.VMEM((d, G * tq), MXU_DTYPE)],
        compiler_params=_cparams("arbitrary", "arbitrary", "arbitrary"),
        name="nsa_cmp",
    )(qt, kc, vct)


def _nsa_slc_kernel(q_ref, sel_ref, k_ref, vt_ref, o_ref, qa_ref, m_ref, acc_ref, s_ref, *, tk):
    i = pl.program_id(2)
    tq = q_ref.shape[2]
    d = NSA_HEAD_DIM
    G = NSA_GROUP
    nblk = sel_ref.shape[2]
    span = LANE * SLC_BLOCK
    n_halves = pl.cdiv(nblk, LANE)

    m_ref[...] = jnp.full_like(m_ref, PICKED)
    acc_ref[...] = jnp.zeros_like(acc_ref)
    for g in range(G):
        qa_ref[:d, g * tq:(g + 1) * tq] = q_ref[g]

    def scores(kt_idx):
        k0 = pl.multiple_of(kt_idx * tk, tk)
        return jnp.dot(k_ref[0, pl.ds(k0, tk), :], qa_ref[...], preferred_element_type=F32)

    def accumulate(kt_idx, s):
        k0 = pl.multiple_of(kt_idx * tk, tk)
        m_old = m_ref[...]
        m_new = jnp.maximum(m_old, jnp.max(s, axis=0, keepdims=True))
        alpha = jnp.exp(m_old - m_new)
        p = jnp.exp(s - m_new)
        acc_ref[...] = alpha * acc_ref[...] + jnp.dot(vt_ref[0, :, pl.ds(k0, tk)], p.astype(MXU_DTYPE),
                                                      preferred_element_type=F32)
        m_ref[...] = m_new

    last = ((i + 1) * tq - 1) // tk
    tiles_per_half = span // tk
    for hf in range(n_halves):
        lo = hf * tiles_per_half
        hi = jnp.minimum(last, lo + tiles_per_half)

        @pl.when(lo <= last)
        def _():
            sel_half = sel_ref[0, 0, hf * LANE:(hf + 1) * LANE, :]
            for g in range(G):
                qa_ref[d:, g * tq:(g + 1) * tq] = sel_half
            n = hi - lo
            odd = (n & 1) == 1
            diag_here = last < lo + tiles_per_half

            def accumulate_diagonal(s):
                col = lax.broadcasted_iota(jnp.int32, s.shape, 1)
                tok = i * tq + (col & (tq - 1))
                key = last * tk + lax.broadcasted_iota(jnp.int32, s.shape, 0)
                accumulate(last, jnp.where(key <= tok, s, NEG))

            s_ref[0] = scores(lo)

            def pair(pi, carry):
                t = lo + 2 * pi
                s_ref[1] = scores(t + 1)
                accumulate(t, s_ref[0])
                s_ref[0] = scores(t + 2)
                accumulate(t + 1, s_ref[1])
                return carry

            lax.fori_loop(0, n >> 1, pair, 0)

            @pl.when(odd & diag_here)
            def _():
                s_ref[1] = scores(last)
                accumulate(hi - 1, s_ref[0])
                accumulate_diagonal(s_ref[1])

            @pl.when(odd & jnp.logical_not(diag_here))
            def _():
                accumulate(hi - 1, s_ref[0])

            @pl.when(jnp.logical_not(odd) & diag_here)
            def _():
                accumulate_diagonal(s_ref[0])

    out = acc_ref[:d, :] * (1.0 / acc_ref[d:d + 1, :])
    for g in range(G):
        o_ref[0, :, g * d:(g + 1) * d] = out[:, g * tq:(g + 1) * tq].T


def nsa_slc_branch(qt, B, sel, k_aug, vt, *, tq=256, tk=1024):
    S = qt.shape[2] // B
    Hkv, G, d = NSA_KV_HEADS, NSA_GROUP, NSA_HEAD_DIM
    nblk = sel.shape[2]
    dv = vt.shape[1]
    tq, tk = min(tq, S), min(tk, S)
    assert S % tq == 0 and S % tk == 0 and tk % tq == 0 and tq & (tq - 1) == 0
    assert nblk % LANE == 0 and (LANE * SLC_BLOCK) % tk == 0
    kern = functools.partial(_nsa_slc_kernel, tk=tk)
    return pl.pallas_call(
        kern,
        grid=(B, Hkv, S // tq),
        in_specs=[
            pl.BlockSpec((G, d, tq), lambda b, h, i: (h, 0, b * (S // tq) + i)),
            pl.BlockSpec((1, 1, nblk, tq), lambda b, h, i: (b, h, 0, i)),
            pl.BlockSpec((1, S, d + LANE), lambda b, h, i: (h, b, 0)),
            pl.BlockSpec((1, dv, S), lambda b, h, i: (h, 0, b)),
        ],
        out_specs=pl.BlockSpec((1, tq, G * d), lambda b, h, i: (b, i, h)),
        out_shape=jax.ShapeDtypeStruct((B, S, Hkv * G * d), F32),
        scratch_shapes=[
            pltpu.VMEM((d + LANE, G * tq), MXU_DTYPE),
            pltpu.VMEM((1, G * tq), F32),
            pltpu.VMEM((dv, G * tq), F32),
            pltpu.VMEM((2, tk, G * tq), F32),
        ],
        compiler_params=_cparams("arbitrary", "arbitrary", "arbitrary"),
        name="nsa_slc",
    )(qt, sel, k_aug, vt)


def _nsa_win_kernel(q_ref, k_ref, vt_ref, bias_ref, o_ref, qt_ref):
    i = pl.program_id(2)
    tq = q_ref.shape[2]
    d = NSA_HEAD_DIM
    G = NSA_GROUP
    nk = WIN + tq
    k0 = pl.multiple_of(jnp.maximum(i * tq - WIN, 0), tq)
    for g in range(G):
        qt_ref[:, g * tq:(g + 1) * tq] = q_ref[g]
    s = (jnp.dot(k_ref[0, pl.ds(k0, nk), :], qt_ref[...], preferred_element_type=F32)
         + jnp.concatenate([bias_ref[0]] * G, axis=1))
    e = jnp.exp(s - jnp.max(s, axis=0, keepdims=True))
    p = e * (1.0 / jnp.sum(e, axis=0, keepdims=True))
    out_t = jnp.dot(vt_ref[0, :d, pl.ds(k0, nk)], p.astype(MXU_DTYPE), preferred_element_type=F32)
    for g in range(G):
        o_ref[0, :, g * d:(g + 1) * d] = out_t[:, g * tq:(g + 1) * tq].T


def nsa_win_branch(qt, B, k_win, vt, *, tq=256):
    S = qt.shape[2] // B
    Hkv, G, d = NSA_KV_HEADS, NSA_GROUP, NSA_HEAD_DIM
    dv = vt.shape[1]
    tq = min(tq, S)
    assert S % tq == 0 and tq % LANE == 0 and tq & (tq - 1) == 0 and WIN % tq == 0 and S >= WIN + tq
    kern = _nsa_win_kernel
    n_clipped = WIN // tq
    t0 = jnp.minimum(jnp.arange(n_clipped + 1) * tq, WIN)[:, None, None]
    pos = (t0 - jnp.minimum(t0, WIN)) + jnp.arange(WIN + tq)[None, :, None]
    tok = t0 + jnp.arange(tq)[None, None, :]
    bias = jnp.where((pos <= tok) & (pos > tok - WIN), 0.0, NEG).astype(F32)
    return pl.pallas_call(
        kern,
        grid=(B, Hkv, S // tq),
        in_specs=[
            pl.BlockSpec((G, d, tq), lambda b, h, i: (h, 0, b * (S // tq) + i)),
            pl.BlockSpec((1, S, d), lambda b, h, i: (h, b, 0)),
            pl.BlockSpec((1, dv, S), lambda b, h, i: (Hkv + h, 0, b)),
            pl.BlockSpec((1, WIN + tq, tq), lambda b, h, i: (jnp.minimum(i, n_clipped), 0, 0)),
        ],
        out_specs=pl.BlockSpec((1, tq, G * d), lambda b, h, i: (b, i, h)),
        out_shape=jax.ShapeDtypeStruct((B, S, Hkv * G * d), F32),
        scratch_shapes=[pltpu.VMEM((d, G * tq), MXU_DTYPE)],
        compiler_params=_cparams("arbitrary", "arbitrary", "arbitrary"),
        name="nsa_win",
    )(qt, k_win, vt, bias)


def _nsa_out_kernel(oc_ref, os_ref, ow_ref, gate_ref, w_ref, h_ref, o_ref, merged_ref):
    d = NSA_HEAD_DIM
    gates = _sigmoid(gate_ref[...])
    for hq in range(NSA_Q_HEADS):
        cols = slice(hq * d, (hq + 1) * d)
        c = N_BRANCH * hq
        merged = (gates[:, c:c + 1] * oc_ref[:, cols] + gates[:, c + 1:c + 2] * os_ref[:, cols]
                  + gates[:, c + 2:c + 3] * ow_ref[:, cols])
        merged_ref[:, cols] = merged.astype(merged_ref.dtype)
    o_ref[...] = h_ref[...] + jnp.dot(merged_ref[...], w_ref[...], preferred_element_type=F32)


def nsa_out(o_cmp, o_slc, o_win, gate_logits, w_o, h, *, tm=256):
    T, HD = o_cmp.shape
    D = w_o.shape[1]
    tm = min(tm, T)
    assert T % tm == 0 and gate_logits.shape == (T, LANE)
    row = lambda i: (i, 0)
    return pl.pallas_call(
        _nsa_out_kernel,
        grid=(T // tm,),
        in_specs=[
            pl.BlockSpec((tm, HD), row),
            pl.BlockSpec((tm, HD), row),
            pl.BlockSpec((tm, HD), row),
            pl.BlockSpec((tm, LANE), row),
            pl.BlockSpec((HD, D), lambda i: (0, 0)),
            pl.BlockSpec((tm, D), row),
        ],
        out_specs=pl.BlockSpec((tm, D), row),
        out_shape=jax.ShapeDtypeStruct((T, D), F32),
        scratch_shapes=[pltpu.VMEM((tm, HD), w_o.dtype)],
        compiler_params=_cparams("arbitrary"),
        name="nsa_out",
    )(o_cmp, o_slc, o_win, gate_logits, w_o, h)


def _phase_major(x, nsel):
    B, n, H, d = x.shape
    return x.reshape(B, nsel, SLC_RATIO, H, d).transpose(0, 3, 2, 1, 4).reshape(B, H, n, d)


def _cast(w):
    return w.astype(MXU_DTYPE)


def nsa_shared_kv(h, B, S, kv_gain, w_kv, cmp_pe_k, cmp_w1_k, cmp_w2_k, cmp_pe_v, cmp_w1_v, cmp_w2_v):
    Hkv, d = NSA_KV_HEADS, NSA_HEAD_DIM
    cmp_in, k_slc, k_win, vt = kv_projection(h, S, kv_gain, _cast(w_kv))
    n = S // CMP_STRIDE
    nsel = S // SLC_BLOCK
    cmp = compress_tokens(cmp_in, B, jnp.stack([cmp_pe_k, cmp_pe_v]), _cast(jnp.stack([cmp_w1_k, cmp_w1_v])),
                          _cast(jnp.stack([cmp_w2_k, cmp_w2_v])))
    cmp = cmp.reshape(2, B, Hkv, n, d).transpose(0, 1, 3, 2, 4)
    kc = _cast(_phase_major(cmp[0], nsel))
    vct = _cast(_phase_major(cmp[1], nsel).transpose(0, 1, 3, 2))
    return kc, vct, k_slc, k_win, vt


def nsa_attention_block(h, B, S, mix_gain, w_q, w_o, shared):
    kc, vct, k_slc, k_win, vt = shared
    T = B * S
    nq = NSA_Q_HEADS * NSA_HEAD_DIM
    nqp = nq + LANE
    w_q = jnp.pad(w_q, ((0, 0), (0, nqp - w_q.shape[1])))
    qt, gate_logits = q_projection(h, mix_gain, _cast(w_q))
    o_cmp, sel = nsa_cmp_branch(qt, B, kc, vct)
    o_slc = nsa_slc_branch(qt, B, sel, k_slc, vt)
    o_win = nsa_win_branch(qt, B, k_win, vt)
    return nsa_out(o_cmp.reshape(T, nq), o_slc.reshape(T, nq), o_win.reshape(T, nq), gate_logits, _cast(w_o), h)


def nsa_mixer(h, B, S, mix_gain, kv_gain, w_kv, cmp_pe_k, cmp_w1_k, cmp_w2_k, cmp_pe_v, cmp_w1_v, cmp_w2_v, w_q, w_o):
    shared = nsa_shared_kv(h, B, S, kv_gain, w_kv, cmp_pe_k, cmp_w1_k, cmp_w2_k, cmp_pe_v, cmp_w1_v, cmp_w2_v)
    return nsa_attention_block(h, B, S, mix_gain, w_q, w_o, shared)


def kernel(x, norm_mix_gain, norm_ffn_gain, ret_w_in, ret_gn_gain, ret_w_out, nsa_kv_norm_gain, nsa_w_kv, cmp_pe_k, cmp_w1_k, cmp_w2_k, cmp_pe_v, cmp_w1_v, cmp_w2_v, nsa_w_q, nsa_w_o, ffn_w_in, ffn_conv_w, ffn_conv_b, ffn_w_out, final_norm_gain):
    B, S, D = x.shape
    T = B * S
    depth = norm_mix_gain.shape[0]
    n_a = depth // 2
    h = x.reshape(T, D)
    shared = None
    for layer in range(depth):
        if layer < n_a:
            proj = norm_matmul(h, norm_mix_gain[layer], _cast(ret_w_in[layer]), tm=1024, tn=1024)
            y = retention_core(proj.reshape(B, S, -1), ret_gn_gain[layer])
            h = matmul_residual(y.reshape(T, -1), _cast(ret_w_out[layer]), h, tm=512, tn=1024)
        else:
            if layer == n_a:
                shared = nsa_shared_kv(h, B, S, nsa_kv_norm_gain, nsa_w_kv, cmp_pe_k, cmp_w1_k, cmp_w2_k,
                                       cmp_pe_v, cmp_w1_v, cmp_w2_v)
            b = layer - n_a
            h = nsa_attention_block(h, B, S, norm_mix_gain[layer], nsa_w_q[b], nsa_w_o[b], shared)
        h = conv_ffn_block(h, S, norm_ffn_gain[layer], _cast(ffn_w_in[layer]), ffn_conv_w[layer], ffn_conv_b[layer],
                           _cast(ffn_w_out[layer]), final_norm_gain, final_norm=(layer == depth - 1))
    return h.reshape(B, S, D)
```

```python
import functools

import jax
import jax.numpy as jnp
from jax import lax
from jax.experimental import pallas as pl
from jax.experimental.pallas import tpu as pltpu

F32 = jnp.float32
MXU_DTYPE = jnp.bfloat16

RMS_EPS = 1e-6
ROPE_BASE = 10000.0

RET_HEADS = 8
RET_CHUNK = 128

NSA_Q_HEADS = 16
NSA_KV_HEADS = 4
NSA_GROUP = NSA_Q_HEADS // NSA_KV_HEADS
NSA_HEAD_DIM = 128
N_BRANCH = 3
CMP_BLOCK = 32
CMP_STRIDE = 16
CMP_RATIO = CMP_BLOCK // CMP_STRIDE
SLC_BLOCK = 64
SLC_TOPK = 16
SLC_RATIO = SLC_BLOCK // CMP_STRIDE
CMP_SPANS = 4
WIN = 512
CONV_W = 3

NEG = -1e30
MASK_BIAS = -(2.0 ** 100)
PICKED = -3.0e38
SEL_FORCE = 1e30

LANE = 128
BF16_SUBLANES = 16
VMEM_LIMIT_BYTES = 56 * 1024 * 1024


def _cparams(*sem):
    return pltpu.CompilerParams(dimension_semantics=sem, vmem_limit_bytes=VMEM_LIMIT_BYTES)


def _sigmoid(x):
    return 1.0 / (1.0 + jnp.exp(-x))


def _rms_normalize(x, gain):
    ms = jnp.mean(x * x, axis=-1, keepdims=True)
    return x * lax.rsqrt(ms + RMS_EPS) * gain


def _norm_matmul_kernel(x_ref, g_ref, w_ref, o_ref, xn_ref):
    @pl.when(pl.program_id(1) == 0)
    def _():
        xn_ref[...] = _rms_normalize(x_ref[...], g_ref[...]).astype(xn_ref.dtype)

    o_ref[...] = jnp.dot(xn_ref[...], w_ref[...], preferred_element_type=F32).astype(o_ref.dtype)


def norm_matmul(h, gain, w, *, tm, tn, out_dtype=F32):
    T, D = h.shape
    N = w.shape[1]
    tm, tn = min(tm, T), min(tn, N)
    assert T % tm == 0 and N % tn == 0
    return pl.pallas_call(
        _norm_matmul_kernel,
        grid=(T // tm, N // tn),
        in_specs=[
            pl.BlockSpec((tm, D), lambda i, j: (i, 0)),
            pl.BlockSpec((1, D), lambda i, j: (0, 0)),
            pl.BlockSpec((D, tn), lambda i, j: (0, j)),
        ],
        out_specs=pl.BlockSpec((tm, tn), lambda i, j: (i, j)),
        out_shape=jax.ShapeDtypeStruct((T, N), out_dtype),
        scratch_shapes=[pltpu.VMEM((tm, D), w.dtype)],
        compiler_params=_cparams("arbitrary", "arbitrary"),
        name="norm_matmul",
    )(h, gain.reshape(1, D), w)


def _matmul_residual_kernel(y_ref, w_ref, h_ref, o_ref):
    o_ref[...] = h_ref[...] + jnp.dot(y_ref[...], w_ref[...], preferred_element_type=F32)


def matmul_residual(y, w, h, *, tm, tn):
    T, K = y.shape
    N = w.shape[1]
    tm, tn = min(tm, T), min(tn, N)
    assert T % tm == 0 and N % tn == 0
    return pl.pallas_call(
        _matmul_residual_kernel,
        grid=(T // tm, N // tn),
        in_specs=[
            pl.BlockSpec((tm, K), lambda i, j: (i, 0)),
            pl.BlockSpec((K, tn), lambda i, j: (0, j)),
            pl.BlockSpec((tm, tn), lambda i, j: (i, j)),
        ],
        out_specs=pl.BlockSpec((tm, tn), lambda i, j: (i, j)),
        out_shape=jax.ShapeDtypeStruct((T, N), F32),
        compiler_params=_cparams("arbitrary", "arbitrary"),
        name="matmul_residual",
    )(y, w, h)


def _retention_kernel(q_ref, k_ref, v_ref, g_ref, cos_ref, sin_ref, dmask_ref, qdec_ref, kdec_ref, gn_ref,
                      o_ref, state_ref, *, chunk, n_chunks, k_scale):
    @pl.when(pl.program_id(2) == 0)
    def _():
        state_ref[...] = jnp.zeros_like(state_ref)

    half = q_ref.shape[-1] // 2
    dmask = dmask_ref[0]
    qdec = qdec_ref[0]
    kdec = kdec_ref[0]
    chunk_decay = qdec[chunk - 1:chunk, :]
    gn = gn_ref[...]

    def rotate(x, cos, sin):
        x1, x2 = x[:, :half], x[:, half:]
        return jnp.concatenate([x1 * cos - x2 * sin, x1 * sin + x2 * cos], axis=-1)

    for ci in range(n_chunks):
        rows = pl.ds(ci * chunk, chunk)
        cos, sin = cos_ref[rows, :], sin_ref[rows, :]
        q = rotate(q_ref[0, rows, :], cos, sin)
        k = rotate(k_ref[0, rows, :], cos, sin) * k_scale
        v = v_ref[0, rows, :].astype(MXU_DTYPE)
        qm = q.astype(MXU_DTYPE)
        scores = lax.dot_general(qm, k.astype(MXU_DTYPE), (((1,), (1,)), ((), ())),
                                 preferred_element_type=F32) * dmask
        inner = jnp.dot(scores.astype(MXU_DTYPE), v, preferred_element_type=F32)
        state = state_ref[...]
        cross = jnp.dot(qm, state.astype(MXU_DTYPE), preferred_element_type=F32) * qdec
        k_dec_t = (k * kdec).T.astype(MXU_DTYPE)
        state_ref[...] = state * chunk_decay + jnp.dot(k_dec_t, v, preferred_element_type=F32)
        out = inner + cross
        mu = jnp.mean(out, axis=-1, keepdims=True)
        cen = out - mu
        var = jnp.mean(cen * cen, axis=-1, keepdims=True)
        normed = cen * lax.rsqrt(var + RMS_EPS) * gn
        g = g_ref[0, rows, :]
        o_ref[0, rows, :] = ((g * _sigmoid(g)) * normed).astype(o_ref.dtype)


def retention_core(proj, gn_gain, *, chunk=2 * RET_CHUNK, tokens_per_step=512):
    B, S, P = proj.shape
    H = RET_HEADS
    dk = P // (6 * H)
    dv = 2 * dk
    tc = min(tokens_per_step, S)
    assert S % tc == 0 and tc % chunk == 0
    half = dk // 2
    pos = jnp.arange(S, dtype=F32)
    freqs = ROPE_BASE ** (-jnp.arange(half, dtype=F32) / half)
    ang = pos[:, None] * freqs[None, :]
    cos, sin = jnp.cos(ang), jnp.sin(ang)
    log_gamma = jnp.log(1.0 - 2.0 ** (-5.0 - jnp.arange(H, dtype=F32)))
    idx = jnp.arange(chunk, dtype=F32)
    diff = idx[:, None] - idx[None, :]
    dmask = jnp.where(diff >= 0, jnp.exp(jnp.maximum(diff, 0.0)[None] * log_gamma[:, None, None]), 0.0)
    qdec = jnp.exp((idx[None, :] + 1.0) * log_gamma[:, None])[:, :, None]
    kdec = jnp.exp((chunk - 1.0 - idx[None, :]) * log_gamma[:, None])[:, :, None]
    kern = functools.partial(_retention_kernel, chunk=chunk, n_chunks=tc // chunk, k_scale=dk ** -0.5)
    k_off, v_off, g_off = (H * dk) // dk, (2 * H * dk) // dv, (2 * H * dk + H * dv) // dv
    return pl.pallas_call(
        kern,
        grid=(B, H, S // tc),
        in_specs=[
            pl.BlockSpec((1, tc, dk), lambda b, h, c: (b, c, h)),
            pl.BlockSpec((1, tc, dk), lambda b, h, c: (b, c, k_off + h)),
            pl.BlockSpec((1, tc, dv), lambda b, h, c: (b, c, v_off + h)),
            pl.BlockSpec((1, tc, dv), lambda b, h, c: (b, c, g_off + h)),
            pl.BlockSpec((tc, half), lambda b, h, c: (c, 0)),
            pl.BlockSpec((tc, half), lambda b, h, c: (c, 0)),
            pl.BlockSpec((1, chunk, chunk), lambda b, h, c: (h, 0, 0)),
            pl.BlockSpec((1, chunk, 1), lambda b, h, c: (h, 0, 0)),
            pl.BlockSpec((1, chunk, 1), lambda b, h, c: (h, 0, 0)),
            pl.BlockSpec((1, dv), lambda b, h, c: (0, h)),
        ],
        out_specs=pl.BlockSpec((1, tc, dv), lambda b, h, c: (b, c, h)),
        out_shape=jax.ShapeDtypeStruct((B, S, H * dv), MXU_DTYPE),
        scratch_shapes=[pltpu.VMEM((dk, dv), F32)],
        compiler_params=_cparams("arbitrary", "arbitrary", "arbitrary"),
        name="retention_core",
    )(proj, proj, proj, proj, cos, sin, dmask, qdec, kdec, gn_gain.reshape(1, H * dv))


def _conv_ffn_kernel(x_ref, g_ref, wa_ref, wb_ref, cwa_ref, cwb_ref, cba_ref, cbb_ref, wo_ref, fg_ref,
                     o_ref, xn_ref, acc_ref, carry_a_ref, carry_b_ref, *, tiles_per_seq, final_norm):
    i, j = pl.program_id(0), pl.program_id(1)
    tm = x_ref.shape[0]

    @pl.when((i == 0) & (j == 0))
    def _():
        carry_a_ref[...] = jnp.zeros_like(carry_a_ref)
        carry_b_ref[...] = jnp.zeros_like(carry_b_ref)

    @pl.when(j == 0)
    def _():
        xn_ref[...] = _rms_normalize(x_ref[...], g_ref[...]).astype(xn_ref.dtype)
        acc_ref[...] = jnp.zeros_like(acc_ref)

    seq_start = (i % tiles_per_seq) == 0

    def causal_conv(w_ref, cw_ref, cb_ref, carry_ref):
        u = jnp.dot(xn_ref[...], w_ref[...], preferred_element_type=F32)
        prev = jnp.where(seq_start, 0.0, carry_ref[j])
        carry_ref[j] = u[tm - 8:, :]
        row = lax.broadcasted_iota(jnp.int32, u.shape, 0)
        u1 = jnp.where(row == 0, prev[7:8, :], pltpu.roll(u, 1, 0))
        u2 = jnp.where(row == 0, prev[6:7, :], jnp.where(row == 1, prev[7:8, :], pltpu.roll(u, 2, 0)))
        cw = cw_ref[...]
        return ((cb_ref[...] + u2 * cw[0:1, :]) + u1 * cw[1:2, :]) + u * cw[2:3, :]

    a = causal_conv(wa_ref, cwa_ref, cba_ref, carry_a_ref)
    b = causal_conv(wb_ref, cwb_ref, cbb_ref, carry_b_ref)
    act = ((a * _sigmoid(a)) * b).astype(wo_ref.dtype)
    acc_ref[...] += jnp.dot(act, wo_ref[...], preferred_element_type=F32)

    @pl.when(j == pl.num_programs(1) - 1)
    def _():
        out = x_ref[...] + acc_ref[...]
        if final_norm:
            out = _rms_normalize(out, fg_ref[...])
        o_ref[...] = out


def conv_ffn_block(h, seq_len, norm_gain, w_in, conv_w, conv_b, w_out, final_gain, *, final_norm, tm=512, tn=512):
    T, D = h.shape
    F = w_out.shape[0]
    tm, tn = min(tm, seq_len), min(tn, F)
    assert seq_len % tm == 0 and F % tn == 0 and tm >= 8
    nj = F // tn
    kern = functools.partial(_conv_ffn_kernel, tiles_per_seq=seq_len // tm, final_norm=final_norm)
    return pl.pallas_call(
        kern,
        grid=(T // tm, nj),
        in_specs=[
            pl.BlockSpec((tm, D), lambda i, j: (i, 0)),
            pl.BlockSpec((1, D), lambda i, j: (0, 0)),
            pl.BlockSpec((D, tn), lambda i, j: (0, j)),
            pl.BlockSpec((D, tn), lambda i, j: (0, nj + j)),
            pl.BlockSpec((CONV_W, tn), lambda i, j: (0, j)),
            pl.BlockSpec((CONV_W, tn), lambda i, j: (0, nj + j)),
            pl.BlockSpec((1, tn), lambda i, j: (0, j)),
            pl.BlockSpec((1, tn), lambda i, j: (0, nj + j)),
            pl.BlockSpec((tn, D), lambda i, j: (j, 0)),
            pl.BlockSpec((1, D), lambda i, j: (0, 0)),
        ],
        out_specs=pl.BlockSpec((tm, D), lambda i, j: (i, 0)),
        out_shape=jax.ShapeDtypeStruct((T, D), F32),
        scratch_shapes=[
            pltpu.VMEM((tm, D), w_in.dtype),
            pltpu.VMEM((tm, D), F32),
            pltpu.VMEM((nj, 8, tn), F32),
            pltpu.VMEM((nj, 8, tn), F32),
        ],
        compiler_params=_cparams("arbitrary", "arbitrary"),
        name="conv_ffn",
    )(h, norm_gain.reshape(1, D), w_in, w_in, conv_w, conv_w, conv_b.reshape(1, 2 * F), conv_b.reshape(1, 2 * F),
      w_out, final_gain.reshape(1, D))


def _kv_proj_kernel(x_ref, g_ref, w_ref, cmp_ref, kslc_ref, kwin_ref, vt_ref, *, seq_len):
    tm = x_ref.shape[0]
    d = NSA_HEAD_DIM
    Hkv = NSA_KV_HEADS
    xn = _rms_normalize(x_ref[...], g_ref[...]).astype(w_ref.dtype)
    res = jnp.dot(xn, w_ref[...], preferred_element_type=F32)
    col = lambda branch, hh: res[:, (branch * Hkv + hh) * d:(branch * Hkv + hh + 1) * d]
    pos = (pl.program_id(0) * tm) % seq_len + lax.broadcasted_iota(jnp.int32, (tm, LANE), 0)
    blk = (pos >> (SLC_BLOCK.bit_length() - 1)) & (LANE - 1)
    onehot = jnp.where(blk == lax.broadcasted_iota(jnp.int32, (tm, LANE), 1), 1.0, 0.0).astype(kslc_ref.dtype)
    ones = jnp.ones((BF16_SUBLANES, tm), vt_ref.dtype)
    for hh in range(Hkv):
        cmp_ref[hh] = col(0, hh)
        cmp_ref[Hkv + hh] = col(1, hh)
        kslc_ref[hh, :, :d] = col(2, hh).astype(kslc_ref.dtype)
        kslc_ref[hh, :, d:] = onehot
        kwin_ref[hh] = col(4, hh).astype(kwin_ref.dtype)
        vt_ref[hh, :d, :] = col(3, hh).T.astype(vt_ref.dtype)
        vt_ref[hh, d:, :] = ones
        vt_ref[Hkv + hh, :d, :] = col(5, hh).T.astype(vt_ref.dtype)
        vt_ref[Hkv + hh, d:, :] = ones


def kv_projection(h, seq_len, gain, w_kv, *, tm=256):
    T, D = h.shape
    Hkv, d = NSA_KV_HEADS, NSA_HEAD_DIM
    N = w_kv.shape[1]
    tm = min(tm, seq_len)
    assert seq_len % tm == 0 and N == 2 * N_BRANCH * Hkv * d and tm % LANE == 0
    row3 = lambda i: (0, i, 0)
    return pl.pallas_call(
        functools.partial(_kv_proj_kernel, seq_len=seq_len),
        grid=(T // tm,),
        in_specs=[
            pl.BlockSpec((tm, D), lambda i: (i, 0)),
            pl.BlockSpec((1, D), lambda i: (0, 0)),
            pl.BlockSpec((D, N), lambda i: (0, 0)),
        ],
        out_specs=[
            pl.BlockSpec((2 * Hkv, tm, d), row3),
            pl.BlockSpec((Hkv, tm, d + LANE), row3),
            pl.BlockSpec((Hkv, tm, d), row3),
            pl.BlockSpec((2 * Hkv, d + BF16_SUBLANES, tm), lambda i: (0, 0, i)),
        ],
        out_shape=[
            jax.ShapeDtypeStruct((2 * Hkv, T, d), F32),
            jax.ShapeDtypeStruct((Hkv, T, d + LANE), MXU_DTYPE),
            jax.ShapeDtypeStruct((Hkv, T, d), MXU_DTYPE),
            jax.ShapeDtypeStruct((2 * Hkv, d + BF16_SUBLANES, T), MXU_DTYPE),
        ],
        compiler_params=_cparams("arbitrary"),
        name="kv_projection",
    )(h, gain.reshape(1, D), w_kv)


def _q_proj_kernel(x_ref, g_ref, w_ref, qt_ref, gate_ref, *, q_scale):
    d = NSA_HEAD_DIM
    xn = _rms_normalize(x_ref[...], g_ref[...]).astype(w_ref.dtype)
    res = jnp.dot(xn, w_ref[...], preferred_element_type=F32)
    for hq in range(NSA_Q_HEADS):
        qt_ref[hq] = (res[:, hq * d:(hq + 1) * d] * q_scale).T.astype(qt_ref.dtype)
    gate_ref[...] = res[:, NSA_Q_HEADS * d:]


def q_projection(h, gain, w_q, *, tm=256):
    T, D = h.shape
    d = NSA_HEAD_DIM
    N = w_q.shape[1]
    tm = min(tm, T)
    assert T % tm == 0 and N == NSA_Q_HEADS * d + LANE and tm % LANE == 0
    return pl.pallas_call(
        functools.partial(_q_proj_kernel, q_scale=d ** -0.5),
        grid=(T // tm,),
        in_specs=[
            pl.BlockSpec((tm, D), lambda i: (i, 0)),
            pl.BlockSpec((1, D), lambda i: (0, 0)),
            pl.BlockSpec((D, N), lambda i: (0, 0)),
        ],
        out_specs=[
            pl.BlockSpec((NSA_Q_HEADS, d, tm), lambda i: (0, 0, i)),
            pl.BlockSpec((tm, LANE), lambda i: (i, 0)),
        ],
        out_shape=[
            jax.ShapeDtypeStruct((NSA_Q_HEADS, d, T), MXU_DTYPE),
            jax.ShapeDtypeStruct((T, LANE), F32),
        ],
        compiler_params=_cparams("arbitrary"),
        name="q_projection",
    )(h, gain.reshape(1, D), w_q)


def _compress_kernel(x_ref, pe_ref, w1_ref, w2_ref, o_ref):
    d = x_ref.shape[2]
    n = o_ref.shape[2]
    first = jnp.zeros((n, w1_ref.shape[2]), F32)
    second = jnp.zeros((n, w1_ref.shape[2]), F32)
    for p in range(CMP_STRIDE):
        xp = x_ref[0, pl.ds(p, n, stride=CMP_STRIDE), :]
        q = CMP_STRIDE + p
        first = first + jnp.dot((xp + pe_ref[0, p:p + 1, :]).astype(MXU_DTYPE), w1_ref[0, p * d:(p + 1) * d, :],
                                preferred_element_type=F32)
        second = second + jnp.dot((xp + pe_ref[0, q:q + 1, :]).astype(MXU_DTYPE), w1_ref[0, q * d:(q + 1) * d, :],
                                  preferred_element_type=F32)
    hid = first + pltpu.roll(second, n - 1, 0)
    gelu = 0.5 * hid * (1.0 + jnp.tanh(0.7978845608028654 * (hid + 0.044715 * (hid * hid * hid))))
    out = jnp.dot(gelu.astype(MXU_DTYPE), w2_ref[0], preferred_element_type=F32)
    row = lax.broadcasted_iota(jnp.int32, out.shape, 0)
    o_ref[0, 0] = jnp.where(row < n - 1, out, 0.0)


def compress_tokens(x, batch, pe, w1, w2):
    H2, T, d = x.shape
    Hkv = H2 // 2
    S = T // batch
    n = S // CMP_STRIDE
    hid = w2.shape[1]
    return pl.pallas_call(
        _compress_kernel,
        grid=(2, batch, Hkv),
        in_specs=[
            pl.BlockSpec((1, S, d), lambda s, b, hh: (s * Hkv + hh, b, 0)),
            pl.BlockSpec((1, CMP_BLOCK, d), lambda s, b, hh: (s, 0, 0)),
            pl.BlockSpec((1, CMP_BLOCK * d, hid), lambda s, b, hh: (s, 0, 0)),
            pl.BlockSpec((1, hid, d), lambda s, b, hh: (s, 0, 0)),
        ],
        out_specs=pl.BlockSpec((1, 1, n, d), lambda s, b, hh: (s, b * Hkv + hh, 0, 0)),
        out_shape=jax.ShapeDtypeStruct((2, batch * Hkv, n, d), F32),
        compiler_params=_cparams("arbitrary", "arbitrary", "arbitrary"),
        name="compress_tokens",
    )(x, pe, w1, w2)


def _nsa_cmp_kernel(q_ref, kc_ref, vct_ref, o_ref, sel_ref, qt_ref, *, top_k):
    i = pl.program_id(2)
    tq = q_ref.shape[2]
    d = NSA_HEAD_DIM
    G = NSA_GROUP
    nsel = kc_ref.shape[2] // SLC_RATIO
    qsz = nsel // CMP_SPANS
    span_rows = SLC_RATIO * qsz
    t0 = i * tq
    for g in range(G):
        qt_ref[:, g * tq:(g + 1) * tq] = q_ref[g]

    def attend(n_spans):
        nrows, nj = n_spans * span_rows, n_spans * qsz
        row = lax.broadcasted_iota(jnp.int32, (nrows, tq), 0)
        tok = t0 + lax.broadcasted_iota(jnp.int32, (nrows, tq), 1)
        blk = (row >> (span_rows.bit_length() - 1)) * qsz + (row & (qsz - 1))
        cblk = SLC_RATIO * blk + ((row >> (qsz.bit_length() - 1)) & (SLC_RATIO - 1))
        bias = jnp.where(cblk * CMP_STRIDE + (CMP_BLOCK - 1) <= tok, 0.0, NEG)
        s = (jnp.dot(kc_ref[0, 0, :nrows, :], qt_ref[...], preferred_element_type=F32)
             + jnp.concatenate([bias] * G, axis=1))
        e = jnp.exp(s - jnp.max(s, axis=0, keepdims=True))
        denom = jnp.sum(e, axis=0, keepdims=True)
        any_valid = jnp.concatenate([tok[0:1, :] >= CMP_BLOCK - 1] * G, axis=1)
        p = e * jnp.where(any_valid, 1.0 / denom, 0.0)
        out_t = jnp.dot(vct_ref[0, 0, :, :nrows], p.astype(MXU_DTYPE), preferred_element_type=F32)
        imp = None
        for g in range(G):
            o_ref[0, :, g * d:(g + 1) * d] = out_t[:, g * tq:(g + 1) * tq].T
            pg = p[:, g * tq:(g + 1) * tq]
            imp = pg if imp is None else imp + pg

        def phase(r):
            return jnp.concatenate([imp[q * span_rows + r * qsz:q * span_rows + (r + 1) * qsz, :]
                                    for q in range(n_spans)], axis=0)

        j = lax.broadcasted_iota(jnp.int32, (nj, tq), 0)
        p_slc = jnp.where(j == 0, 0.0, pltpu.roll(phase(SLC_RATIO - 1), 1, 0))
        for r in range(SLC_RATIO):
            p_slc = p_slc + phase(r)
        cur = (t0 + lax.broadcasted_iota(jnp.int32, (nj, tq), 1)) >> (SLC_BLOCK.bit_length() - 1)
        causal = j <= cur
        if nj < nsel:
            sel_ref[0, 0, nj:, :] = jnp.full((nsel - nj, tq), MASK_BIAS, sel_ref.dtype)

        @pl.when(t0 < top_k * SLC_BLOCK)
        def _():
            sel_ref[0, 0, :nj, :] = jnp.where(causal, 0.0, MASK_BIAS).astype(sel_ref.dtype)

        @pl.when(t0 >= top_k * SLC_BLOCK)
        def _():
            forced = (j == 0) | (j == cur) | (j == cur - 1)
            score = jnp.where(causal & jnp.logical_not(forced), p_slc, PICKED)
            jf = j.astype(F32)
            for _ in range(top_k - 3):
                m = jnp.max(score, axis=0, keepdims=True)
                first = jnp.min(jnp.where(score == m, jf, float(nsel)), axis=0, keepdims=True)
                score = jnp.where(jf == first, PICKED, score)
            sel_ref[0, 0, :nj, :] = jnp.where((score == PICKED) & causal, 0.0, MASK_BIAS).astype(sel_ref.dtype)

    last_span = (t0 + tq - 1) // (qsz * SLC_BLOCK)
    for n_spans in range(1, CMP_SPANS + 1):
        pl.when(last_span == n_spans - 1)(functools.partial(attend, n_spans))


def nsa_cmp_branch(qt, B, kc, vct, *, tq=512):
    S = qt.shape[2] // B
    Hkv, G, d = NSA_KV_HEADS, NSA_GROUP, NSA_HEAD_DIM
    ncp = kc.shape[2]
    nsel = ncp // SLC_RATIO
    top_k = min(SLC_TOPK, nsel)
    tq = min(tq, S)
    assert S % tq == 0 and nsel & (nsel - 1) == 0 and tq & (tq - 1) == 0
    assert (top_k * SLC_BLOCK) % tq == 0 and top_k >= 3
    assert nsel % (CMP_SPANS * BF16_SUBLANES) == 0 and (SLC_RATIO * nsel // CMP_SPANS) % LANE == 0
    assert (nsel // CMP_SPANS * SLC_BLOCK) % tq == 0
    kern = functools.partial(_nsa_cmp_kernel, top_k=top_k)
    return pl.pallas_call(
        kern,
        grid=(B, Hkv, S // tq),
        in_specs=[
            pl.BlockSpec((G, d, tq), lambda b, h, i: (h, 0, b * (S // tq) + i)),
            pl.BlockSpec((1, 1, ncp, d), lambda b, h, i: (b, h, 0, 0)),
            pl.BlockSpec((1, 1, d, ncp), lambda b, h, i: (b, h, 0, 0)),
        ],
        out_specs=[
            pl.BlockSpec((1, tq, G * d), lambda b, h, i: (b, i, h)),
            pl.BlockSpec((1, 1, nsel, tq), lambda b, h, i: (b, h, 0, i)),
        ],
        out_shape=[
            jax.ShapeDtypeStruct((B, S, Hkv * G * d), F32),
            jax.ShapeDtypeStruct((B, Hkv, nsel, S), MXU_DTYPE),
        ],
        scratch_shapes=[pltpu.VMEM((d, G * tq), MXU_DTYPE)],
        compiler_params=_cparams("arbitrary", "arbitrary", "arbitrary"),
        name="nsa_cmp",
    )(qt, kc, vct)


def _nsa_slc_kernel(q_ref, sel_ref, k_ref, vt_ref, o_ref, qa_ref, m_ref, acc_ref, s_ref, *, tk):
    i = pl.program_id(2)
    tq = q_ref.shape[2]
    d = NSA_HEAD_DIM
    G = NSA_GROUP
    nblk = sel_ref.shape[2]
    span = LANE * SLC_BLOCK
    n_halves = pl.cdiv(nblk, LANE)

    m_ref[...] = jnp.full_like(m_ref, PICKED)
    acc_ref[...] = jnp.zeros_like(acc_ref)
    for g in range(G):
        qa_ref[:d, g * tq:(g + 1) * tq] = q_ref[g]

    def scores(kt_idx):
        k0 = pl.multiple_of(kt_idx * tk, tk)
        return jnp.dot(k_ref[0, pl.ds(k0, tk), :], qa_ref[...], preferred_element_type=F32)

    def accumulate(kt_idx, s):
        k0 = pl.multiple_of(kt_idx * tk, tk)
        m_old = m_ref[...]
        m_new = jnp.maximum(m_old, jnp.max(s, axis=0, keepdims=True))
        alpha = jnp.exp(m_old - m_new)
        p = jnp.exp(s - m_new)
        acc_ref[...] = alpha * acc_ref[...] + jnp.dot(vt_ref[0, :, pl.ds(k0, tk)], p.astype(MXU_DTYPE),
                                                      preferred_element_type=F32)
        m_ref[...] = m_new

    last = ((i + 1) * tq - 1) // tk
    tiles_per_half = span // tk
    for hf in range(n_halves):
        lo = hf * tiles_per_half
        hi = jnp.minimum(last, lo + tiles_per_half)

        @pl.when(lo <= last)
        def _():
            sel_half = sel_ref[0, 0, hf * LANE:(hf + 1) * LANE, :]
            for g in range(G):
                qa_ref[d:, g * tq:(g + 1) * tq] = sel_half
            n = hi - lo
            odd = (n & 1) == 1
            diag_here = last < lo + tiles_per_half

            def accumulate_diagonal(s):
                col = lax.broadcasted_iota(jnp.int32, s.shape, 1)
                tok = i * tq + (col & (tq - 1))
                key = last * tk + lax.broadcasted_iota(jnp.int32, s.shape, 0)
                accumulate(last, jnp.where(key <= tok, s, NEG))

            s_ref[0] = scores(lo)

            def pair(pi, carry):
                t = lo + 2 * pi
                s_ref[1] = scores(t + 1)
                accumulate(t, s_ref[0])
                s_ref[0] = scores(t + 2)
                accumulate(t + 1, s_ref[1])
                return carry

            lax.fori_loop(0, n >> 1, pair, 0)

            @pl.when(odd & diag_here)
            def _():
                s_ref[1] = scores(last)
                accumulate(hi - 1, s_ref[0])
                accumulate_diagonal(s_ref[1])

            @pl.when(odd & jnp.logical_not(diag_here))
            def _():
                accumulate(hi - 1, s_ref[0])

            @pl.when(jnp.logical_not(odd) & diag_here)
            def _():
                accumulate_diagonal(s_ref[0])

    out = acc_ref[:d, :] * (1.0 / acc_ref[d:d + 1, :])
    for g in range(G):
        o_ref[0, :, g * d:(g + 1) * d] = out[:, g * tq:(g + 1) * tq].T


def nsa_slc_branch(qt, B, sel, k_aug, vt, *, tq=256, tk=1024):
    S = qt.shape[2] // B
    Hkv, G, d = NSA_KV_HEADS, NSA_GROUP, NSA_HEAD_DIM
    nblk = sel.shape[2]
    dv = vt.shape[1]
    tq, tk = min(tq, S), min(tk, S)
    assert S % tq == 0 and S % tk == 0 and tk % tq == 0 and tq & (tq - 1) == 0
    assert nblk % LANE == 0 and (LANE * SLC_BLOCK) % tk == 0
    kern = functools.partial(_nsa_slc_kernel, tk=tk)
    return pl.pallas_call(
        kern,
        grid=(B, Hkv, S // tq),
        in_specs=[
            pl.BlockSpec((G, d, tq), lambda b, h, i: (h, 0, b * (S // tq) + i)),
            pl.BlockSpec((1, 1, nblk, tq), lambda b, h, i: (b, h, 0, i)),
            pl.BlockSpec((1, S, d + LANE), lambda b, h, i: (h, b, 0)),
            pl.BlockSpec((1, dv, S), lambda b, h, i: (h, 0, b)),
        ],
        out_specs=pl.BlockSpec((1, tq, G * d), lambda b, h, i: (b, i, h)),
        out_shape=jax.ShapeDtypeStruct((B, S, Hkv * G * d), F32),
        scratch_shapes=[
            pltpu.VMEM((d + LANE, G * tq), MXU_DTYPE),
            pltpu.VMEM((1, G * tq), F32),
            pltpu.VMEM((dv, G * tq), F32),
            pltpu.VMEM((2, tk, G * tq), F32),
        ],
        compiler_params=_cparams("arbitrary", "arbitrary", "arbitrary"),
        name="nsa_slc",
    )(qt, sel, k_aug, vt)


def _nsa_win_kernel(q_ref, k_ref, vt_ref, bias_ref, o_ref, qt_ref):
    i = pl.program_id(2)
    tq = q_ref.shape[2]
    d = NSA_HEAD_DIM
    G = NSA_GROUP
    nk = WIN + tq
    k0 = pl.multiple_of(jnp.maximum(i * tq - WIN, 0), tq)
    for g in range(G):
        qt_ref[:, g * tq:(g + 1) * tq] = q_ref[g]
    s = (jnp.dot(k_ref[0, pl.ds(k0, nk), :], qt_ref[...], preferred_element_type=F32)
         + jnp.concatenate([bias_ref[0]] * G, axis=1))
    e = jnp.exp(s - jnp.max(s, axis=0, keepdims=True))
    p = e * (1.0 / jnp.sum(e, axis=0, keepdims=True))
    out_t = jnp.dot(vt_ref[0, :d, pl.ds(k0, nk)], p.astype(MXU_DTYPE), preferred_element_type=F32)
    for g in range(G):
        o_ref[0, :, g * d:(g + 1) * d] = out_t[:, g * tq:(g + 1) * tq].T


def nsa_win_branch(qt, B, k_win, vt, *, tq=256):
    S = qt.shape[2] // B
    Hkv, G, d = NSA_KV_HEADS, NSA_GROUP, NSA_HEAD_DIM
    dv = vt.shape[1]
    tq = min(tq, S)
    assert S % tq == 0 and tq % LANE == 0 and tq & (tq - 1) == 0 and WIN % tq == 0 and S >= WIN + tq
    kern = _nsa_win_kernel
    n_clipped = WIN // tq
    t0 = jnp.minimum(jnp.arange(n_clipped + 1) * tq, WIN)[:, None, None]
    pos = (t0 - jnp.minimum(t0, WIN)) + jnp.arange(WIN + tq)[None, :, None]
    tok = t0 + jnp.arange(tq)[None, None, :]
    bias = jnp.where((pos <= tok) & (pos > tok - WIN), 0.0, NEG).astype(F32)
    return pl.pallas_call(
        kern,
        grid=(B, Hkv, S // tq),
        in_specs=[
            pl.BlockSpec((G, d, tq), lambda b, h, i: (h, 0, b * (S // tq) + i)),
            pl.BlockSpec((1, S, d), lambda b, h, i: (h, b, 0)),
            pl.BlockSpec((1, dv, S), lambda b, h, i: (Hkv + h, 0, b)),
            pl.BlockSpec((1, WIN + tq, tq), lambda b, h, i: (jnp.minimum(i, n_clipped), 0, 0)),
        ],
        out_specs=pl.BlockSpec((1, tq, G * d), lambda b, h, i: (b, i, h)),
        out_shape=jax.ShapeDtypeStruct((B, S, Hkv * G * d), F32),
        scratch_shapes=[pltpu.VMEM((d, G * tq), MXU_DTYPE)],
        compiler_params=_cparams("arbitrary", "arbitrary", "arbitrary"),
        name="nsa_win",
    )(qt, k_win, vt, bias)


def _nsa_out_kernel(oc_ref, os_ref, ow_ref, gate_ref, w_ref, h_ref, o_ref, merged_ref):
    d = NSA_HEAD_DIM
    gates = _sigmoid(gate_ref[...])
    for hq in range(NSA_Q_HEADS):
        cols = slice(hq * d, (hq + 1) * d)
        c = N_BRANCH * hq
        merged = (gates[:, c:c + 1] * oc_ref[:, cols] + gates[:, c + 1:c + 2] * os_ref[:, cols]
                  + gates[:, c + 2:c + 3] * ow_ref[:, cols])
        merged_ref[:, cols] = merged.astype(merged_ref.dtype)
    o_ref[...] = h_ref[...] + jnp.dot(merged_ref[...], w_ref[...], preferred_element_type=F32)


def nsa_out(o_cmp, o_slc, o_win, gate_logits, w_o, h, *, tm=256):
    T, HD = o_cmp.shape
    D = w_o.shape[1]
    tm = min(tm, T)
    assert T % tm == 0 and gate_logits.shape == (T, LANE)
    row = lambda i: (i, 0)
    return pl.pallas_call(
        _nsa_out_kernel,
        grid=(T // tm,),
        in_specs=[
            pl.BlockSpec((tm, HD), row),
            pl.BlockSpec((tm, HD), row),
            pl.BlockSpec((tm, HD), row),
            pl.BlockSpec((tm, LANE), row),
            pl.BlockSpec((HD, D), lambda i: (0, 0)),
            pl.BlockSpec((tm, D), row),
        ],
        out_specs=pl.BlockSpec((tm, D), row),
        out_shape=jax.ShapeDtypeStruct((T, D), F32),
        scratch_shapes=[pltpu.VMEM((tm, HD), w_o.dtype)],
        compiler_params=_cparams("arbitrary"),
        name="nsa_out",
    )(o_cmp, o_slc, o_win, gate_logits, w_o, h)


def _span_major(x, nsel):
    B, n, H, d = x.shape
    qsz = nsel // CMP_SPANS
    return x.reshape(B, CMP_SPANS, qsz, SLC_RATIO, H, d).transpose(0, 4, 1, 3, 2, 5).reshape(B, H, n, d)


def _cast(w):
    return w.astype(MXU_DTYPE)


def nsa_shared_kv(h, B, S, kv_gain, w_kv, cmp_pe_k, cmp_w1_k, cmp_w2_k, cmp_pe_v, cmp_w1_v, cmp_w2_v):
    Hkv, d = NSA_KV_HEADS, NSA_HEAD_DIM
    cmp_in, k_slc, k_win, vt = kv_projection(h, S, kv_gain, _cast(w_kv))
    n = S // CMP_STRIDE
    nsel = S // SLC_BLOCK
    cmp = compress_tokens(cmp_in, B, jnp.stack([cmp_pe_k, cmp_pe_v]), _cast(jnp.stack([cmp_w1_k, cmp_w1_v])),
                          _cast(jnp.stack([cmp_w2_k, cmp_w2_v])))
    cmp = cmp.reshape(2, B, Hkv, n, d).transpose(0, 1, 3, 2, 4)
    kc = _cast(_span_major(cmp[0], nsel))
    vct = _cast(_span_major(cmp[1], nsel).transpose(0, 1, 3, 2))
    return kc, vct, k_slc, k_win, vt


def nsa_attention_block(h, B, S, mix_gain, w_q, w_o, shared):
    kc, vct, k_slc, k_win, vt = shared
    T = B * S
    nq = NSA_Q_HEADS * NSA_HEAD_DIM
    nqp = nq + LANE
    w_q = jnp.pad(w_q, ((0, 0), (0, nqp - w_q.shape[1])))
    qt, gate_logits = q_projection(h, mix_gain, _cast(w_q))
    o_cmp, sel = nsa_cmp_branch(qt, B, kc, vct)
    o_slc = nsa_slc_branch(qt, B, sel, k_slc, vt)
    o_win = nsa_win_branch(qt, B, k_win, vt)
    return nsa_out(o_cmp.reshape(T, nq), o_slc.reshape(T, nq), o_win.reshape(T, nq), gate_logits, _cast(w_o), h)


def nsa_mixer(h, B, S, mix_gain, kv_gain, w_kv, cmp_pe_k, cmp_w1_k, cmp_w2_k, cmp_pe_v, cmp_w1_v, cmp_w2_v, w_q, w_o):
    shared = nsa_shared_kv(h, B, S, kv_gain, w_kv, cmp_pe_k, cmp_w1_k, cmp_w2_k, cmp_pe_v, cmp_w1_v, cmp_w2_v)
    return nsa_attention_block(h, B, S, mix_gain, w_q, w_o, shared)


def kernel(x, norm_mix_gain, norm_ffn_gain, ret_w_in, ret_gn_gain, ret_w_out, nsa_kv_norm_gain, nsa_w_kv, cmp_pe_k, cmp_w1_k, cmp_w2_k, cmp_pe_v, cmp_w1_v, cmp_w2_v, nsa_w_q, nsa_w_o, ffn_w_in, ffn_conv_w, ffn_conv_b, ffn_w_out, final_norm_gain):
    B, S, D = x.shape
    T = B * S
    depth = norm_mix_gain.shape[0]
    n_a = depth // 2
    h = x.reshape(T, D)
    shared = None
    for layer in range(depth):
        if layer < n_a:
            proj = norm_matmul(h, norm_mix_gain[layer], _cast(ret_w_in[layer]), tm=1024, tn=1024)
            y = retention_core(proj.reshape(B, S, -1), ret_gn_gain[layer])
            h = matmul_residual(y.reshape(T, -1), _cast(ret_w_out[layer]), h, tm=512, tn=1024)
        else:
            if layer == n_a:
                shared = nsa_shared_kv(h, B, S, nsa_kv_norm_gain, nsa_w_kv, cmp_pe_k, cmp_w1_k, cmp_w2_k,
                                       cmp_pe_v, cmp_w1_v, cmp_w2_v)
            b = layer - n_a
            h = nsa_attention_block(h, B, S, norm_mix_gain[layer], nsa_w_q[b], nsa_w_o[b], shared)
        h = conv_ffn_block(h, S, norm_ffn_gain[layer], _cast(ffn_w_in[layer]), ffn_conv_w[layer], ffn_conv_b[layer],
                           _cast(ffn_w_out[layer]), final_norm_gain, final_norm=(layer == depth - 1))
    return h.reshape(B, S, D)
```

```python
import functools

import jax
import jax.numpy as jnp
from jax import lax
from jax.experimental import pallas as pl
from jax.experimental.pallas import tpu as pltpu

F32 = jnp.float32
MXU_DTYPE = jnp.bfloat16

RMS_EPS = 1e-6
ROPE_BASE = 10000.0

RET_HEADS = 8
RET_CHUNK = 128

NSA_Q_HEADS = 16
NSA_KV_HEADS = 4
NSA_GROUP = NSA_Q_HEADS // NSA_KV_HEADS
NSA_HEAD_DIM = 128
N_BRANCH = 3
CMP_BLOCK = 32
CMP_STRIDE = 16
CMP_RATIO = CMP_BLOCK // CMP_STRIDE
SLC_BLOCK = 64
SLC_TOPK = 16
SLC_RATIO = SLC_BLOCK // CMP_STRIDE
CMP_SPANS = 4
WIN = 512
CONV_W = 3

NEG = -1e30
MASK_BIAS = -(2.0 ** 100)
PICKED = -3.0e38
SEL_FORCE = 1e30

LANE = 128
BF16_SUBLANES = 16
VMEM_LIMIT_BYTES = 56 * 1024 * 1024


def _cparams(*sem):
    return pltpu.CompilerParams(dimension_semantics=sem, vmem_limit_bytes=VMEM_LIMIT_BYTES)


def _sigmoid(x):
    return 1.0 / (1.0 + jnp.exp(-x))


def _rms_normalize(x, gain):
    ms = jnp.mean(x * x, axis=-1, keepdims=True)
    return x * lax.rsqrt(ms + RMS_EPS) * gain


def _norm_matmul_kernel(x_ref, g_ref, w_ref, o_ref, xn_ref):
    @pl.when(pl.program_id(1) == 0)
    def _():
        xn_ref[...] = _rms_normalize(x_ref[...], g_ref[...]).astype(xn_ref.dtype)

    o_ref[...] = jnp.dot(xn_ref[...], w_ref[...], preferred_element_type=F32).astype(o_ref.dtype)


def norm_matmul(h, gain, w, *, tm, tn, out_dtype=F32):
    T, D = h.shape
    N = w.shape[1]
    tm, tn = min(tm, T), min(tn, N)
    assert T % tm == 0 and N % tn == 0
    return pl.pallas_call(
        _norm_matmul_kernel,
        grid=(T // tm, N // tn),
        in_specs=[
            pl.BlockSpec((tm, D), lambda i, j: (i, 0)),
            pl.BlockSpec((1, D), lambda i, j: (0, 0)),
            pl.BlockSpec((D, tn), lambda i, j: (0, j)),
        ],
        out_specs=pl.BlockSpec((tm, tn), lambda i, j: (i, j)),
        out_shape=jax.ShapeDtypeStruct((T, N), out_dtype),
        scratch_shapes=[pltpu.VMEM((tm, D), w.dtype)],
        compiler_params=_cparams("arbitrary", "arbitrary"),
        name="norm_matmul",
    )(h, gain.reshape(1, D), w)


def _matmul_residual_kernel(y_ref, w_ref, h_ref, o_ref):
    o_ref[...] = h_ref[...] + jnp.dot(y_ref[...], w_ref[...], preferred_element_type=F32)


def matmul_residual(y, w, h, *, tm, tn):
    T, K = y.shape
    N = w.shape[1]
    tm, tn = min(tm, T), min(tn, N)
    assert T % tm == 0 and N % tn == 0
    return pl.pallas_call(
        _matmul_residual_kernel,
        grid=(T // tm, N // tn),
        in_specs=[
            pl.BlockSpec((tm, K), lambda i, j: (i, 0)),
            pl.BlockSpec((K, tn), lambda i, j: (0, j)),
            pl.BlockSpec((tm, tn), lambda i, j: (i, j)),
        ],
        out_specs=pl.BlockSpec((tm, tn), lambda i, j: (i, j)),
        out_shape=jax.ShapeDtypeStruct((T, N), F32),
        compiler_params=_cparams("arbitrary", "arbitrary"),
        name="matmul_residual",
    )(y, w, h)


def _retention_kernel(q_ref, k_ref, v_ref, g_ref, cos_ref, sin_ref, dmask_ref, qdec_ref, kdec_ref, gn_ref,
                      o_ref, state_ref, *, chunk, n_chunks, k_scale):
    @pl.when(pl.program_id(2) == 0)
    def _():
        state_ref[...] = jnp.zeros_like(state_ref)

    half = q_ref.shape[-1] // 2
    dmask = dmask_ref[0]
    qdec = qdec_ref[0]
    kdec = kdec_ref[0]
    chunk_decay = qdec[chunk - 1:chunk, :]
    gn = gn_ref[...]

    def rotate(x, cos, sin):
        x1, x2 = x[:, :half], x[:, half:]
        return jnp.concatenate([x1 * cos - x2 * sin, x1 * sin + x2 * cos], axis=-1)

    for ci in range(n_chunks):
        rows = pl.ds(ci * chunk, chunk)
        cos, sin = cos_ref[rows, :], sin_ref[rows, :]
        q = rotate(q_ref[0, rows, :], cos, sin)
        k = rotate(k_ref[0, rows, :], cos, sin) * k_scale
        v = v_ref[0, rows, :].astype(MXU_DTYPE)
        qm = q.astype(MXU_DTYPE)
        scores = lax.dot_general(qm, k.astype(MXU_DTYPE), (((1,), (1,)), ((), ())),
                                 preferred_element_type=F32) * dmask
        inner = jnp.dot(scores.astype(MXU_DTYPE), v, preferred_element_type=F32)
        state = state_ref[...]
        cross = jnp.dot(qm, state.astype(MXU_DTYPE), preferred_element_type=F32) * qdec
        k_dec_t = (k * kdec).T.astype(MXU_DTYPE)
        state_ref[...] = state * chunk_decay + jnp.dot(k_dec_t, v, preferred_element_type=F32)
        out = inner + cross
        mu = jnp.mean(out, axis=-1, keepdims=True)
        cen = out - mu
        var = jnp.mean(cen * cen, axis=-1, keepdims=True)
        normed = cen * lax.rsqrt(var + RMS_EPS) * gn
        g = g_ref[0, rows, :]
        o_ref[0, rows, :] = ((g * _sigmoid(g)) * normed).astype(o_ref.dtype)


def retention_core(proj, gn_gain, *, chunk=2 * RET_CHUNK, tokens_per_step=512):
    B, S, P = proj.shape
    H = RET_HEADS
    dk = P // (6 * H)
    dv = 2 * dk
    tc = min(tokens_per_step, S)
    assert S % tc == 0 and tc % chunk == 0
    half = dk // 2
    pos = jnp.arange(S, dtype=F32)
    freqs = ROPE_BASE ** (-jnp.arange(half, dtype=F32) / half)
    ang = pos[:, None] * freqs[None, :]
    cos, sin = jnp.cos(ang), jnp.sin(ang)
    log_gamma = jnp.log(1.0 - 2.0 ** (-5.0 - jnp.arange(H, dtype=F32)))
    idx = jnp.arange(chunk, dtype=F32)
    diff = idx[:, None] - idx[None, :]
    dmask = jnp.where(diff >= 0, jnp.exp(jnp.maximum(diff, 0.0)[None] * log_gamma[:, None, None]), 0.0)
    qdec = jnp.exp((idx[None, :] + 1.0) * log_gamma[:, None])[:, :, None]
    kdec = jnp.exp((chunk - 1.0 - idx[None, :]) * log_gamma[:, None])[:, :, None]
    kern = functools.partial(_retention_kernel, chunk=chunk, n_chunks=tc // chunk, k_scale=dk ** -0.5)
    k_off, v_off, g_off = (H * dk) // dk, (2 * H * dk) // dv, (2 * H * dk + H * dv) // dv
    return pl.pallas_call(
        kern,
        grid=(B, H, S // tc),
        in_specs=[
            pl.BlockSpec((1, tc, dk), lambda b, h, c: (b, c, h)),
            pl.BlockSpec((1, tc, dk), lambda b, h, c: (b, c, k_off + h)),
            pl.BlockSpec((1, tc, dv), lambda b, h, c: (b, c, v_off + h)),
            pl.BlockSpec((1, tc, dv), lambda b, h, c: (b, c, g_off + h)),
            pl.BlockSpec((tc, half), lambda b, h, c: (c, 0)),
            pl.BlockSpec((tc, half), lambda b, h, c: (c, 0)),
            pl.BlockSpec((1, chunk, chunk), lambda b, h, c: (h, 0, 0)),
            pl.BlockSpec((1, chunk, 1), lambda b, h, c: (h, 0, 0)),
            pl.BlockSpec((1, chunk, 1), lambda b, h, c: (h, 0, 0)),
            pl.BlockSpec((1, dv), lambda b, h, c: (0, h)),
        ],
        out_specs=pl.BlockSpec((1, tc, dv), lambda b, h, c: (b, c, h)),
        out_shape=jax.ShapeDtypeStruct((B, S, H * dv), MXU_DTYPE),
        scratch_shapes=[pltpu.VMEM((dk, dv), F32)],
        compiler_params=_cparams("arbitrary", "arbitrary", "arbitrary"),
        name="retention_core",
    )(proj, proj, proj, proj, cos, sin, dmask, qdec, kdec, gn_gain.reshape(1, H * dv))


def _conv_ffn_kernel(x_ref, g_ref, wa_ref, wb_ref, cwa_ref, cwb_ref, cba_ref, cbb_ref, wo_ref, fg_ref,
                     o_ref, xn_ref, acc_ref, carry_a_ref, carry_b_ref, *, tiles_per_seq, final_norm):
    i, j = pl.program_id(0), pl.program_id(1)
    tm = x_ref.shape[0]

    @pl.when((i == 0) & (j == 0))
    def _():
        carry_a_ref[...] = jnp.zeros_like(carry_a_ref)
        carry_b_ref[...] = jnp.zeros_like(carry_b_ref)

    @pl.when(j == 0)
    def _():
        xn_ref[...] = _rms_normalize(x_ref[...], g_ref[...]).astype(xn_ref.dtype)
        acc_ref[...] = jnp.zeros_like(acc_ref)

    seq_start = (i % tiles_per_seq) == 0

    def causal_conv(w_ref, cw_ref, cb_ref, carry_ref):
        u = jnp.dot(xn_ref[...], w_ref[...], preferred_element_type=F32)
        prev = jnp.where(seq_start, 0.0, carry_ref[j])
        carry_ref[j] = u[tm - 8:, :]
        row = lax.broadcasted_iota(jnp.int32, u.shape, 0)
        u1 = jnp.where(row == 0, prev[7:8, :], pltpu.roll(u, 1, 0))
        u2 = jnp.where(row == 0, prev[6:7, :], jnp.where(row == 1, prev[7:8, :], pltpu.roll(u, 2, 0)))
        cw = cw_ref[...]
        return ((cb_ref[...] + u2 * cw[0:1, :]) + u1 * cw[1:2, :]) + u * cw[2:3, :]

    a = causal_conv(wa_ref, cwa_ref, cba_ref, carry_a_ref)
    b = causal_conv(wb_ref, cwb_ref, cbb_ref, carry_b_ref)
    act = ((a * _sigmoid(a)) * b).astype(wo_ref.dtype)
    acc_ref[...] += jnp.dot(act, wo_ref[...], preferred_element_type=F32)

    @pl.when(j == pl.num_programs(1) - 1)
    def _():
        out = x_ref[...] + acc_ref[...]
        if final_norm:
            out = _rms_normalize(out, fg_ref[...])
        o_ref[...] = out


def conv_ffn_block(h, seq_len, norm_gain, w_in, conv_w, conv_b, w_out, final_gain, *, final_norm, tm=512, tn=512):
    T, D = h.shape
    F = w_out.shape[0]
    tm, tn = min(tm, seq_len), min(tn, F)
    assert seq_len % tm == 0 and F % tn == 0 and tm >= 8
    nj = F // tn
    kern = functools.partial(_conv_ffn_kernel, tiles_per_seq=seq_len // tm, final_norm=final_norm)
    return pl.pallas_call(
        kern,
        grid=(T // tm, nj),
        in_specs=[
            pl.BlockSpec((tm, D), lambda i, j: (i, 0)),
            pl.BlockSpec((1, D), lambda i, j: (0, 0)),
            pl.BlockSpec((D, tn), lambda i, j: (0, j)),
            pl.BlockSpec((D, tn), lambda i, j: (0, nj + j)),
            pl.BlockSpec((CONV_W, tn), lambda i, j: (0, j)),
            pl.BlockSpec((CONV_W, tn), lambda i, j: (0, nj + j)),
            pl.BlockSpec((1, tn), lambda i, j: (0, j)),
            pl.BlockSpec((1, tn), lambda i, j: (0, nj + j)),
            pl.BlockSpec((tn, D), lambda i, j: (j, 0)),
            pl.BlockSpec((1, D), lambda i, j: (0, 0)),
        ],
        out_specs=pl.BlockSpec((tm, D), lambda i, j: (i, 0)),
        out_shape=jax.ShapeDtypeStruct((T, D), F32),
        scratch_shapes=[
            pltpu.VMEM((tm, D), w_in.dtype),
            pltpu.VMEM((tm, D), F32),
            pltpu.VMEM((nj, 8, tn), F32),
            pltpu.VMEM((nj, 8, tn), F32),
        ],
        compiler_params=_cparams("arbitrary", "arbitrary"),
        name="conv_ffn",
    )(h, norm_gain.reshape(1, D), w_in, w_in, conv_w, conv_w, conv_b.reshape(1, 2 * F), conv_b.reshape(1, 2 * F),
      w_out, final_gain.reshape(1, D))


def _kv_proj_kernel(x_ref, g_ref, w_ref, cmp_ref, kslc_ref, kwin_ref, vt_ref, *, seq_len):
    tm = x_ref.shape[0]
    d = NSA_HEAD_DIM
    Hkv = NSA_KV_HEADS
    xn = _rms_normalize(x_ref[...], g_ref[...]).astype(w_ref.dtype)
    res = jnp.dot(xn, w_ref[...], preferred_element_type=F32)
    col = lambda branch, hh: res[:, (branch * Hkv + hh) * d:(branch * Hkv + hh + 1) * d]
    pos = (pl.program_id(0) * tm) % seq_len + lax.broadcasted_iota(jnp.int32, (tm, LANE), 0)
    blk = (pos >> (SLC_BLOCK.bit_length() - 1)) & (LANE - 1)
    onehot = jnp.where(blk == lax.broadcasted_iota(jnp.int32, (tm, LANE), 1), 1.0, 0.0).astype(kslc_ref.dtype)
    ones = jnp.ones((BF16_SUBLANES, tm), vt_ref.dtype)
    for hh in range(Hkv):
        cmp_ref[hh] = col(0, hh)
        cmp_ref[Hkv + hh] = col(1, hh)
        kslc_ref[hh, :, :d] = col(2, hh).astype(kslc_ref.dtype)
        kslc_ref[hh, :, d:] = onehot
        kwin_ref[hh] = col(4, hh).astype(kwin_ref.dtype)
        vt_ref[hh, :d, :] = col(3, hh).T.astype(vt_ref.dtype)
        vt_ref[hh, d:, :] = ones
        vt_ref[Hkv + hh, :d, :] = col(5, hh).T.astype(vt_ref.dtype)
        vt_ref[Hkv + hh, d:, :] = ones


def kv_projection(h, seq_len, gain, w_kv, *, tm=256):
    T, D = h.shape
    Hkv, d = NSA_KV_HEADS, NSA_HEAD_DIM
    N = w_kv.shape[1]
    tm = min(tm, seq_len)
    assert seq_len % tm == 0 and N == 2 * N_BRANCH * Hkv * d and tm % LANE == 0
    row3 = lambda i: (0, i, 0)
    return pl.pallas_call(
        functools.partial(_kv_proj_kernel, seq_len=seq_len),
        grid=(T // tm,),
        in_specs=[
            pl.BlockSpec((tm, D), lambda i: (i, 0)),
            pl.BlockSpec((1, D), lambda i: (0, 0)),
            pl.BlockSpec((D, N), lambda i: (0, 0)),
        ],
        out_specs=[
            pl.BlockSpec((2 * Hkv, tm, d), row3),
            pl.BlockSpec((Hkv, tm, d + LANE), row3),
            pl.BlockSpec((Hkv, tm, d), row3),
            pl.BlockSpec((2 * Hkv, d + BF16_SUBLANES, tm), lambda i: (0, 0, i)),
        ],
        out_shape=[
            jax.ShapeDtypeStruct((2 * Hkv, T, d), F32),
            jax.ShapeDtypeStruct((Hkv, T, d + LANE), MXU_DTYPE),
            jax.ShapeDtypeStruct((Hkv, T, d), MXU_DTYPE),
            jax.ShapeDtypeStruct((2 * Hkv, d + BF16_SUBLANES, T), MXU_DTYPE),
        ],
        compiler_params=_cparams("arbitrary"),
        name="kv_projection",
    )(h, gain.reshape(1, D), w_kv)


def _q_proj_kernel(x_ref, g_ref, w_ref, qt_ref, gate_ref, *, q_scale):
    d = NSA_HEAD_DIM
    xn = _rms_normalize(x_ref[...], g_ref[...]).astype(w_ref.dtype)
    res = jnp.dot(xn, w_ref[...], preferred_element_type=F32)
    for hq in range(NSA_Q_HEADS):
        qt_ref[hq] = (res[:, hq * d:(hq + 1) * d] * q_scale).T.astype(qt_ref.dtype)
    gate_ref[...] = res[:, NSA_Q_HEADS * d:]


def q_projection(h, gain, w_q, *, tm=256):
    T, D = h.shape
    d = NSA_HEAD_DIM
    N = w_q.shape[1]
    tm = min(tm, T)
    assert T % tm == 0 and N == NSA_Q_HEADS * d + LANE and tm % LANE == 0
    return pl.pallas_call(
        functools.partial(_q_proj_kernel, q_scale=d ** -0.5),
        grid=(T // tm,),
        in_specs=[
            pl.BlockSpec((tm, D), lambda i: (i, 0)),
            pl.BlockSpec((1, D), lambda i: (0, 0)),
            pl.BlockSpec((D, N), lambda i: (0, 0)),
        ],
        out_specs=[
            pl.BlockSpec((NSA_Q_HEADS, d, tm), lambda i: (0, 0, i)),
            pl.BlockSpec((tm, LANE), lambda i: (i, 0)),
        ],
        out_shape=[
            jax.ShapeDtypeStruct((NSA_Q_HEADS, d, T), MXU_DTYPE),
            jax.ShapeDtypeStruct((T, LANE), F32),
        ],
        compiler_params=_cparams("arbitrary"),
        name="q_projection",
    )(h, gain.reshape(1, D), w_q)


def _compress_kernel(x_ref, pe_ref, w1_ref, w2_ref, o_ref):
    d = x_ref.shape[2]
    n = o_ref.shape[2]
    first = jnp.zeros((n, w1_ref.shape[2]), F32)
    second = jnp.zeros((n, w1_ref.shape[2]), F32)
    for p in range(CMP_STRIDE):
        xp = x_ref[0, pl.ds(p, n, stride=CMP_STRIDE), :]
        q = CMP_STRIDE + p
        first = first + jnp.dot((xp + pe_ref[0, p:p + 1, :]).astype(MXU_DTYPE), w1_ref[0, p * d:(p + 1) * d, :],
                                preferred_element_type=F32)
        second = second + jnp.dot((xp + pe_ref[0, q:q + 1, :]).astype(MXU_DTYPE), w1_ref[0, q * d:(q + 1) * d, :],
                                  preferred_element_type=F32)
    hid = first + pltpu.roll(second, n - 1, 0)
    gelu = 0.5 * hid * (1.0 + jnp.tanh(0.7978845608028654 * (hid + 0.044715 * (hid * hid * hid))))
    out = jnp.dot(gelu.astype(MXU_DTYPE), w2_ref[0], preferred_element_type=F32)
    row = lax.broadcasted_iota(jnp.int32, out.shape, 0)
    o_ref[0, 0] = jnp.where(row < n - 1, out, 0.0)


def compress_tokens(x, batch, pe, w1, w2):
    H2, T, d = x.shape
    Hkv = H2 // 2
    S = T // batch
    n = S // CMP_STRIDE
    hid = w2.shape[1]
    return pl.pallas_call(
        _compress_kernel,
        grid=(2, batch, Hkv),
        in_specs=[
            pl.BlockSpec((1, S, d), lambda s, b, hh: (s * Hkv + hh, b, 0)),
            pl.BlockSpec((1, CMP_BLOCK, d), lambda s, b, hh: (s, 0, 0)),
            pl.BlockSpec((1, CMP_BLOCK * d, hid), lambda s, b, hh: (s, 0, 0)),
            pl.BlockSpec((1, hid, d), lambda s, b, hh: (s, 0, 0)),
        ],
        out_specs=pl.BlockSpec((1, 1, n, d), lambda s, b, hh: (s, b * Hkv + hh, 0, 0)),
        out_shape=jax.ShapeDtypeStruct((2, batch * Hkv, n, d), F32),
        compiler_params=_cparams("arbitrary", "arbitrary", "arbitrary"),
        name="compress_tokens",
    )(x, pe, w1, w2)


def _nsa_cmp_kernel(q_ref, kc_ref, vct_ref, o_ref, sel_ref, qt_ref, *, top_k):
    i = pl.program_id(2)
    tq = q_ref.shape[2]
    d = NSA_HEAD_DIM
    G = NSA_GROUP
    nsel = kc_ref.shape[2] // SLC_RATIO
    qsz = nsel // CMP_SPANS
    span_rows = SLC_RATIO * qsz
    t0 = i * tq
    for g in range(G):
        qt_ref[:, g * tq:(g + 1) * tq] = q_ref[g]

    def attend(n_spans):
        nrows, nj = n_spans * span_rows, n_spans * qsz
        row = lax.broadcasted_iota(jnp.int32, (nrows, tq), 0)
        tok = t0 + lax.broadcasted_iota(jnp.int32, (nrows, tq), 1)
        blk = (row >> (span_rows.bit_length() - 1)) * qsz + (row & (qsz - 1))
        cblk = SLC_RATIO * blk + ((row >> (qsz.bit_length() - 1)) & (SLC_RATIO - 1))
        bias = jnp.where(cblk * CMP_STRIDE + (CMP_BLOCK - 1) <= tok, 0.0, NEG)
        s = (jnp.dot(kc_ref[0, 0, :nrows, :], qt_ref[...], preferred_element_type=F32)
             + jnp.concatenate([bias] * G, axis=1))
        e = jnp.exp(s - jnp.max(s, axis=0, keepdims=True))
        denom = jnp.sum(e, axis=0, keepdims=True)
        any_valid = jnp.concatenate([tok[0:1, :] >= CMP_BLOCK - 1] * G, axis=1)
        p = e * jnp.where(any_valid, 1.0 / denom, 0.0)
        out_t = jnp.dot(vct_ref[0, 0, :, :nrows], p.astype(MXU_DTYPE), preferred_element_type=F32)
        imp = None
        for g in range(G):
            o_ref[0, :, g * d:(g + 1) * d] = out_t[:, g * tq:(g + 1) * tq].T
            pg = p[:, g * tq:(g + 1) * tq]
            imp = pg if imp is None else imp + pg

        def phase(r):
            return jnp.concatenate([imp[q * span_rows + r * qsz:q * span_rows + (r + 1) * qsz, :]
                                    for q in range(n_spans)], axis=0)

        j = lax.broadcasted_iota(jnp.int32, (nj, tq), 0)
        p_slc = jnp.where(j == 0, 0.0, pltpu.roll(phase(SLC_RATIO - 1), 1, 0))
        for r in range(SLC_RATIO):
            p_slc = p_slc + phase(r)
        cur = (t0 + lax.broadcasted_iota(jnp.int32, (nj, tq), 1)) >> (SLC_BLOCK.bit_length() - 1)
        causal = j <= cur
        if nj < nsel:
            sel_ref[0, 0, nj:, :] = jnp.full((nsel - nj, tq), MASK_BIAS, sel_ref.dtype)

        @pl.when(t0 < top_k * SLC_BLOCK)
        def _():
            sel_ref[0, 0, :nj, :] = jnp.where(causal, 0.0, MASK_BIAS).astype(sel_ref.dtype)

        @pl.when(t0 >= top_k * SLC_BLOCK)
        def _():
            forced = (j == 0) | (j == cur) | (j == cur - 1)
            score = jnp.where(causal & jnp.logical_not(forced), p_slc, PICKED)
            jf = j.astype(F32)
            for _ in range(top_k - 3):
                m = jnp.max(score, axis=0, keepdims=True)
                first = jnp.min(jnp.where(score == m, jf, float(nsel)), axis=0, keepdims=True)
                score = jnp.where(jf == first, PICKED, score)
            sel_ref[0, 0, :nj, :] = jnp.where((score == PICKED) & causal, 0.0, MASK_BIAS).astype(sel_ref.dtype)

    last_span = (t0 + tq - 1) // (qsz * SLC_BLOCK)
    for n_spans in range(1, CMP_SPANS + 1):
        pl.when(last_span == n_spans - 1)(functools.partial(attend, n_spans))


def nsa_cmp_branch(qt, B, kc, vct, *, tq=512):
    S = qt.shape[2] // B
    Hkv, G, d = NSA_KV_HEADS, NSA_GROUP, NSA_HEAD_DIM
    ncp = kc.shape[2]
    nsel = ncp // SLC_RATIO
    top_k = min(SLC_TOPK, nsel)
    tq = min(tq, S)
    assert S % tq == 0 and nsel & (nsel - 1) == 0 and tq & (tq - 1) == 0
    assert (top_k * SLC_BLOCK) % tq == 0 and top_k >= 3
    assert nsel % (CMP_SPANS * BF16_SUBLANES) == 0 and (SLC_RATIO * nsel // CMP_SPANS) % LANE == 0
    assert (nsel // CMP_SPANS * SLC_BLOCK) % tq == 0
    kern = functools.partial(_nsa_cmp_kernel, top_k=top_k)
    return pl.pallas_call(
        kern,
        grid=(B, Hkv, S // tq),
        in_specs=[
            pl.BlockSpec((G, d, tq), lambda b, h, i: (h, 0, b * (S // tq) + i)),
            pl.BlockSpec((1, 1, ncp, d), lambda b, h, i: (b, h, 0, 0)),
            pl.BlockSpec((1, 1, d, ncp), lambda b, h, i: (b, h, 0, 0)),
        ],
        out_specs=[
            pl.BlockSpec((1, tq, G * d), lambda b, h, i: (b, i, h)),
            pl.BlockSpec((1, 1, nsel, tq), lambda b, h, i: (b, h, 0, i)),
        ],
        out_shape=[
            jax.ShapeDtypeStruct((B, S, Hkv * G * d), F32),
            jax.ShapeDtypeStruct((B, Hkv, nsel, S), MXU_DTYPE),
        ],
        scratch_shapes=[pltpu.VMEM((d, G * tq), MXU_DTYPE)],
        compiler_params=_cparams("arbitrary", "arbitrary", "arbitrary"),
        name="nsa_cmp",
    )(qt, kc, vct)


def _nsa_slc_kernel(q_ref, sel_ref, k_ref, vt_ref, o_ref, qa_ref, m_ref, acc_ref, s_ref, smax_ref, *, tk):
    i = pl.program_id(2)
    tq = q_ref.shape[2]
    d = NSA_HEAD_DIM
    G = NSA_GROUP
    nblk = sel_ref.shape[2]
    span = LANE * SLC_BLOCK
    n_halves = pl.cdiv(nblk, LANE)

    m_ref[...] = jnp.full_like(m_ref, PICKED)
    acc_ref[...] = jnp.zeros_like(acc_ref)
    for g in range(G):
        qa_ref[:d, g * tq:(g + 1) * tq] = q_ref[g]

    def scores(kt_idx):
        k0 = pl.multiple_of(kt_idx * tk, tk)
        return jnp.dot(k_ref[0, pl.ds(k0, tk), :], qa_ref[...], preferred_element_type=F32)

    def put_scores(slot, kt_idx):
        s = scores(kt_idx)
        s_ref[slot] = s
        smax_ref[slot] = jnp.max(s, axis=0, keepdims=True)

    def accumulate(kt_idx, s, smax):
        k0 = pl.multiple_of(kt_idx * tk, tk)
        m_old = m_ref[...]
        m_new = jnp.maximum(m_old, smax)
        alpha = jnp.exp(m_old - m_new)
        p = jnp.exp(s - m_new)
        acc_ref[...] = alpha * acc_ref[...] + jnp.dot(vt_ref[0, :, pl.ds(k0, tk)], p.astype(MXU_DTYPE),
                                                      preferred_element_type=F32)
        m_ref[...] = m_new

    last = ((i + 1) * tq - 1) // tk
    tiles_per_half = span // tk
    for hf in range(n_halves):
        lo = hf * tiles_per_half
        hi = jnp.minimum(last, lo + tiles_per_half)

        @pl.when(lo <= last)
        def _():
            sel_half = sel_ref[0, 0, hf * LANE:(hf + 1) * LANE, :]
            for g in range(G):
                qa_ref[d:, g * tq:(g + 1) * tq] = sel_half
            n = hi - lo
            odd = (n & 1) == 1
            diag_here = last < lo + tiles_per_half

            def accumulate_diagonal(s):
                col = lax.broadcasted_iota(jnp.int32, s.shape, 1)
                tok = i * tq + (col & (tq - 1))
                key = last * tk + lax.broadcasted_iota(jnp.int32, s.shape, 0)
                masked = jnp.where(key <= tok, s, NEG)
                accumulate(last, masked, jnp.max(masked, axis=0, keepdims=True))

            put_scores(0, lo)

            def pair(pi, carry):
                t = lo + 2 * pi
                put_scores(1, t + 1)
                accumulate(t, s_ref[0], smax_ref[0])
                put_scores(0, t + 2)
                accumulate(t + 1, s_ref[1], smax_ref[1])
                return carry

            lax.fori_loop(0, n >> 1, pair, 0)

            @pl.when(odd & diag_here)
            def _():
                put_scores(1, last)
                accumulate(hi - 1, s_ref[0], smax_ref[0])
                accumulate_diagonal(s_ref[1])

            @pl.when(odd & jnp.logical_not(diag_here))
            def _():
                accumulate(hi - 1, s_ref[0], smax_ref[0])

            @pl.when(jnp.logical_not(odd) & diag_here)
            def _():
                accumulate_diagonal(s_ref[0])

    out = acc_ref[:d, :] * (1.0 / acc_ref[d:d + 1, :])
    for g in range(G):
        o_ref[0, :, g * d:(g + 1) * d] = out[:, g * tq:(g + 1) * tq].T


def nsa_slc_branch(qt, B, sel, k_aug, vt, *, tq=512, tk=1024):
    S = qt.shape[2] // B
    Hkv, G, d = NSA_KV_HEADS, NSA_GROUP, NSA_HEAD_DIM
    nblk = sel.shape[2]
    dv = vt.shape[1]
    tq, tk = min(tq, S), min(tk, S)
    assert S % tq == 0 and S % tk == 0 and tk % tq == 0 and tq & (tq - 1) == 0
    assert nblk % LANE == 0 and (LANE * SLC_BLOCK) % tk == 0
    kern = functools.partial(_nsa_slc_kernel, tk=tk)
    return pl.pallas_call(
        kern,
        grid=(B, Hkv, S // tq),
        in_specs=[
            pl.BlockSpec((G, d, tq), lambda b, h, i: (h, 0, b * (S // tq) + i)),
            pl.BlockSpec((1, 1, nblk, tq), lambda b, h, i: (b, h, 0, i)),
            pl.BlockSpec((1, S, d + LANE), lambda b, h, i: (h, b, 0), pipeline_mode=pl.Buffered(1)),
            pl.BlockSpec((1, dv, S), lambda b, h, i: (h, 0, b), pipeline_mode=pl.Buffered(1)),
        ],
        out_specs=pl.BlockSpec((1, tq, G * d), lambda b, h, i: (b, i, h)),
        out_shape=jax.ShapeDtypeStruct((B, S, Hkv * G * d), F32),
        scratch_shapes=[
            pltpu.VMEM((d + LANE, G * tq), MXU_DTYPE),
            pltpu.VMEM((1, G * tq), F32),
            pltpu.VMEM((dv, G * tq), F32),
            pltpu.VMEM((2, tk, G * tq), F32),
            pltpu.VMEM((2, 1, G * tq), F32),
        ],
        compiler_params=_cparams("arbitrary", "arbitrary", "arbitrary"),
        name="nsa_slc",
    )(qt, sel, k_aug, vt)


def _nsa_win_kernel(q_ref, k_ref, vt_ref, bias_ref, o_ref, qt_ref):
    i = pl.program_id(2)
    tq = q_ref.shape[2]
    d = NSA_HEAD_DIM
    G = NSA_GROUP
    nk = WIN + tq
    k0 = pl.multiple_of(jnp.maximum(i * tq - WIN, 0), tq)
    for g in range(G):
        qt_ref[:, g * tq:(g + 1) * tq] = q_ref[g]
    s = (jnp.dot(k_ref[0, pl.ds(k0, nk), :], qt_ref[...], preferred_element_type=F32)
         + jnp.concatenate([bias_ref[0]] * G, axis=1))
    e = jnp.exp(s - jnp.max(s, axis=0, keepdims=True))
    p = e * (1.0 / jnp.sum(e, axis=0, keepdims=True))
    out_t = jnp.dot(vt_ref[0, :d, pl.ds(k0, nk)], p.astype(MXU_DTYPE), preferred_element_type=F32)
    for g in range(G):
        o_ref[0, :, g * d:(g + 1) * d] = out_t[:, g * tq:(g + 1) * tq].T


def nsa_win_branch(qt, B, k_win, vt, *, tq=256):
    S = qt.shape[2] // B
    Hkv, G, d = NSA_KV_HEADS, NSA_GROUP, NSA_HEAD_DIM
    dv = vt.shape[1]
    tq = min(tq, S)
    assert S % tq == 0 and tq % LANE == 0 and tq & (tq - 1) == 0 and WIN % tq == 0 and S >= WIN + tq
    kern = _nsa_win_kernel
    n_clipped = WIN // tq
    t0 = jnp.minimum(jnp.arange(n_clipped + 1) * tq, WIN)[:, None, None]
    pos = (t0 - jnp.minimum(t0, WIN)) + jnp.arange(WIN + tq)[None, :, None]
    tok = t0 + jnp.arange(tq)[None, None, :]
    bias = jnp.where((pos <= tok) & (pos > tok - WIN), 0.0, NEG).astype(F32)
    return pl.pallas_call(
        kern,
        grid=(B, Hkv, S // tq),
        in_specs=[
            pl.BlockSpec((G, d, tq), lambda b, h, i: (h, 0, b * (S // tq) + i)),
            pl.BlockSpec((1, S, d), lambda b, h, i: (h, b, 0)),
            pl.BlockSpec((1, dv, S), lambda b, h, i: (Hkv + h, 0, b)),
            pl.BlockSpec((1, WIN + tq, tq), lambda b, h, i: (jnp.minimum(i, n_clipped), 0, 0)),
        ],
        out_specs=pl.BlockSpec((1, tq, G * d), lambda b, h, i: (b, i, h)),
        out_shape=jax.ShapeDtypeStruct((B, S, Hkv * G * d), F32),
        scratch_shapes=[pltpu.VMEM((d, G * tq), MXU_DTYPE)],
        compiler_params=_cparams("arbitrary", "arbitrary", "arbitrary"),
        name="nsa_win",
    )(qt, k_win, vt, bias)


def _nsa_out_kernel(oc_ref, os_ref, ow_ref, gate_ref, w_ref, h_ref, o_ref, merged_ref):
    d = NSA_HEAD_DIM
    gates = _sigmoid(gate_ref[...])
    for hq in range(NSA_Q_HEADS):
        cols = slice(hq * d, (hq + 1) * d)
        c = N_BRANCH * hq
        merged = (gates[:, c:c + 1] * oc_ref[:, cols] + gates[:, c + 1:c + 2] * os_ref[:, cols]
                  + gates[:, c + 2:c + 3] * ow_ref[:, cols])
        merged_ref[:, cols] = merged.astype(merged_ref.dtype)
    o_ref[...] = h_ref[...] + jnp.dot(merged_ref[...], w_ref[...], preferred_element_type=F32)


def nsa_out(o_cmp, o_slc, o_win, gate_logits, w_o, h, *, tm=256):
    T, HD = o_cmp.shape
    D = w_o.shape[1]
    tm = min(tm, T)
    assert T % tm == 0 and gate_logits.shape == (T, LANE)
    row = lambda i: (i, 0)
    return pl.pallas_call(
        _nsa_out_kernel,
        grid=(T // tm,),
        in_specs=[
            pl.BlockSpec((tm, HD), row),
            pl.BlockSpec((tm, HD), row),
            pl.BlockSpec((tm, HD), row),
            pl.BlockSpec((tm, LANE), row),
            pl.BlockSpec((HD, D), lambda i: (0, 0)),
            pl.BlockSpec((tm, D), row),
        ],
        out_specs=pl.BlockSpec((tm, D), row),
        out_shape=jax.ShapeDtypeStruct((T, D), F32),
        scratch_shapes=[pltpu.VMEM((tm, HD), w_o.dtype)],
        compiler_params=_cparams("arbitrary"),
        name="nsa_out",
    )(o_cmp, o_slc, o_win, gate_logits, w_o, h)


def _span_major(x, nsel):
    B, n, H, d = x.shape
    qsz = nsel // CMP_SPANS
    return x.reshape(B, CMP_SPANS, qsz, SLC_RATIO, H, d).transpose(0, 4, 1, 3, 2, 5).reshape(B, H, n, d)


def _cast(w):
    return w.astype(MXU_DTYPE)


def nsa_shared_kv(h, B, S, kv_gain, w_kv, cmp_pe_k, cmp_w1_k, cmp_w2_k, cmp_pe_v, cmp_w1_v, cmp_w2_v):
    Hkv, d = NSA_KV_HEADS, NSA_HEAD_DIM
    cmp_in, k_slc, k_win, vt = kv_projection(h, S, kv_gain, _cast(w_kv))
    n = S // CMP_STRIDE
    nsel = S // SLC_BLOCK
    cmp = compress_tokens(cmp_in, B, jnp.stack([cmp_pe_k, cmp_pe_v]), _cast(jnp.stack([cmp_w1_k, cmp_w1_v])),
                          _cast(jnp.stack([cmp_w2_k, cmp_w2_v])))
    cmp = cmp.reshape(2, B, Hkv, n, d).transpose(0, 1, 3, 2, 4)
    kc = _cast(_span_major(cmp[0], nsel))
    vct = _cast(_span_major(cmp[1], nsel).transpose(0, 1, 3, 2))
    return kc, vct, k_slc, k_win, vt


def nsa_attention_block(h, B, S, mix_gain, w_q, w_o, shared):
    kc, vct, k_slc, k_win, vt = shared
    T = B * S
    nq = NSA_Q_HEADS * NSA_HEAD_DIM
    nqp = nq + LANE
    w_q = jnp.pad(w_q, ((0, 0), (0, nqp - w_q.shape[1])))
    qt, gate_logits = q_projection(h, mix_gain, _cast(w_q))
    o_cmp, sel = nsa_cmp_branch(qt, B, kc, vct)
    o_slc = nsa_slc_branch(qt, B, sel, k_slc, vt)
    o_win = nsa_win_branch(qt, B, k_win, vt)
    return nsa_out(o_cmp.reshape(T, nq), o_slc.reshape(T, nq), o_win.reshape(T, nq), gate_logits, _cast(w_o), h)


def nsa_mixer(h, B, S, mix_gain, kv_gain, w_kv, cmp_pe_k, cmp_w1_k, cmp_w2_k, cmp_pe_v, cmp_w1_v, cmp_w2_v, w_q, w_o):
    shared = nsa_shared_kv(h, B, S, kv_gain, w_kv, cmp_pe_k, cmp_w1_k, cmp_w2_k, cmp_pe_v, cmp_w1_v, cmp_w2_v)
    return nsa_attention_block(h, B, S, mix_gain, w_q, w_o, shared)


def kernel(x, norm_mix_gain, norm_ffn_gain, ret_w_in, ret_gn_gain, ret_w_out, nsa_kv_norm_gain, nsa_w_kv, cmp_pe_k, cmp_w1_k, cmp_w2_k, cmp_pe_v, cmp_w1_v, cmp_w2_v, nsa_w_q, nsa_w_o, ffn_w_in, ffn_conv_w, ffn_conv_b, ffn_w_out, final_norm_gain):
    B, S, D = x.shape
    T = B * S
    depth = norm_mix_gain.shape[0]
    n_a = depth // 2
    h = x.reshape(T, D)
    shared = None
    for layer in range(depth):
        if layer < n_a:
            proj = norm_matmul(h, norm_mix_gain[layer], _cast(ret_w_in[layer]), tm=1024, tn=1024)
            y = retention_core(proj.reshape(B, S, -1), ret_gn_gain[layer])
            h = matmul_residual(y.reshape(T, -1), _cast(ret_w_out[layer]), h, tm=512, tn=1024)
        else:
            if layer == n_a:
                shared = nsa_shared_kv(h, B, S, nsa_kv_norm_gain, nsa_w_kv, cmp_pe_k, cmp_w1_k, cmp_w2_k,
                                       cmp_pe_v, cmp_w1_v, cmp_w2_v)
            b = layer - n_a
            h = nsa_attention_block(h, B, S, norm_mix_gain[layer], nsa_w_q[b], nsa_w_o[b], shared)
        h = conv_ffn_block(h, S, norm_ffn_gain[layer], _cast(ffn_w_in[layer]), ffn_conv_w[layer], ffn_conv_b[layer],
                           _cast(ffn_w_out[layer]), final_norm_gain, final_norm=(layer == depth - 1))
    return h.reshape(B, S, D)
```

```python
import functools

import jax
import jax.numpy as jnp
from jax import lax
from jax.experimental import pallas as pl
from jax.experimental.pallas import tpu as pltpu

F32 = jnp.float32
MXU_DTYPE = jnp.bfloat16

RMS_EPS = 1e-6
ROPE_BASE = 10000.0

RET_HEADS = 8
RET_CHUNK = 128

NSA_Q_HEADS = 16
NSA_KV_HEADS = 4
NSA_GROUP = NSA_Q_HEADS // NSA_KV_HEADS
NSA_HEAD_DIM = 128
N_BRANCH = 3
CMP_BLOCK = 32
CMP_STRIDE = 16
CMP_RATIO = CMP_BLOCK // CMP_STRIDE
SLC_BLOCK = 64
SLC_TOPK = 16
SLC_RATIO = SLC_BLOCK // CMP_STRIDE
CMP_SPANS = 8
WIN = 512
CONV_W = 3

NEG = -1e30
MASK_BIAS = -(2.0 ** 100)
PICKED = -3.0e38
SEL_FORCE = 1e30

LANE = 128
BF16_SUBLANES = 16
VMEM_LIMIT_BYTES = 56 * 1024 * 1024


def _cparams(*sem):
    return pltpu.CompilerParams(dimension_semantics=sem, vmem_limit_bytes=VMEM_LIMIT_BYTES)


def _sigmoid(x):
    return 1.0 / (1.0 + jnp.exp(-x))


def _rms_normalize(x, gain):
    ms = jnp.mean(x * x, axis=-1, keepdims=True)
    return x * lax.rsqrt(ms + RMS_EPS) * gain


def _norm_matmul_kernel(x_ref, g_ref, w_ref, o_ref, xn_ref):
    @pl.when(pl.program_id(1) == 0)
    def _():
        xn_ref[...] = _rms_normalize(x_ref[...], g_ref[...]).astype(xn_ref.dtype)

    o_ref[...] = jnp.dot(xn_ref[...], w_ref[...], preferred_element_type=F32).astype(o_ref.dtype)


def norm_matmul(h, gain, w, *, tm, tn, out_dtype=F32):
    T, D = h.shape
    N = w.shape[1]
    tm, tn = min(tm, T), min(tn, N)
    assert T % tm == 0 and N % tn == 0
    return pl.pallas_call(
        _norm_matmul_kernel,
        grid=(T // tm, N // tn),
        in_specs=[
            pl.BlockSpec((tm, D), lambda i, j: (i, 0)),
            pl.BlockSpec((1, D), lambda i, j: (0, 0)),
            pl.BlockSpec((D, tn), lambda i, j: (0, j)),
        ],
        out_specs=pl.BlockSpec((tm, tn), lambda i, j: (i, j)),
        out_shape=jax.ShapeDtypeStruct((T, N), out_dtype),
        scratch_shapes=[pltpu.VMEM((tm, D), w.dtype)],
        compiler_params=_cparams("arbitrary", "arbitrary"),
        name="norm_matmul",
    )(h, gain.reshape(1, D), w)


def _matmul_residual_kernel(y_ref, w_ref, h_ref, o_ref):
    o_ref[...] = h_ref[...] + jnp.dot(y_ref[...], w_ref[...], preferred_element_type=F32)


def matmul_residual(y, w, h, *, tm, tn):
    T, K = y.shape
    N = w.shape[1]
    tm, tn = min(tm, T), min(tn, N)
    assert T % tm == 0 and N % tn == 0
    return pl.pallas_call(
        _matmul_residual_kernel,
        grid=(T // tm, N // tn),
        in_specs=[
            pl.BlockSpec((tm, K), lambda i, j: (i, 0)),
            pl.BlockSpec((K, tn), lambda i, j: (0, j)),
            pl.BlockSpec((tm, tn), lambda i, j: (i, j)),
        ],
        out_specs=pl.BlockSpec((tm, tn), lambda i, j: (i, j)),
        out_shape=jax.ShapeDtypeStruct((T, N), F32),
        compiler_params=_cparams("arbitrary", "arbitrary"),
        name="matmul_residual",
    )(y, w, h)


def _retention_kernel(q_ref, k_ref, v_ref, g_ref, cos_ref, sin_ref, dmask_ref, qdec_ref, kdec_ref, gn_ref,
                      o_ref, state_ref, *, chunk, n_chunks, k_scale):
    @pl.when(pl.program_id(2) == 0)
    def _():
        state_ref[...] = jnp.zeros_like(state_ref)

    half = q_ref.shape[-1] // 2
    dmask = dmask_ref[0]
    qdec = qdec_ref[0]
    kdec = kdec_ref[0]
    chunk_decay = qdec[chunk - 1:chunk, :]
    gn = gn_ref[...]

    def rotate(x, cos, sin):
        x1, x2 = x[:, :half], x[:, half:]
        return jnp.concatenate([x1 * cos - x2 * sin, x1 * sin + x2 * cos], axis=-1)

    for ci in range(n_chunks):
        rows = pl.ds(ci * chunk, chunk)
        cos, sin = cos_ref[rows, :], sin_ref[rows, :]
        q = rotate(q_ref[0, rows, :], cos, sin)
        k = rotate(k_ref[0, rows, :], cos, sin) * k_scale
        v = v_ref[0, rows, :].astype(MXU_DTYPE)
        qm = q.astype(MXU_DTYPE)
        scores = lax.dot_general(qm, k.astype(MXU_DTYPE), (((1,), (1,)), ((), ())),
                                 preferred_element_type=F32) * dmask
        inner = jnp.dot(scores.astype(MXU_DTYPE), v, preferred_element_type=F32)
        state = state_ref[...]
        cross = jnp.dot(qm, state.astype(MXU_DTYPE), preferred_element_type=F32) * qdec
        k_dec_t = (k * kdec).T.astype(MXU_DTYPE)
        state_ref[...] = state * chunk_decay + jnp.dot(k_dec_t, v, preferred_element_type=F32)
        out = inner + cross
        mu = jnp.mean(out, axis=-1, keepdims=True)
        cen = out - mu
        var = jnp.mean(cen * cen, axis=-1, keepdims=True)
        normed = cen * lax.rsqrt(var + RMS_EPS) * gn
        g = g_ref[0, rows, :]
        o_ref[0, rows, :] = ((g * _sigmoid(g)) * normed).astype(o_ref.dtype)


def retention_core(proj, gn_gain, *, chunk=2 * RET_CHUNK, tokens_per_step=1024):
    B, S, P = proj.shape
    H = RET_HEADS
    dk = P // (6 * H)
    dv = 2 * dk
    tc = min(tokens_per_step, S)
    assert S % tc == 0 and tc % chunk == 0
    half = dk // 2
    pos = jnp.arange(S, dtype=F32)
    freqs = ROPE_BASE ** (-jnp.arange(half, dtype=F32) / half)
    ang = pos[:, None] * freqs[None, :]
    cos, sin = jnp.cos(ang), jnp.sin(ang)
    log_gamma = jnp.log(1.0 - 2.0 ** (-5.0 - jnp.arange(H, dtype=F32)))
    idx = jnp.arange(chunk, dtype=F32)
    diff = idx[:, None] - idx[None, :]
    dmask = jnp.where(diff >= 0, jnp.exp(jnp.maximum(diff, 0.0)[None] * log_gamma[:, None, None]), 0.0)
    qdec = jnp.exp((idx[None, :] + 1.0) * log_gamma[:, None])[:, :, None]
    kdec = jnp.exp((chunk - 1.0 - idx[None, :]) * log_gamma[:, None])[:, :, None]
    kern = functools.partial(_retention_kernel, chunk=chunk, n_chunks=tc // chunk, k_scale=dk ** -0.5)
    k_off, v_off, g_off = (H * dk) // dk, (2 * H * dk) // dv, (2 * H * dk + H * dv) // dv
    return pl.pallas_call(
        kern,
        grid=(B, H, S // tc),
        in_specs=[
            pl.BlockSpec((1, tc, dk), lambda b, h, c: (b, c, h)),
            pl.BlockSpec((1, tc, dk), lambda b, h, c: (b, c, k_off + h)),
            pl.BlockSpec((1, tc, dv), lambda b, h, c: (b, c, v_off + h)),
            pl.BlockSpec((1, tc, dv), lambda b, h, c: (b, c, g_off + h)),
            pl.BlockSpec((tc, half), lambda b, h, c: (c, 0)),
            pl.BlockSpec((tc, half), lambda b, h, c: (c, 0)),
            pl.BlockSpec((1, chunk, chunk), lambda b, h, c: (h, 0, 0)),
            pl.BlockSpec((1, chunk, 1), lambda b, h, c: (h, 0, 0)),
            pl.BlockSpec((1, chunk, 1), lambda b, h, c: (h, 0, 0)),
            pl.BlockSpec((1, dv), lambda b, h, c: (0, h)),
        ],
        out_specs=pl.BlockSpec((1, tc, dv), lambda b, h, c: (b, c, h)),
        out_shape=jax.ShapeDtypeStruct((B, S, H * dv), MXU_DTYPE),
        scratch_shapes=[pltpu.VMEM((dk, dv), F32)],
        compiler_params=_cparams("arbitrary", "arbitrary", "arbitrary"),
        name="retention_core",
    )(proj, proj, proj, proj, cos, sin, dmask, qdec, kdec, gn_gain.reshape(1, H * dv))


def _conv_ffn_kernel(x_ref, g_ref, wa_ref, wb_ref, cwa_ref, cwb_ref, cba_ref, cbb_ref, wo_ref, fg_ref,
                     o_ref, xn_ref, acc_ref, carry_a_ref, carry_b_ref, *, tiles_per_seq, final_norm):
    i, j = pl.program_id(0), pl.program_id(1)
    tm = x_ref.shape[0]

    @pl.when((i == 0) & (j == 0))
    def _():
        carry_a_ref[...] = jnp.zeros_like(carry_a_ref)
        carry_b_ref[...] = jnp.zeros_like(carry_b_ref)

    @pl.when(j == 0)
    def _():
        xn_ref[...] = _rms_normalize(x_ref[...], g_ref[...]).astype(xn_ref.dtype)
        acc_ref[...] = jnp.zeros_like(acc_ref)

    seq_start = (i % tiles_per_seq) == 0

    def causal_conv(w_ref, cw_ref, cb_ref, carry_ref):
        u = jnp.dot(xn_ref[...], w_ref[...], preferred_element_type=F32)
        prev = jnp.where(seq_start, 0.0, carry_ref[j])
        carry_ref[j] = u[tm - 8:, :]
        row = lax.broadcasted_iota(jnp.int32, u.shape, 0)
        u1 = jnp.where(row == 0, prev[7:8, :], pltpu.roll(u, 1, 0))
        u2 = jnp.where(row == 0, prev[6:7, :], jnp.where(row == 1, prev[7:8, :], pltpu.roll(u, 2, 0)))
        cw = cw_ref[...]
        return ((cb_ref[...] + u2 * cw[0:1, :]) + u1 * cw[1:2, :]) + u * cw[2:3, :]

    a = causal_conv(wa_ref, cwa_ref, cba_ref, carry_a_ref)
    b = causal_conv(wb_ref, cwb_ref, cbb_ref, carry_b_ref)
    act = ((a * _sigmoid(a)) * b).astype(wo_ref.dtype)
    acc_ref[...] += jnp.dot(act, wo_ref[...], preferred_element_type=F32)

    @pl.when(j == pl.num_programs(1) - 1)
    def _():
        out = x_ref[...] + acc_ref[...]
        if final_norm:
            out = _rms_normalize(out, fg_ref[...])
        o_ref[...] = out


def conv_ffn_block(h, seq_len, norm_gain, w_in, conv_w, conv_b, w_out, final_gain, *, final_norm, tm=512, tn=512):
    T, D = h.shape
    F = w_out.shape[0]
    tm, tn = min(tm, seq_len), min(tn, F)
    assert seq_len % tm == 0 and F % tn == 0 and tm >= 8
    nj = F // tn
    kern = functools.partial(_conv_ffn_kernel, tiles_per_seq=seq_len // tm, final_norm=final_norm)
    return pl.pallas_call(
        kern,
        grid=(T // tm, nj),
        in_specs=[
            pl.BlockSpec((tm, D), lambda i, j: (i, 0)),
            pl.BlockSpec((1, D), lambda i, j: (0, 0)),
            pl.BlockSpec((D, tn), lambda i, j: (0, j)),
            pl.BlockSpec((D, tn), lambda i, j: (0, nj + j)),
            pl.BlockSpec((CONV_W, tn), lambda i, j: (0, j)),
            pl.BlockSpec((CONV_W, tn), lambda i, j: (0, nj + j)),
            pl.BlockSpec((1, tn), lambda i, j: (0, j)),
            pl.BlockSpec((1, tn), lambda i, j: (0, nj + j)),
            pl.BlockSpec((tn, D), lambda i, j: (j, 0)),
            pl.BlockSpec((1, D), lambda i, j: (0, 0)),
        ],
        out_specs=pl.BlockSpec((tm, D), lambda i, j: (i, 0)),
        out_shape=jax.ShapeDtypeStruct((T, D), F32),
        scratch_shapes=[
            pltpu.VMEM((tm, D), w_in.dtype),
            pltpu.VMEM((tm, D), F32),
            pltpu.VMEM((nj, 8, tn), F32),
            pltpu.VMEM((nj, 8, tn), F32),
        ],
        compiler_params=_cparams("arbitrary", "arbitrary"),
        name="conv_ffn",
    )(h, norm_gain.reshape(1, D), w_in, w_in, conv_w, conv_w, conv_b.reshape(1, 2 * F), conv_b.reshape(1, 2 * F),
      w_out, final_gain.reshape(1, D))


def _kv_proj_kernel(x_ref, g_ref, w_ref, cmp_ref, kslc_ref, kwin_ref, vt_ref, *, seq_len):
    tm = x_ref.shape[0]
    d = NSA_HEAD_DIM
    Hkv = NSA_KV_HEADS
    xn = _rms_normalize(x_ref[...], g_ref[...]).astype(w_ref.dtype)
    res = jnp.dot(xn, w_ref[...], preferred_element_type=F32)
    col = lambda branch, hh: res[:, (branch * Hkv + hh) * d:(branch * Hkv + hh + 1) * d]
    pos = (pl.program_id(0) * tm) % seq_len + lax.broadcasted_iota(jnp.int32, (tm, LANE), 0)
    blk = (pos >> (SLC_BLOCK.bit_length() - 1)) & (LANE - 1)
    onehot = jnp.where(blk == lax.broadcasted_iota(jnp.int32, (tm, LANE), 1), 1.0, 0.0).astype(kslc_ref.dtype)
    ones = jnp.ones((BF16_SUBLANES, tm), vt_ref.dtype)
    for hh in range(Hkv):
        cmp_ref[hh] = col(0, hh)
        cmp_ref[Hkv + hh] = col(1, hh)
        kslc_ref[hh, :, :d] = col(2, hh).astype(kslc_ref.dtype)
        kslc_ref[hh, :, d:] = onehot
        kwin_ref[hh] = col(4, hh).astype(kwin_ref.dtype)
        vt_ref[hh, :d, :] = col(3, hh).T.astype(vt_ref.dtype)
        vt_ref[hh, d:, :] = ones
        vt_ref[Hkv + hh, :d, :] = col(5, hh).T.astype(vt_ref.dtype)
        vt_ref[Hkv + hh, d:, :] = ones


def kv_projection(h, seq_len, gain, w_kv, *, tm=256):
    T, D = h.shape
    Hkv, d = NSA_KV_HEADS, NSA_HEAD_DIM
    N = w_kv.shape[1]
    tm = min(tm, seq_len)
    assert seq_len % tm == 0 and N == 2 * N_BRANCH * Hkv * d and tm % LANE == 0
    row3 = lambda i: (0, i, 0)
    return pl.pallas_call(
        functools.partial(_kv_proj_kernel, seq_len=seq_len),
        grid=(T // tm,),
        in_specs=[
            pl.BlockSpec((tm, D), lambda i: (i, 0)),
            pl.BlockSpec((1, D), lambda i: (0, 0)),
            pl.BlockSpec((D, N), lambda i: (0, 0)),
        ],
        out_specs=[
            pl.BlockSpec((2 * Hkv, tm, d), row3),
            pl.BlockSpec((Hkv, tm, d + LANE), row3),
            pl.BlockSpec((Hkv, tm, d), row3),
            pl.BlockSpec((2 * Hkv, d + BF16_SUBLANES, tm), lambda i: (0, 0, i)),
        ],
        out_shape=[
            jax.ShapeDtypeStruct((2 * Hkv, T, d), F32),
            jax.ShapeDtypeStruct((Hkv, T, d + LANE), MXU_DTYPE),
            jax.ShapeDtypeStruct((Hkv, T, d), MXU_DTYPE),
            jax.ShapeDtypeStruct((2 * Hkv, d + BF16_SUBLANES, T), MXU_DTYPE),
        ],
        compiler_params=_cparams("arbitrary"),
        name="kv_projection",
    )(h, gain.reshape(1, D), w_kv)


def _q_proj_kernel(x_ref, g_ref, w_ref, qt_ref, gate_ref, *, q_scale):
    d = NSA_HEAD_DIM
    xn = _rms_normalize(x_ref[...], g_ref[...]).astype(w_ref.dtype)
    res = jnp.dot(xn, w_ref[...], preferred_element_type=F32)
    for hq in range(NSA_Q_HEADS):
        qt_ref[hq] = (res[:, hq * d:(hq + 1) * d] * q_scale).T.astype(qt_ref.dtype)
    gate_ref[...] = res[:, NSA_Q_HEADS * d:]


def q_projection(h, gain, w_q, *, tm=256):
    T, D = h.shape
    d = NSA_HEAD_DIM
    N = w_q.shape[1]
    tm = min(tm, T)
    assert T % tm == 0 and N == NSA_Q_HEADS * d + LANE and tm % LANE == 0
    return pl.pallas_call(
        functools.partial(_q_proj_kernel, q_scale=d ** -0.5),
        grid=(T // tm,),
        in_specs=[
            pl.BlockSpec((tm, D), lambda i: (i, 0)),
            pl.BlockSpec((1, D), lambda i: (0, 0)),
            pl.BlockSpec((D, N), lambda i: (0, 0)),
        ],
        out_specs=[
            pl.BlockSpec((NSA_Q_HEADS, d, tm), lambda i: (0, 0, i)),
            pl.BlockSpec((tm, LANE), lambda i: (i, 0)),
        ],
        out_shape=[
            jax.ShapeDtypeStruct((NSA_Q_HEADS, d, T), MXU_DTYPE),
            jax.ShapeDtypeStruct((T, LANE), F32),
        ],
        compiler_params=_cparams("arbitrary"),
        name="q_projection",
    )(h, gain.reshape(1, D), w_q)


def _compress_kernel(x_ref, pe_ref, w1_ref, w2_ref, o_ref):
    d = x_ref.shape[2]
    n = o_ref.shape[2]
    first = jnp.zeros((n, w1_ref.shape[2]), F32)
    second = jnp.zeros((n, w1_ref.shape[2]), F32)
    for p in range(CMP_STRIDE):
        xp = x_ref[0, pl.ds(p, n, stride=CMP_STRIDE), :]
        q = CMP_STRIDE + p
        first = first + jnp.dot((xp + pe_ref[0, p:p + 1, :]).astype(MXU_DTYPE), w1_ref[0, p * d:(p + 1) * d, :],
                                preferred_element_type=F32)
        second = second + jnp.dot((xp + pe_ref[0, q:q + 1, :]).astype(MXU_DTYPE), w1_ref[0, q * d:(q + 1) * d, :],
                                  preferred_element_type=F32)
    hid = first + pltpu.roll(second, n - 1, 0)
    gelu = 0.5 * hid * (1.0 + jnp.tanh(0.7978845608028654 * (hid + 0.044715 * (hid * hid * hid))))
    out = jnp.dot(gelu.astype(MXU_DTYPE), w2_ref[0], preferred_element_type=F32)
    row = lax.broadcasted_iota(jnp.int32, out.shape, 0)
    o_ref[0, 0] = jnp.where(row < n - 1, out, 0.0)


def compress_tokens(x, batch, pe, w1, w2):
    H2, T, d = x.shape
    Hkv = H2 // 2
    S = T // batch
    n = S // CMP_STRIDE
    hid = w2.shape[1]
    return pl.pallas_call(
        _compress_kernel,
        grid=(2, batch, Hkv),
        in_specs=[
            pl.BlockSpec((1, S, d), lambda s, b, hh: (s * Hkv + hh, b, 0)),
            pl.BlockSpec((1, CMP_BLOCK, d), lambda s, b, hh: (s, 0, 0)),
            pl.BlockSpec((1, CMP_BLOCK * d, hid), lambda s, b, hh: (s, 0, 0)),
            pl.BlockSpec((1, hid, d), lambda s, b, hh: (s, 0, 0)),
        ],
        out_specs=pl.BlockSpec((1, 1, n, d), lambda s, b, hh: (s, b * Hkv + hh, 0, 0)),
        out_shape=jax.ShapeDtypeStruct((2, batch * Hkv, n, d), F32),
        compiler_params=_cparams("arbitrary", "arbitrary", "arbitrary"),
        name="compress_tokens",
    )(x, pe, w1, w2)


def _nsa_cmp_kernel(q_ref, kc_ref, vct_ref, o_ref, sel_ref, qt_ref, *, top_k):
    i = pl.program_id(2)
    tq = q_ref.shape[2]
    d = NSA_HEAD_DIM
    G = NSA_GROUP
    nsel = kc_ref.shape[2] // SLC_RATIO
    qsz = nsel // CMP_SPANS
    span_rows = SLC_RATIO * qsz
    t0 = i * tq
    for g in range(G):
        qt_ref[:, g * tq:(g + 1) * tq] = q_ref[g]

    def attend(n_spans):
        nrows, nj = n_spans * span_rows, n_spans * qsz
        row = lax.broadcasted_iota(jnp.int32, (nrows, tq), 0)
        tok = t0 + lax.broadcasted_iota(jnp.int32, (nrows, tq), 1)
        blk = (row >> (span_rows.bit_length() - 1)) * qsz + (row & (qsz - 1))
        cblk = SLC_RATIO * blk + ((row >> (qsz.bit_length() - 1)) & (SLC_RATIO - 1))
        bias = jnp.where(cblk * CMP_STRIDE + (CMP_BLOCK - 1) <= tok, 0.0, NEG)
        s = (jnp.dot(kc_ref[0, 0, :nrows, :], qt_ref[...], preferred_element_type=F32)
             + jnp.concatenate([bias] * G, axis=1))
        e = jnp.exp(s - jnp.max(s, axis=0, keepdims=True))
        denom = jnp.sum(e, axis=0, keepdims=True)
        any_valid = jnp.concatenate([tok[0:1, :] >= CMP_BLOCK - 1] * G, axis=1)
        p = e * jnp.where(any_valid, 1.0 / denom, 0.0)
        out_t = jnp.dot(vct_ref[0, 0, :, :nrows], p.astype(MXU_DTYPE), preferred_element_type=F32)
        imp = None
        for g in range(G):
            o_ref[0, :, g * d:(g + 1) * d] = out_t[:, g * tq:(g + 1) * tq].T
            pg = p[:, g * tq:(g + 1) * tq]
            imp = pg if imp is None else imp + pg

        def phase(r):
            return jnp.concatenate([imp[q * span_rows + r * qsz:q * span_rows + (r + 1) * qsz, :]
                                    for q in range(n_spans)], axis=0)

        j = lax.broadcasted_iota(jnp.int32, (nj, tq), 0)
        p_slc = jnp.where(j == 0, 0.0, pltpu.roll(phase(SLC_RATIO - 1), 1, 0))
        for r in range(SLC_RATIO):
            p_slc = p_slc + phase(r)
        cur = (t0 + lax.broadcasted_iota(jnp.int32, (nj, tq), 1)) >> (SLC_BLOCK.bit_length() - 1)
        causal = j <= cur
        if nj < nsel:
            sel_ref[0, 0, nj:, :] = jnp.full((nsel - nj, tq), MASK_BIAS, sel_ref.dtype)

        @pl.when(t0 < top_k * SLC_BLOCK)
        def _():
            sel_ref[0, 0, :nj, :] = jnp.where(causal, 0.0, MASK_BIAS).astype(sel_ref.dtype)

        @pl.when(t0 >= top_k * SLC_BLOCK)
        def _():
            forced = (j == 0) | (j == cur) | (j == cur - 1)
            score = jnp.where(causal & jnp.logical_not(forced), p_slc, PICKED)
            jf = j.astype(F32)
            for _ in range(top_k - 3):
                m = jnp.max(score, axis=0, keepdims=True)
                first = jnp.min(jnp.where(score == m, jf, float(nsel)), axis=0, keepdims=True)
                score = jnp.where(jf == first, PICKED, score)
            sel_ref[0, 0, :nj, :] = jnp.where((score == PICKED) & causal, 0.0, MASK_BIAS).astype(sel_ref.dtype)

    last_span = (t0 + tq - 1) // (qsz * SLC_BLOCK)
    for n_spans in range(1, CMP_SPANS + 1):
        pl.when(last_span == n_spans - 1)(functools.partial(attend, n_spans))


def nsa_cmp_branch(qt, B, kc, vct, *, tq=512):
    S = qt.shape[2] // B
    Hkv, G, d = NSA_KV_HEADS, NSA_GROUP, NSA_HEAD_DIM
    ncp = kc.shape[2]
    nsel = ncp // SLC_RATIO
    top_k = min(SLC_TOPK, nsel)
    tq = min(tq, S)
    assert S % tq == 0 and nsel & (nsel - 1) == 0 and tq & (tq - 1) == 0
    assert (top_k * SLC_BLOCK) % tq == 0 and top_k >= 3
    assert nsel % (CMP_SPANS * BF16_SUBLANES) == 0 and (SLC_RATIO * nsel // CMP_SPANS) % LANE == 0
    assert (nsel // CMP_SPANS * SLC_BLOCK) % tq == 0
    kern = functools.partial(_nsa_cmp_kernel, top_k=top_k)
    return pl.pallas_call(
        kern,
        grid=(B, Hkv, S // tq),
        in_specs=[
            pl.BlockSpec((G, d, tq), lambda b, h, i: (h, 0, b * (S // tq) + i)),
            pl.BlockSpec((1, 1, ncp, d), lambda b, h, i: (b, h, 0, 0)),
            pl.BlockSpec((1, 1, d, ncp), lambda b, h, i: (b, h, 0, 0)),
        ],
        out_specs=[
            pl.BlockSpec((1, tq, G * d), lambda b, h, i: (b, i, h)),
            pl.BlockSpec((1, 1, nsel, tq), lambda b, h, i: (b, h, 0, i)),
        ],
        out_shape=[
            jax.ShapeDtypeStruct((B, S, Hkv * G * d), F32),
            jax.ShapeDtypeStruct((B, Hkv, nsel, S), MXU_DTYPE),
        ],
        scratch_shapes=[pltpu.VMEM((d, G * tq), MXU_DTYPE)],
        compiler_params=_cparams("arbitrary", "arbitrary", "arbitrary"),
        name="nsa_cmp",
    )(qt, kc, vct)


def _nsa_slc_kernel(q_ref, sel_ref, k_ref, vt_ref, o_ref, qa_ref, m_ref, acc_ref, s_ref, smax_ref, *, tk):
    i = pl.program_id(2)
    tq = q_ref.shape[2]
    d = NSA_HEAD_DIM
    G = NSA_GROUP
    nblk = sel_ref.shape[2]
    span = LANE * SLC_BLOCK
    n_halves = pl.cdiv(nblk, LANE)

    m_ref[...] = jnp.full_like(m_ref, PICKED)
    acc_ref[...] = jnp.zeros_like(acc_ref)
    for g in range(G):
        qa_ref[:d, g * tq:(g + 1) * tq] = q_ref[g]

    def scores(kt_idx):
        k0 = pl.multiple_of(kt_idx * tk, tk)
        return jnp.dot(k_ref[0, pl.ds(k0, tk), :], qa_ref[...], preferred_element_type=F32)

    def put_scores(slot, kt_idx):
        s = scores(kt_idx)
        s_ref[slot] = s
        smax_ref[slot] = jnp.max(s, axis=0, keepdims=True)

    def accumulate(kt_idx, s, smax):
        k0 = pl.multiple_of(kt_idx * tk, tk)
        m_old = m_ref[...]
        m_new = jnp.maximum(m_old, smax)
        alpha = jnp.exp(m_old - m_new)
        p = jnp.exp(s - m_new)
        acc_ref[...] = alpha * acc_ref[...] + jnp.dot(vt_ref[0, :, pl.ds(k0, tk)], p.astype(MXU_DTYPE),
                                                      preferred_element_type=F32)
        m_ref[...] = m_new

    last = ((i + 1) * tq - 1) // tk
    tiles_per_half = span // tk
    for hf in range(n_halves):
        lo = hf * tiles_per_half
        hi = jnp.minimum(last, lo + tiles_per_half)

        @pl.when(lo <= last)
        def _():
            sel_half = sel_ref[0, 0, hf * LANE:(hf + 1) * LANE, :]
            for g in range(G):
                qa_ref[d:, g * tq:(g + 1) * tq] = sel_half
            n = hi - lo
            odd = (n & 1) == 1
            diag_here = last < lo + tiles_per_half

            def accumulate_diagonal(s):
                col = lax.broadcasted_iota(jnp.int32, s.shape, 1)
                tok = i * tq + (col & (tq - 1))
                key = last * tk + lax.broadcasted_iota(jnp.int32, s.shape, 0)
                masked = jnp.where(key <= tok, s, NEG)
                accumulate(last, masked, jnp.max(masked, axis=0, keepdims=True))

            put_scores(0, lo)

            def pair(pi, carry):
                t = lo + 2 * pi
                put_scores(1, t + 1)
                accumulate(t, s_ref[0], smax_ref[0])
                put_scores(0, t + 2)
                accumulate(t + 1, s_ref[1], smax_ref[1])
                return carry

            lax.fori_loop(0, n >> 1, pair, 0)

            @pl.when(odd & diag_here)
            def _():
                put_scores(1, last)
                accumulate(hi - 1, s_ref[0], smax_ref[0])
                accumulate_diagonal(s_ref[1])

            @pl.when(odd & jnp.logical_not(diag_here))
            def _():
                accumulate(hi - 1, s_ref[0], smax_ref[0])

            @pl.when(jnp.logical_not(odd) & diag_here)
            def _():
                accumulate_diagonal(s_ref[0])

    out = acc_ref[:d, :] * (1.0 / acc_ref[d:d + 1, :])
    for g in range(G):
        o_ref[0, :, g * d:(g + 1) * d] = out[:, g * tq:(g + 1) * tq].T


def nsa_slc_branch(qt, B, sel, k_aug, vt, *, tq=512, tk=1024):
    S = qt.shape[2] // B
    Hkv, G, d = NSA_KV_HEADS, NSA_GROUP, NSA_HEAD_DIM
    nblk = sel.shape[2]
    dv = vt.shape[1]
    tq, tk = min(tq, S), min(tk, S)
    assert S % tq == 0 and S % tk == 0 and tk % tq == 0 and tq & (tq - 1) == 0
    assert nblk % LANE == 0 and (LANE * SLC_BLOCK) % tk == 0
    kern = functools.partial(_nsa_slc_kernel, tk=tk)
    return pl.pallas_call(
        kern,
        grid=(B, Hkv, S // tq),
        in_specs=[
            pl.BlockSpec((G, d, tq), lambda b, h, i: (h, 0, b * (S // tq) + i)),
            pl.BlockSpec((1, 1, nblk, tq), lambda b, h, i: (b, h, 0, i)),
            pl.BlockSpec((1, S, d + LANE), lambda b, h, i: (h, b, 0), pipeline_mode=pl.Buffered(1)),
            pl.BlockSpec((1, dv, S), lambda b, h, i: (h, 0, b), pipeline_mode=pl.Buffered(1)),
        ],
        out_specs=pl.BlockSpec((1, tq, G * d), lambda b, h, i: (b, i, h)),
        out_shape=jax.ShapeDtypeStruct((B, S, Hkv * G * d), F32),
        scratch_shapes=[
            pltpu.VMEM((d + LANE, G * tq), MXU_DTYPE),
            pltpu.VMEM((1, G * tq), F32),
            pltpu.VMEM((dv, G * tq), F32),
            pltpu.VMEM((2, tk, G * tq), F32),
            pltpu.VMEM((2, 1, G * tq), F32),
        ],
        compiler_params=_cparams("arbitrary", "arbitrary", "arbitrary"),
        name="nsa_slc",
    )(qt, sel, k_aug, vt)


def _nsa_win_kernel(q_ref, k_ref, vt_ref, bias_ref, o_ref, qt_ref):
    i = pl.program_id(2)
    tq = q_ref.shape[2]
    d = NSA_HEAD_DIM
    G = NSA_GROUP
    nk = WIN + tq
    k0 = pl.multiple_of(jnp.maximum(i * tq - WIN, 0), tq)
    for g in range(G):
        qt_ref[:, g * tq:(g + 1) * tq] = q_ref[g]
    s = (jnp.dot(k_ref[0, pl.ds(k0, nk), :], qt_ref[...], preferred_element_type=F32)
         + jnp.concatenate([bias_ref[0]] * G, axis=1))
    e = jnp.exp(s - jnp.max(s, axis=0, keepdims=True))
    p = e * (1.0 / jnp.sum(e, axis=0, keepdims=True))
    out_t = jnp.dot(vt_ref[0, :d, pl.ds(k0, nk)], p.astype(MXU_DTYPE), preferred_element_type=F32)
    for g in range(G):
        o_ref[0, :, g * d:(g + 1) * d] = out_t[:, g * tq:(g + 1) * tq].T


def nsa_win_branch(qt, B, k_win, vt, *, tq=256):
    S = qt.shape[2] // B
    Hkv, G, d = NSA_KV_HEADS, NSA_GROUP, NSA_HEAD_DIM
    dv = vt.shape[1]
    tq = min(tq, S)
    assert S % tq == 0 and tq % LANE == 0 and tq & (tq - 1) == 0 and WIN % tq == 0 and S >= WIN + tq
    kern = _nsa_win_kernel
    n_clipped = WIN // tq
    t0 = jnp.minimum(jnp.arange(n_clipped + 1) * tq, WIN)[:, None, None]
    pos = (t0 - jnp.minimum(t0, WIN)) + jnp.arange(WIN + tq)[None, :, None]
    tok = t0 + jnp.arange(tq)[None, None, :]
    bias = jnp.where((pos <= tok) & (pos > tok - WIN), 0.0, NEG).astype(F32)
    return pl.pallas_call(
        kern,
        grid=(B, Hkv, S // tq),
        in_specs=[
            pl.BlockSpec((G, d, tq), lambda b, h, i: (h, 0, b * (S // tq) + i)),
            pl.BlockSpec((1, S, d), lambda b, h, i: (h, b, 0)),
            pl.BlockSpec((1, dv, S), lambda b, h, i: (Hkv + h, 0, b)),
            pl.BlockSpec((1, WIN + tq, tq), lambda b, h, i: (jnp.minimum(i, n_clipped), 0, 0)),
        ],
        out_specs=pl.BlockSpec((1, tq, G * d), lambda b, h, i: (b, i, h)),
        out_shape=jax.ShapeDtypeStruct((B, S, Hkv * G * d), F32),
        scratch_shapes=[pltpu.VMEM((d, G * tq), MXU_DTYPE)],
        compiler_params=_cparams("arbitrary", "arbitrary", "arbitrary"),
        name="nsa_win",
    )(qt, k_win, vt, bias)


def _nsa_out_kernel(oc_ref, os_ref, ow_ref, gate_ref, w_ref, h_ref, o_ref, merged_ref):
    d = NSA_HEAD_DIM
    gates = _sigmoid(gate_ref[...])
    for hq in range(NSA_Q_HEADS):
        cols = slice(hq * d, (hq + 1) * d)
        c = N_BRANCH * hq
        merged = (gates[:, c:c + 1] * oc_ref[:, cols] + gates[:, c + 1:c + 2] * os_ref[:, cols]
                  + gates[:, c + 2:c + 3] * ow_ref[:, cols])
        merged_ref[:, cols] = merged.astype(merged_ref.dtype)
    o_ref[...] = h_ref[...] + jnp.dot(merged_ref[...], w_ref[...], preferred_element_type=F32)


def nsa_out(o_cmp, o_slc, o_win, gate_logits, w_o, h, *, tm=256):
    T, HD = o_cmp.shape
    D = w_o.shape[1]
    tm = min(tm, T)
    assert T % tm == 0 and gate_logits.shape == (T, LANE)
    row = lambda i: (i, 0)
    return pl.pallas_call(
        _nsa_out_kernel,
        grid=(T // tm,),
        in_specs=[
            pl.BlockSpec((tm, HD), row),
            pl.BlockSpec((tm, HD), row),
            pl.BlockSpec((tm, HD), row),
            pl.BlockSpec((tm, LANE), row),
            pl.BlockSpec((HD, D), lambda i: (0, 0)),
            pl.BlockSpec((tm, D), row),
        ],
        out_specs=pl.BlockSpec((tm, D), row),
        out_shape=jax.ShapeDtypeStruct((T, D), F32),
        scratch_shapes=[pltpu.VMEM((tm, HD), w_o.dtype)],
        compiler_params=_cparams("arbitrary"),
        name="nsa_out",
    )(o_cmp, o_slc, o_win, gate_logits, w_o, h)


def _span_major(x, nsel):
    B, n, H, d = x.shape
    qsz = nsel // CMP_SPANS
    return x.reshape(B, CMP_SPANS, qsz, SLC_RATIO, H, d).transpose(0, 4, 1, 3, 2, 5).reshape(B, H, n, d)


def _cast(w):
    return w.astype(MXU_DTYPE)


def nsa_shared_kv(h, B, S, kv_gain, w_kv, cmp_pe_k, cmp_w1_k, cmp_w2_k, cmp_pe_v, cmp_w1_v, cmp_w2_v):
    Hkv, d = NSA_KV_HEADS, NSA_HEAD_DIM
    cmp_in, k_slc, k_win, vt = kv_projection(h, S, kv_gain, _cast(w_kv))
    n = S // CMP_STRIDE
    nsel = S // SLC_BLOCK
    cmp = compress_tokens(cmp_in, B, jnp.stack([cmp_pe_k, cmp_pe_v]), _cast(jnp.stack([cmp_w1_k, cmp_w1_v])),
                          _cast(jnp.stack([cmp_w2_k, cmp_w2_v])))
    cmp = cmp.reshape(2, B, Hkv, n, d).transpose(0, 1, 3, 2, 4)
    kc = _cast(_span_major(cmp[0], nsel))
    vct = _cast(_span_major(cmp[1], nsel).transpose(0, 1, 3, 2))
    return kc, vct, k_slc, k_win, vt


def nsa_attention_block(h, B, S, mix_gain, w_q, w_o, shared):
    kc, vct, k_slc, k_win, vt = shared
    T = B * S
    nq = NSA_Q_HEADS * NSA_HEAD_DIM
    nqp = nq + LANE
    w_q = jnp.pad(w_q, ((0, 0), (0, nqp - w_q.shape[1])))
    qt, gate_logits = q_projection(h, mix_gain, _cast(w_q))
    o_cmp, sel = nsa_cmp_branch(qt, B, kc, vct)
    o_slc = nsa_slc_branch(qt, B, sel, k_slc, vt)
    o_win = nsa_win_branch(qt, B, k_win, vt)
    return nsa_out(o_cmp.reshape(T, nq), o_slc.reshape(T, nq), o_win.reshape(T, nq), gate_logits, _cast(w_o), h)


def nsa_mixer(h, B, S, mix_gain, kv_gain, w_kv, cmp_pe_k, cmp_w1_k, cmp_w2_k, cmp_pe_v, cmp_w1_v, cmp_w2_v, w_q, w_o):
    shared = nsa_shared_kv(h, B, S, kv_gain, w_kv, cmp_pe_k, cmp_w1_k, cmp_w2_k, cmp_pe_v, cmp_w1_v, cmp_w2_v)
    return nsa_attention_block(h, B, S, mix_gain, w_q, w_o, shared)


def kernel(x, norm_mix_gain, norm_ffn_gain, ret_w_in, ret_gn_gain, ret_w_out, nsa_kv_norm_gain, nsa_w_kv, cmp_pe_k, cmp_w1_k, cmp_w2_k, cmp_pe_v, cmp_w1_v, cmp_w2_v, nsa_w_q, nsa_w_o, ffn_w_in, ffn_conv_w, ffn_conv_b, ffn_w_out, final_norm_gain):
    B, S, D = x.shape
    T = B * S
    depth = norm_mix_gain.shape[0]
    n_a = depth // 2
    h = x.reshape(T, D)
    shared = None
    for layer in range(depth):
        if layer < n_a:
            proj = norm_matmul(h, norm_mix_gain[layer], _cast(ret_w_in[layer]), tm=1024, tn=1024)
            y = retention_core(proj.reshape(B, S, -1), ret_gn_gain[layer])
            h = matmul_residual(y.reshape(T, -1), _cast(ret_w_out[layer]), h, tm=512, tn=1024)
        else:
            if layer == n_a:
                shared = nsa_shared_kv(h, B, S, nsa_kv_norm_gain, nsa_w_kv, cmp_pe_k, cmp_w1_k, cmp_w2_k,
                                       cmp_pe_v, cmp_w1_v, cmp_w2_v)
            b = layer - n_a
            h = nsa_attention_block(h, B, S, norm_mix_gain[layer], nsa_w_q[b], nsa_w_o[b], shared)
        h = conv_ffn_block(h, S, norm_ffn_gain[layer], _cast(ffn_w_in[layer]), ffn_conv_w[layer], ffn_conv_b[layer],
                           _cast(ffn_w_out[layer]), final_norm_gain, final_norm=(layer == depth - 1))
    return h.reshape(B, S, D)
```

```python
import functools

import jax
import jax.numpy as jnp
from jax import lax
from jax.experimental import pallas as pl
from jax.experimental.pallas import tpu as pltpu

F32 = jnp.float32
MXU_DTYPE = jnp.bfloat16

RMS_EPS = 1e-6
ROPE_BASE = 10000.0

RET_HEADS = 8
RET_CHUNK = 128

NSA_Q_HEADS = 16
NSA_KV_HEADS = 4
NSA_GROUP = NSA_Q_HEADS // NSA_KV_HEADS
NSA_HEAD_DIM = 128
N_BRANCH = 3
CMP_BLOCK = 32
CMP_STRIDE = 16
CMP_RATIO = CMP_BLOCK // CMP_STRIDE
SLC_BLOCK = 64
SLC_TOPK = 16
SLC_RATIO = SLC_BLOCK // CMP_STRIDE
CMP_SPANS = 8
WIN = 512
CONV_W = 3

NEG = -1e30
MASK_BIAS = -(2.0 ** 100)
PICKED = -3.0e38
SEL_FORCE = 1e30
LOG2_E = 1.4426950408889634

LANE = 128
BF16_SUBLANES = 16
VMEM_LIMIT_BYTES = 56 * 1024 * 1024


def _cparams(*sem):
    return pltpu.CompilerParams(dimension_semantics=sem, vmem_limit_bytes=VMEM_LIMIT_BYTES)


def _sigmoid(x):
    return 1.0 / (1.0 + jnp.exp(-x))


def _rms_normalize(x, gain):
    ms = jnp.mean(x * x, axis=-1, keepdims=True)
    return x * lax.rsqrt(ms + RMS_EPS) * gain


def _norm_matmul_kernel(x_ref, g_ref, w_ref, o_ref, xn_ref):
    @pl.when(pl.program_id(1) == 0)
    def _():
        xn_ref[...] = _rms_normalize(x_ref[...], g_ref[...]).astype(xn_ref.dtype)

    o_ref[...] = jnp.dot(xn_ref[...], w_ref[...], preferred_element_type=F32).astype(o_ref.dtype)


def norm_matmul(h, gain, w, *, tm, tn, out_dtype=F32):
    T, D = h.shape
    N = w.shape[1]
    tm, tn = min(tm, T), min(tn, N)
    assert T % tm == 0 and N % tn == 0
    return pl.pallas_call(
        _norm_matmul_kernel,
        grid=(T // tm, N // tn),
        in_specs=[
            pl.BlockSpec((tm, D), lambda i, j: (i, 0)),
            pl.BlockSpec((1, D), lambda i, j: (0, 0)),
            pl.BlockSpec((D, tn), lambda i, j: (0, j)),
        ],
        out_specs=pl.BlockSpec((tm, tn), lambda i, j: (i, j)),
        out_shape=jax.ShapeDtypeStruct((T, N), out_dtype),
        scratch_shapes=[pltpu.VMEM((tm, D), w.dtype)],
        compiler_params=_cparams("arbitrary", "arbitrary"),
        name="norm_matmul",
    )(h, gain.reshape(1, D), w)


def _matmul_residual_kernel(y_ref, w_ref, h_ref, o_ref):
    o_ref[...] = h_ref[...] + jnp.dot(y_ref[...], w_ref[...], preferred_element_type=F32)


def matmul_residual(y, w, h, *, tm, tn):
    T, K = y.shape
    N = w.shape[1]
    tm, tn = min(tm, T), min(tn, N)
    assert T % tm == 0 and N % tn == 0
    return pl.pallas_call(
        _matmul_residual_kernel,
        grid=(T // tm, N // tn),
        in_specs=[
            pl.BlockSpec((tm, K), lambda i, j: (i, 0)),
            pl.BlockSpec((K, tn), lambda i, j: (0, j)),
            pl.BlockSpec((tm, tn), lambda i, j: (i, j)),
        ],
        out_specs=pl.BlockSpec((tm, tn), lambda i, j: (i, j)),
        out_shape=jax.ShapeDtypeStruct((T, N), F32),
        compiler_params=_cparams("arbitrary", "arbitrary"),
        name="matmul_residual",
    )(y, w, h)


def _retention_kernel(q_ref, k_ref, v_ref, g_ref, cos_ref, sin_ref, dmask_ref, qdec_ref, kdec_ref, gn_ref,
                      o_ref, state_ref, *, chunk, n_chunks, k_scale):
    @pl.when(pl.program_id(2) == 0)
    def _():
        state_ref[...] = jnp.zeros_like(state_ref)

    half = q_ref.shape[-1] // 2
    dmask = dmask_ref[0]
    qdec = qdec_ref[0]
    kdec = kdec_ref[0]
    chunk_decay = qdec[chunk - 1:chunk, :]
    gn = gn_ref[...]

    def rotate(x, cos, sin):
        x1, x2 = x[:, :half], x[:, half:]
        return jnp.concatenate([x1 * cos - x2 * sin, x1 * sin + x2 * cos], axis=-1)

    for ci in range(n_chunks):
        rows = pl.ds(ci * chunk, chunk)
        cos, sin = cos_ref[rows, :], sin_ref[rows, :]
        q = rotate(q_ref[0, rows, :], cos, sin)
        k = rotate(k_ref[0, rows, :], cos, sin) * k_scale
        v = v_ref[0, rows, :].astype(MXU_DTYPE)
        qm = q.astype(MXU_DTYPE)
        scores = lax.dot_general(qm, k.astype(MXU_DTYPE), (((1,), (1,)), ((), ())),
                                 preferred_element_type=F32) * dmask
        inner = jnp.dot(scores.astype(MXU_DTYPE), v, preferred_element_type=F32)
        state = state_ref[...]
        cross = jnp.dot(qm, state.astype(MXU_DTYPE), preferred_element_type=F32) * qdec
        k_dec_t = (k * kdec).T.astype(MXU_DTYPE)
        state_ref[...] = state * chunk_decay + jnp.dot(k_dec_t, v, preferred_element_type=F32)
        out = inner + cross
        mu = jnp.mean(out, axis=-1, keepdims=True)
        cen = out - mu
        var = jnp.mean(cen * cen, axis=-1, keepdims=True)
        normed = cen * lax.rsqrt(var + RMS_EPS) * gn
        g = g_ref[0, rows, :]
        o_ref[0, rows, :] = ((g * _sigmoid(g)) * normed).astype(o_ref.dtype)


def retention_core(proj, gn_gain, *, chunk=2 * RET_CHUNK, tokens_per_step=1024):
    B, S, P = proj.shape
    H = RET_HEADS
    dk = P // (6 * H)
    dv = 2 * dk
    tc = min(tokens_per_step, S)
    assert S % tc == 0 and tc % chunk == 0
    half = dk // 2
    pos = jnp.arange(S, dtype=F32)
    freqs = ROPE_BASE ** (-jnp.arange(half, dtype=F32) / half)
    ang = pos[:, None] * freqs[None, :]
    cos, sin = jnp.cos(ang), jnp.sin(ang)
    log_gamma = jnp.log(1.0 - 2.0 ** (-5.0 - jnp.arange(H, dtype=F32)))
    idx = jnp.arange(chunk, dtype=F32)
    diff = idx[:, None] - idx[None, :]
    dmask = jnp.where(diff >= 0, jnp.exp(jnp.maximum(diff, 0.0)[None] * log_gamma[:, None, None]), 0.0)
    qdec = jnp.exp((idx[None, :] + 1.0) * log_gamma[:, None])[:, :, None]
    kdec = jnp.exp((chunk - 1.0 - idx[None, :]) * log_gamma[:, None])[:, :, None]
    kern = functools.partial(_retention_kernel, chunk=chunk, n_chunks=tc // chunk, k_scale=dk ** -0.5)
    k_off, v_off, g_off = (H * dk) // dk, (2 * H * dk) // dv, (2 * H * dk + H * dv) // dv
    return pl.pallas_call(
        kern,
        grid=(B, H, S // tc),
        in_specs=[
            pl.BlockSpec((1, tc, dk), lambda b, h, c: (b, c, h)),
            pl.BlockSpec((1, tc, dk), lambda b, h, c: (b, c, k_off + h)),
            pl.BlockSpec((1, tc, dv), lambda b, h, c: (b, c, v_off + h)),
            pl.BlockSpec((1, tc, dv), lambda b, h, c: (b, c, g_off + h)),
            pl.BlockSpec((tc, half), lambda b, h, c: (c, 0)),
            pl.BlockSpec((tc, half), lambda b, h, c: (c, 0)),
            pl.BlockSpec((1, chunk, chunk), lambda b, h, c: (h, 0, 0)),
            pl.BlockSpec((1, chunk, 1), lambda b, h, c: (h, 0, 0)),
            pl.BlockSpec((1, chunk, 1), lambda b, h, c: (h, 0, 0)),
            pl.BlockSpec((1, dv), lambda b, h, c: (0, h)),
        ],
        out_specs=pl.BlockSpec((1, tc, dv), lambda b, h, c: (b, c, h)),
        out_shape=jax.ShapeDtypeStruct((B, S, H * dv), MXU_DTYPE),
        scratch_shapes=[pltpu.VMEM((dk, dv), F32)],
        compiler_params=_cparams("arbitrary", "arbitrary", "arbitrary"),
        name="retention_core",
    )(proj, proj, proj, proj, cos, sin, dmask, qdec, kdec, gn_gain.reshape(1, H * dv))


def _conv_ffn_kernel(x_ref, g_ref, wa_ref, wb_ref, cwa_ref, cwb_ref, cba_ref, cbb_ref, wo_ref, fg_ref,
                     o_ref, xn_ref, acc_ref, carry_a_ref, carry_b_ref, *, tiles_per_seq, final_norm):
    i, j = pl.program_id(0), pl.program_id(1)
    tm = x_ref.shape[0]

    @pl.when((i == 0) & (j == 0))
    def _():
        carry_a_ref[...] = jnp.zeros_like(carry_a_ref)
        carry_b_ref[...] = jnp.zeros_like(carry_b_ref)

    @pl.when(j == 0)
    def _():
        xn_ref[...] = _rms_normalize(x_ref[...], g_ref[...]).astype(xn_ref.dtype)
        acc_ref[...] = jnp.zeros_like(acc_ref)

    seq_start = (i % tiles_per_seq) == 0

    def causal_conv(w_ref, cw_ref, cb_ref, carry_ref):
        u = jnp.dot(xn_ref[...], w_ref[...], preferred_element_type=F32)
        prev = jnp.where(seq_start, 0.0, carry_ref[j])
        carry_ref[j] = u[tm - 8:, :]
        row = lax.broadcasted_iota(jnp.int32, u.shape, 0)
        u1 = jnp.where(row == 0, prev[7:8, :], pltpu.roll(u, 1, 0))
        u2 = jnp.where(row == 0, prev[6:7, :], jnp.where(row == 1, prev[7:8, :], pltpu.roll(u, 2, 0)))
        cw = cw_ref[...]
        return ((cb_ref[...] + u2 * cw[0:1, :]) + u1 * cw[1:2, :]) + u * cw[2:3, :]

    a = causal_conv(wa_ref, cwa_ref, cba_ref, carry_a_ref)
    b = causal_conv(wb_ref, cwb_ref, cbb_ref, carry_b_ref)
    act = ((a * _sigmoid(a)) * b).astype(wo_ref.dtype)
    acc_ref[...] += jnp.dot(act, wo_ref[...], preferred_element_type=F32)

    @pl.when(j == pl.num_programs(1) - 1)
    def _():
        out = x_ref[...] + acc_ref[...]
        if final_norm:
            out = _rms_normalize(out, fg_ref[...])
        o_ref[...] = out


def conv_ffn_block(h, seq_len, norm_gain, w_in, conv_w, conv_b, w_out, final_gain, *, final_norm, tm=512, tn=512):
    T, D = h.shape
    F = w_out.shape[0]
    tm, tn = min(tm, seq_len), min(tn, F)
    assert seq_len % tm == 0 and F % tn == 0 and tm >= 8
    nj = F // tn
    kern = functools.partial(_conv_ffn_kernel, tiles_per_seq=seq_len // tm, final_norm=final_norm)
    return pl.pallas_call(
        kern,
        grid=(T // tm, nj),
        in_specs=[
            pl.BlockSpec((tm, D), lambda i, j: (i, 0)),
            pl.BlockSpec((1, D), lambda i, j: (0, 0)),
            pl.BlockSpec((D, tn), lambda i, j: (0, j)),
            pl.BlockSpec((D, tn), lambda i, j: (0, nj + j)),
            pl.BlockSpec((CONV_W, tn), lambda i, j: (0, j)),
            pl.BlockSpec((CONV_W, tn), lambda i, j: (0, nj + j)),
            pl.BlockSpec((1, tn), lambda i, j: (0, j)),
            pl.BlockSpec((1, tn), lambda i, j: (0, nj + j)),
            pl.BlockSpec((tn, D), lambda i, j: (j, 0)),
            pl.BlockSpec((1, D), lambda i, j: (0, 0)),
        ],
        out_specs=pl.BlockSpec((tm, D), lambda i, j: (i, 0)),
        out_shape=jax.ShapeDtypeStruct((T, D), F32),
        scratch_shapes=[
            pltpu.VMEM((tm, D), w_in.dtype),
            pltpu.VMEM((tm, D), F32),
            pltpu.VMEM((nj, 8, tn), F32),
            pltpu.VMEM((nj, 8, tn), F32),
        ],
        compiler_params=_cparams("arbitrary", "arbitrary"),
        name="conv_ffn",
    )(h, norm_gain.reshape(1, D), w_in, w_in, conv_w, conv_w, conv_b.reshape(1, 2 * F), conv_b.reshape(1, 2 * F),
      w_out, final_gain.reshape(1, D))


def _kv_proj_kernel(x_ref, g_ref, w_ref, cmp_ref, kslc_ref, kwin_ref, vt_ref, *, seq_len):
    tm = x_ref.shape[0]
    d = NSA_HEAD_DIM
    Hkv = NSA_KV_HEADS
    xn = _rms_normalize(x_ref[...], g_ref[...]).astype(w_ref.dtype)
    res = jnp.dot(xn, w_ref[...], preferred_element_type=F32)
    col = lambda branch, hh: res[:, (branch * Hkv + hh) * d:(branch * Hkv + hh + 1) * d]
    pos = (pl.program_id(0) * tm) % seq_len + lax.broadcasted_iota(jnp.int32, (tm, LANE), 0)
    blk = (pos >> (SLC_BLOCK.bit_length() - 1)) & (LANE - 1)
    onehot = jnp.where(blk == lax.broadcasted_iota(jnp.int32, (tm, LANE), 1), 1.0, 0.0).astype(kslc_ref.dtype)
    ones = jnp.ones((BF16_SUBLANES, tm), vt_ref.dtype)
    for hh in range(Hkv):
        cmp_ref[hh] = col(0, hh)
        cmp_ref[Hkv + hh] = col(1, hh)
        kslc_ref[hh, :, :d] = col(2, hh).astype(kslc_ref.dtype)
        kslc_ref[hh, :, d:] = onehot
        kwin_ref[hh] = col(4, hh).astype(kwin_ref.dtype)
        vt_ref[hh, :d, :] = col(3, hh).T.astype(vt_ref.dtype)
        vt_ref[hh, d:, :] = ones
        vt_ref[Hkv + hh, :d, :] = col(5, hh).T.astype(vt_ref.dtype)
        vt_ref[Hkv + hh, d:, :] = ones


def kv_projection(h, seq_len, gain, w_kv, *, tm=256):
    T, D = h.shape
    Hkv, d = NSA_KV_HEADS, NSA_HEAD_DIM
    N = w_kv.shape[1]
    tm = min(tm, seq_len)
    assert seq_len % tm == 0 and N == 2 * N_BRANCH * Hkv * d and tm % LANE == 0
    row3 = lambda i: (0, i, 0)
    return pl.pallas_call(
        functools.partial(_kv_proj_kernel, seq_len=seq_len),
        grid=(T // tm,),
        in_specs=[
            pl.BlockSpec((tm, D), lambda i: (i, 0)),
            pl.BlockSpec((1, D), lambda i: (0, 0)),
            pl.BlockSpec((D, N), lambda i: (0, 0)),
        ],
        out_specs=[
            pl.BlockSpec((2 * Hkv, tm, d), row3),
            pl.BlockSpec((Hkv, tm, d + LANE), row3),
            pl.BlockSpec((Hkv, tm, d), row3),
            pl.BlockSpec((2 * Hkv, d + BF16_SUBLANES, tm), lambda i: (0, 0, i)),
        ],
        out_shape=[
            jax.ShapeDtypeStruct((2 * Hkv, T, d), F32),
            jax.ShapeDtypeStruct((Hkv, T, d + LANE), MXU_DTYPE),
            jax.ShapeDtypeStruct((Hkv, T, d), MXU_DTYPE),
            jax.ShapeDtypeStruct((2 * Hkv, d + BF16_SUBLANES, T), MXU_DTYPE),
        ],
        compiler_params=_cparams("arbitrary"),
        name="kv_projection",
    )(h, gain.reshape(1, D), w_kv)


def _q_proj_kernel(x_ref, g_ref, w_ref, qt_ref, gate_ref, *, q_scale):
    d = NSA_HEAD_DIM
    xn = _rms_normalize(x_ref[...], g_ref[...]).astype(w_ref.dtype)
    res = jnp.dot(xn, w_ref[...], preferred_element_type=F32)
    for hq in range(NSA_Q_HEADS):
        qt_ref[hq] = (res[:, hq * d:(hq + 1) * d] * q_scale).T.astype(qt_ref.dtype)
    gate_ref[...] = res[:, NSA_Q_HEADS * d:]


def q_projection(h, gain, w_q, *, tm=256):
    T, D = h.shape
    d = NSA_HEAD_DIM
    N = w_q.shape[1]
    tm = min(tm, T)
    assert T % tm == 0 and N == NSA_Q_HEADS * d + LANE and tm % LANE == 0
    return pl.pallas_call(
        functools.partial(_q_proj_kernel, q_scale=d ** -0.5 * LOG2_E),
        grid=(T // tm,),
        in_specs=[
            pl.BlockSpec((tm, D), lambda i: (i, 0)),
            pl.BlockSpec((1, D), lambda i: (0, 0)),
            pl.BlockSpec((D, N), lambda i: (0, 0)),
        ],
        out_specs=[
            pl.BlockSpec((NSA_Q_HEADS, d, tm), lambda i: (0, 0, i)),
            pl.BlockSpec((tm, LANE), lambda i: (i, 0)),
        ],
        out_shape=[
            jax.ShapeDtypeStruct((NSA_Q_HEADS, d, T), MXU_DTYPE),
            jax.ShapeDtypeStruct((T, LANE), F32),
        ],
        compiler_params=_cparams("arbitrary"),
        name="q_projection",
    )(h, gain.reshape(1, D), w_q)


def _compress_kernel(x_ref, pe_ref, w1_ref, w2_ref, o_ref):
    d = x_ref.shape[2]
    n = o_ref.shape[2]
    first = jnp.zeros((n, w1_ref.shape[2]), F32)
    second = jnp.zeros((n, w1_ref.shape[2]), F32)
    for p in range(CMP_STRIDE):
        xp = x_ref[0, pl.ds(p, n, stride=CMP_STRIDE), :]
        q = CMP_STRIDE + p
        first = first + jnp.dot((xp + pe_ref[0, p:p + 1, :]).astype(MXU_DTYPE), w1_ref[0, p * d:(p + 1) * d, :],
                                preferred_element_type=F32)
        second = second + jnp.dot((xp + pe_ref[0, q:q + 1, :]).astype(MXU_DTYPE), w1_ref[0, q * d:(q + 1) * d, :],
                                  preferred_element_type=F32)
    hid = first + pltpu.roll(second, n - 1, 0)
    gelu = 0.5 * hid * (1.0 + jnp.tanh(0.7978845608028654 * (hid + 0.044715 * (hid * hid * hid))))
    out = jnp.dot(gelu.astype(MXU_DTYPE), w2_ref[0], preferred_element_type=F32)
    row = lax.broadcasted_iota(jnp.int32, out.shape, 0)
    o_ref[0, 0] = jnp.where(row < n - 1, out, 0.0)


def compress_tokens(x, batch, pe, w1, w2):
    H2, T, d = x.shape
    Hkv = H2 // 2
    S = T // batch
    n = S // CMP_STRIDE
    hid = w2.shape[1]
    return pl.pallas_call(
        _compress_kernel,
        grid=(2, batch, Hkv),
        in_specs=[
            pl.BlockSpec((1, S, d), lambda s, b, hh: (s * Hkv + hh, b, 0)),
            pl.BlockSpec((1, CMP_BLOCK, d), lambda s, b, hh: (s, 0, 0)),
            pl.BlockSpec((1, CMP_BLOCK * d, hid), lambda s, b, hh: (s, 0, 0)),
            pl.BlockSpec((1, hid, d), lambda s, b, hh: (s, 0, 0)),
        ],
        out_specs=pl.BlockSpec((1, 1, n, d), lambda s, b, hh: (s, b * Hkv + hh, 0, 0)),
        out_shape=jax.ShapeDtypeStruct((2, batch * Hkv, n, d), F32),
        compiler_params=_cparams("arbitrary", "arbitrary", "arbitrary"),
        name="compress_tokens",
    )(x, pe, w1, w2)


def _nsa_cmp_kernel(q_ref, kc_ref, vct_ref, o_ref, sel_ref, qt_ref, *, top_k):
    i = pl.program_id(2)
    tq = q_ref.shape[2]
    d = NSA_HEAD_DIM
    G = NSA_GROUP
    nsel = kc_ref.shape[2] // SLC_RATIO
    qsz = nsel // CMP_SPANS
    span_rows = SLC_RATIO * qsz
    t0 = i * tq
    for g in range(G):
        qt_ref[:, g * tq:(g + 1) * tq] = q_ref[g]

    def attend(n_spans):
        nrows, nj = n_spans * span_rows, n_spans * qsz
        row = lax.broadcasted_iota(jnp.int32, (nrows, tq), 0)
        tok = t0 + lax.broadcasted_iota(jnp.int32, (nrows, tq), 1)
        blk = (row >> (span_rows.bit_length() - 1)) * qsz + (row & (qsz - 1))
        cblk = SLC_RATIO * blk + ((row >> (qsz.bit_length() - 1)) & (SLC_RATIO - 1))
        bias = jnp.where(cblk * CMP_STRIDE + (CMP_BLOCK - 1) <= tok, 0.0, NEG)
        s = (jnp.dot(kc_ref[0, 0, :nrows, :], qt_ref[...], preferred_element_type=F32)
             + jnp.concatenate([bias] * G, axis=1))
        e = jnp.exp2(s - jnp.max(s, axis=0, keepdims=True))
        denom = jnp.sum(e, axis=0, keepdims=True)
        any_valid = jnp.concatenate([tok[0:1, :] >= CMP_BLOCK - 1] * G, axis=1)
        p = e * jnp.where(any_valid, 1.0 / denom, 0.0)
        out_t = jnp.dot(vct_ref[0, 0, :, :nrows], p.astype(MXU_DTYPE), preferred_element_type=F32)
        imp = None
        for g in range(G):
            o_ref[0, :, g * d:(g + 1) * d] = out_t[:, g * tq:(g + 1) * tq].T
            pg = p[:, g * tq:(g + 1) * tq]
            imp = pg if imp is None else imp + pg

        def phase(r):
            return jnp.concatenate([imp[q * span_rows + r * qsz:q * span_rows + (r + 1) * qsz, :]
                                    for q in range(n_spans)], axis=0)

        j = lax.broadcasted_iota(jnp.int32, (nj, tq), 0)
        p_slc = jnp.where(j == 0, 0.0, pltpu.roll(phase(SLC_RATIO - 1), 1, 0))
        for r in range(SLC_RATIO):
            p_slc = p_slc + phase(r)
        cur = (t0 + lax.broadcasted_iota(jnp.int32, (nj, tq), 1)) >> (SLC_BLOCK.bit_length() - 1)
        causal = j <= cur
        if nj < nsel:
            sel_ref[0, 0, nj:, :] = jnp.full((nsel - nj, tq), MASK_BIAS, sel_ref.dtype)

        @pl.when(t0 < top_k * SLC_BLOCK)
        def _():
            sel_ref[0, 0, :nj, :] = jnp.where(causal, 0.0, MASK_BIAS).astype(sel_ref.dtype)

        @pl.when(t0 >= top_k * SLC_BLOCK)
        def _():
            forced = (j == 0) | (j == cur) | (j == cur - 1)
            score = jnp.where(causal & jnp.logical_not(forced), p_slc, PICKED)
            jf = j.astype(F32)
            for _ in range(top_k - 3):
                m = jnp.max(score, axis=0, keepdims=True)
                first = jnp.min(jnp.where(score == m, jf, float(nsel)), axis=0, keepdims=True)
                score = jnp.where(jf == first, PICKED, score)
            sel_ref[0, 0, :nj, :] = jnp.where((score == PICKED) & causal, 0.0, MASK_BIAS).astype(sel_ref.dtype)

    last_span = (t0 + tq - 1) // (qsz * SLC_BLOCK)
    for n_spans in range(1, CMP_SPANS + 1):
        pl.when(last_span == n_spans - 1)(functools.partial(attend, n_spans))


def nsa_cmp_branch(qt, B, kc, vct, *, tq=512):
    S = qt.shape[2] // B
    Hkv, G, d = NSA_KV_HEADS, NSA_GROUP, NSA_HEAD_DIM
    ncp = kc.shape[2]
    nsel = ncp // SLC_RATIO
    top_k = min(SLC_TOPK, nsel)
    tq = min(tq, S)
    assert S % tq == 0 and nsel & (nsel - 1) == 0 and tq & (tq - 1) == 0
    assert (top_k * SLC_BLOCK) % tq == 0 and top_k >= 3
    assert nsel % (CMP_SPANS * BF16_SUBLANES) == 0 and (SLC_RATIO * nsel // CMP_SPANS) % LANE == 0
    assert (nsel // CMP_SPANS * SLC_BLOCK) % tq == 0
    kern = functools.partial(_nsa_cmp_kernel, top_k=top_k)
    return pl.pallas_call(
        kern,
        grid=(B, Hkv, S // tq),
        in_specs=[
            pl.BlockSpec((G, d, tq), lambda b, h, i: (h, 0, b * (S // tq) + i)),
            pl.BlockSpec((1, 1, ncp, d), lambda b, h, i: (b, h, 0, 0)),
            pl.BlockSpec((1, 1, d, ncp), lambda b, h, i: (b, h, 0, 0)),
        ],
        out_specs=[
            pl.BlockSpec((1, tq, G * d), lambda b, h, i: (b, i, h)),
            pl.BlockSpec((1, 1, nsel, tq), lambda b, h, i: (b, h, 0, i)),
        ],
        out_shape=[
            jax.ShapeDtypeStruct((B, S, Hkv * G * d), F32),
            jax.ShapeDtypeStruct((B, Hkv, nsel, S), MXU_DTYPE),
        ],
        scratch_shapes=[pltpu.VMEM((d, G * tq), MXU_DTYPE)],
        compiler_params=_cparams("arbitrary", "arbitrary", "arbitrary"),
        name="nsa_cmp",
    )(qt, kc, vct)


def _nsa_slc_kernel(q_ref, sel_ref, k_ref, vt_ref, o_ref, qa_ref, m_ref, acc_ref, s_ref, smax_ref, *, tk):
    i = pl.program_id(2)
    tq = q_ref.shape[2]
    d = NSA_HEAD_DIM
    G = NSA_GROUP
    nblk = sel_ref.shape[2]
    span = LANE * SLC_BLOCK
    n_halves = pl.cdiv(nblk, LANE)

    m_ref[...] = jnp.full_like(m_ref, PICKED)
    acc_ref[...] = jnp.zeros_like(acc_ref)
    for g in range(G):
        qa_ref[:d, g * tq:(g + 1) * tq] = q_ref[g]

    def scores(kt_idx):
        k0 = pl.multiple_of(kt_idx * tk, tk)
        return jnp.dot(k_ref[0, pl.ds(k0, tk), :], qa_ref[...], preferred_element_type=F32)

    def put_scores(slot, kt_idx):
        s = scores(kt_idx)
        s_ref[slot] = s
        smax_ref[slot] = jnp.max(s, axis=0, keepdims=True)

    def accumulate(kt_idx, s, smax):
        k0 = pl.multiple_of(kt_idx * tk, tk)
        m_old = m_ref[...]
        m_new = jnp.maximum(m_old, smax)
        alpha = jnp.exp2(m_old - m_new)
        p = jnp.exp2(s - m_new)
        acc_ref[...] = alpha * acc_ref[...] + jnp.dot(vt_ref[0, :, pl.ds(k0, tk)], p.astype(MXU_DTYPE),
                                                      preferred_element_type=F32)
        m_ref[...] = m_new

    last = ((i + 1) * tq - 1) // tk
    tiles_per_half = span // tk
    for hf in range(n_halves):
        lo = hf * tiles_per_half
        hi = jnp.minimum(last, lo + tiles_per_half)

        @pl.when(lo <= last)
        def _():
            sel_half = sel_ref[0, 0, hf * LANE:(hf + 1) * LANE, :]
            for g in range(G):
                qa_ref[d:, g * tq:(g + 1) * tq] = sel_half
            n = hi - lo
            odd = (n & 1) == 1
            diag_here = last < lo + tiles_per_half

            def accumulate_diagonal(s):
                col = lax.broadcasted_iota(jnp.int32, s.shape, 1)
                tok = i * tq + (col & (tq - 1))
                key = last * tk + lax.broadcasted_iota(jnp.int32, s.shape, 0)
                masked = jnp.where(key <= tok, s, NEG)
                accumulate(last, masked, jnp.max(masked, axis=0, keepdims=True))

            put_scores(0, lo)

            def pair(pi, carry):
                t = lo + 2 * pi
                put_scores(1, t + 1)
                accumulate(t, s_ref[0], smax_ref[0])
                put_scores(0, t + 2)
                accumulate(t + 1, s_ref[1], smax_ref[1])
                return carry

            lax.fori_loop(0, n >> 1, pair, 0)

            @pl.when(odd & diag_here)
            def _():
                put_scores(1, last)
                accumulate(hi - 1, s_ref[0], smax_ref[0])
                accumulate_diagonal(s_ref[1])

            @pl.when(odd & jnp.logical_not(diag_here))
            def _():
                accumulate(hi - 1, s_ref[0], smax_ref[0])

            @pl.when(jnp.logical_not(odd) & diag_here)
            def _():
                accumulate_diagonal(s_ref[0])

    out = acc_ref[:d, :] * (1.0 / acc_ref[d:d + 1, :])
    for g in range(G):
        o_ref[0, :, g * d:(g + 1) * d] = out[:, g * tq:(g + 1) * tq].T


def nsa_slc_branch(qt, B, sel, k_aug, vt, *, tq=512, tk=1024):
    S = qt.shape[2] // B
    Hkv, G, d = NSA_KV_HEADS, NSA_GROUP, NSA_HEAD_DIM
    nblk = sel.shape[2]
    dv = vt.shape[1]
    tq, tk = min(tq, S), min(tk, S)
    assert S % tq == 0 and S % tk == 0 and tk % tq == 0 and tq & (tq - 1) == 0
    assert nblk % LANE == 0 and (LANE * SLC_BLOCK) % tk == 0
    kern = functools.partial(_nsa_slc_kernel, tk=tk)
    return pl.pallas_call(
        kern,
        grid=(B, Hkv, S // tq),
        in_specs=[
            pl.BlockSpec((G, d, tq), lambda b, h, i: (h, 0, b * (S // tq) + i)),
            pl.BlockSpec((1, 1, nblk, tq), lambda b, h, i: (b, h, 0, i)),
            pl.BlockSpec((1, S, d + LANE), lambda b, h, i: (h, b, 0), pipeline_mode=pl.Buffered(1)),
            pl.BlockSpec((1, dv, S), lambda b, h, i: (h, 0, b), pipeline_mode=pl.Buffered(1)),
        ],
        out_specs=pl.BlockSpec((1, tq, G * d), lambda b, h, i: (b, i, h)),
        out_shape=jax.ShapeDtypeStruct((B, S, Hkv * G * d), F32),
        scratch_shapes=[
            pltpu.VMEM((d + LANE, G * tq), MXU_DTYPE),
            pltpu.VMEM((1, G * tq), F32),
            pltpu.VMEM((dv, G * tq), F32),
            pltpu.VMEM((2, tk, G * tq), F32),
            pltpu.VMEM((2, 1, G * tq), F32),
        ],
        compiler_params=_cparams("arbitrary", "arbitrary", "arbitrary"),
        name="nsa_slc",
    )(qt, sel, k_aug, vt)


def _nsa_win_kernel(q_ref, k_ref, vt_ref, bias_ref, o_ref, qt_ref):
    i = pl.program_id(2)
    tq = q_ref.shape[2]
    d = NSA_HEAD_DIM
    G = NSA_GROUP
    nk = WIN + tq
    k0 = pl.multiple_of(jnp.maximum(i * tq - WIN, 0), tq)
    for g in range(G):
        qt_ref[:, g * tq:(g + 1) * tq] = q_ref[g]
    s = (jnp.dot(k_ref[0, pl.ds(k0, nk), :], qt_ref[...], preferred_element_type=F32)
         + jnp.concatenate([bias_ref[0]] * G, axis=1))
    e = jnp.exp2(s - jnp.max(s, axis=0, keepdims=True))
    p = e * (1.0 / jnp.sum(e, axis=0, keepdims=True))
    out_t = jnp.dot(vt_ref[0, :d, pl.ds(k0, nk)], p.astype(MXU_DTYPE), preferred_element_type=F32)
    for g in range(G):
        o_ref[0, :, g * d:(g + 1) * d] = out_t[:, g * tq:(g + 1) * tq].T


def nsa_win_branch(qt, B, k_win, vt, *, tq=256):
    S = qt.shape[2] // B
    Hkv, G, d = NSA_KV_HEADS, NSA_GROUP, NSA_HEAD_DIM
    dv = vt.shape[1]
    tq = min(tq, S)
    assert S % tq == 0 and tq % LANE == 0 and tq & (tq - 1) == 0 and WIN % tq == 0 and S >= WIN + tq
    kern = _nsa_win_kernel
    n_clipped = WIN // tq
    t0 = jnp.minimum(jnp.arange(n_clipped + 1) * tq, WIN)[:, None, None]
    pos = (t0 - jnp.minimum(t0, WIN)) + jnp.arange(WIN + tq)[None, :, None]
    tok = t0 + jnp.arange(tq)[None, None, :]
    bias = jnp.where((pos <= tok) & (pos > tok - WIN), 0.0, NEG).astype(F32)
    return pl.pallas_call(
        kern,
        grid=(B, Hkv, S // tq),
        in_specs=[
            pl.BlockSpec((G, d, tq), lambda b, h, i: (h, 0, b * (S // tq) + i)),
            pl.BlockSpec((1, S, d), lambda b, h, i: (h, b, 0)),
            pl.BlockSpec((1, dv, S), lambda b, h, i: (Hkv + h, 0, b)),
            pl.BlockSpec((1, WIN + tq, tq), lambda b, h, i: (jnp.minimum(i, n_clipped), 0, 0)),
        ],
        out_specs=pl.BlockSpec((1, tq, G * d), lambda b, h, i: (b, i, h)),
        out_shape=jax.ShapeDtypeStruct((B, S, Hkv * G * d), F32),
        scratch_shapes=[pltpu.VMEM((d, G * tq), MXU_DTYPE)],
        compiler_params=_cparams("arbitrary", "arbitrary", "arbitrary"),
        name="nsa_win",
    )(qt, k_win, vt, bias)


def _nsa_out_kernel(oc_ref, os_ref, ow_ref, gate_ref, w_ref, h_ref, o_ref, merged_ref):
    d = NSA_HEAD_DIM
    gates = _sigmoid(gate_ref[...])
    for hq in range(NSA_Q_HEADS):
        cols = slice(hq * d, (hq + 1) * d)
        c = N_BRANCH * hq
        merged = (gates[:, c:c + 1] * oc_ref[:, cols] + gates[:, c + 1:c + 2] * os_ref[:, cols]
                  + gates[:, c + 2:c + 3] * ow_ref[:, cols])
        merged_ref[:, cols] = merged.astype(merged_ref.dtype)
    o_ref[...] = h_ref[...] + jnp.dot(merged_ref[...], w_ref[...], preferred_element_type=F32)


def nsa_out(o_cmp, o_slc, o_win, gate_logits, w_o, h, *, tm=256):
    T, HD = o_cmp.shape
    D = w_o.shape[1]
    tm = min(tm, T)
    assert T % tm == 0 and gate_logits.shape == (T, LANE)
    row = lambda i: (i, 0)
    return pl.pallas_call(
        _nsa_out_kernel,
        grid=(T // tm,),
        in_specs=[
            pl.BlockSpec((tm, HD), row),
            pl.BlockSpec((tm, HD), row),
            pl.BlockSpec((tm, HD), row),
            pl.BlockSpec((tm, LANE), row),
            pl.BlockSpec((HD, D), lambda i: (0, 0)),
            pl.BlockSpec((tm, D), row),
        ],
        out_specs=pl.BlockSpec((tm, D), row),
        out_shape=jax.ShapeDtypeStruct((T, D), F32),
        scratch_shapes=[pltpu.VMEM((tm, HD), w_o.dtype)],
        compiler_params=_cparams("arbitrary"),
        name="nsa_out",
    )(o_cmp, o_slc, o_win, gate_logits, w_o, h)


def _span_major(x, nsel):
    B, n, H, d = x.shape
    qsz = nsel // CMP_SPANS
    return x.reshape(B, CMP_SPANS, qsz, SLC_RATIO, H, d).transpose(0, 4, 1, 3, 2, 5).reshape(B, H, n, d)


def _cast(w):
    return w.astype(MXU_DTYPE)


def nsa_shared_kv(h, B, S, kv_gain, w_kv, cmp_pe_k, cmp_w1_k, cmp_w2_k, cmp_pe_v, cmp_w1_v, cmp_w2_v):
    Hkv, d = NSA_KV_HEADS, NSA_HEAD_DIM
    cmp_in, k_slc, k_win, vt = kv_projection(h, S, kv_gain, _cast(w_kv))
    n = S // CMP_STRIDE
    nsel = S // SLC_BLOCK
    cmp = compress_tokens(cmp_in, B, jnp.stack([cmp_pe_k, cmp_pe_v]), _cast(jnp.stack([cmp_w1_k, cmp_w1_v])),
                          _cast(jnp.stack([cmp_w2_k, cmp_w2_v])))
    cmp = cmp.reshape(2, B, Hkv, n, d).transpose(0, 1, 3, 2, 4)
    kc = _cast(_span_major(cmp[0], nsel))
    vct = _cast(_span_major(cmp[1], nsel).transpose(0, 1, 3, 2))
    return kc, vct, k_slc, k_win, vt


def nsa_attention_block(h, B, S, mix_gain, w_q, w_o, shared):
    kc, vct, k_slc, k_win, vt = shared
    T = B * S
    nq = NSA_Q_HEADS * NSA_HEAD_DIM
    nqp = nq + LANE
    w_q = jnp.pad(w_q, ((0, 0), (0, nqp - w_q.shape[1])))
    qt, gate_logits = q_projection(h, mix_gain, _cast(w_q))
    o_cmp, sel = nsa_cmp_branch(qt, B, kc, vct)
    o_slc = nsa_slc_branch(qt, B, sel, k_slc, vt)
    o_win = nsa_win_branch(qt, B, k_win, vt)
    return nsa_out(o_cmp.reshape(T, nq), o_slc.reshape(T, nq), o_win.reshape(T, nq), gate_logits, _cast(w_o), h)


def nsa_mixer(h, B, S, mix_gain, kv_gain, w_kv, cmp_pe_k, cmp_w1_k, cmp_w2_k, cmp_pe_v, cmp_w1_v, cmp_w2_v, w_q, w_o):
    shared = nsa_shared_kv(h, B, S, kv_gain, w_kv, cmp_pe_k, cmp_w1_k, cmp_w2_k, cmp_pe_v, cmp_w1_v, cmp_w2_v)
    return nsa_attention_block(h, B, S, mix_gain, w_q, w_o, shared)


def kernel(x, norm_mix_gain, norm_ffn_gain, ret_w_in, ret_gn_gain, ret_w_out, nsa_kv_norm_gain, nsa_w_kv, cmp_pe_k, cmp_w1_k, cmp_w2_k, cmp_pe_v, cmp_w1_v, cmp_w2_v, nsa_w_q, nsa_w_o, ffn_w_in, ffn_conv_w, ffn_conv_b, ffn_w_out, final_norm_gain):
    B, S, D = x.shape
    T = B * S
    depth = norm_mix_gain.shape[0]
    n_a = depth // 2
    h = x.reshape(T, D)
    shared = None
    for layer in range(depth):
        if layer < n_a:
            proj = norm_matmul(h, norm_mix_gain[layer], _cast(ret_w_in[layer]), tm=1024, tn=1024)
            y = retention_core(proj.reshape(B, S, -1), ret_gn_gain[layer])
            h = matmul_residual(y.reshape(T, -1), _cast(ret_w_out[layer]), h, tm=512, tn=1024)
        else:
            if layer == n_a:
                shared = nsa_shared_kv(h, B, S, nsa_kv_norm_gain, nsa_w_kv, cmp_pe_k, cmp_w1_k, cmp_w2_k,
                                       cmp_pe_v, cmp_w1_v, cmp_w2_v)
            b = layer - n_a
            h = nsa_attention_block(h, B, S, norm_mix_gain[layer], nsa_w_q[b], nsa_w_o[b], shared)
        h = conv_ffn_block(h, S, norm_ffn_gain[layer], _cast(ffn_w_in[layer]), ffn_conv_w[layer], ffn_conv_b[layer],
                           _cast(ffn_w_out[layer]), final_norm_gain, final_norm=(layer == depth - 1))
    return h.reshape(B, S, D)
```

```python
import functools

import jax
import jax.numpy as jnp
from jax import lax
from jax.experimental import pallas as pl
from jax.experimental.pallas import tpu as pltpu

F32 = jnp.float32
MXU_DTYPE = jnp.bfloat16

RMS_EPS = 1e-6
ROPE_BASE = 10000.0

RET_HEADS = 8
RET_CHUNK = 128

NSA_Q_HEADS = 16
NSA_KV_HEADS = 4
NSA_GROUP = NSA_Q_HEADS // NSA_KV_HEADS
NSA_HEAD_DIM = 128
N_BRANCH = 3
CMP_BLOCK = 32
CMP_STRIDE = 16
CMP_RATIO = CMP_BLOCK // CMP_STRIDE
SLC_BLOCK = 64
SLC_TOPK = 16
SLC_RATIO = SLC_BLOCK // CMP_STRIDE
CMP_SPANS = 8
WIN = 512
CONV_W = 3

NEG = -1e30
MASK_BIAS = -(2.0 ** 100)
PICKED = -3.0e38
SEL_FORCE = 1e30
LOG2_E = 1.4426950408889634

LANE = 128
BF16_SUBLANES = 16
VMEM_LIMIT_BYTES = 56 * 1024 * 1024


def _cparams(*sem):
    return pltpu.CompilerParams(dimension_semantics=sem, vmem_limit_bytes=VMEM_LIMIT_BYTES)


def _sigmoid(x):
    return 1.0 / (1.0 + jnp.exp(-x))


def _rms_normalize(x, gain):
    ms = jnp.mean(x * x, axis=-1, keepdims=True)
    return x * lax.rsqrt(ms + RMS_EPS) * gain


def _norm_matmul_kernel(x_ref, g_ref, w_ref, o_ref, xn_ref):
    @pl.when(pl.program_id(1) == 0)
    def _():
        xn_ref[...] = _rms_normalize(x_ref[...], g_ref[...]).astype(xn_ref.dtype)

    o_ref[...] = jnp.dot(xn_ref[...], w_ref[...], preferred_element_type=F32).astype(o_ref.dtype)


def norm_matmul(h, gain, w, *, tm, tn, out_dtype=F32):
    T, D = h.shape
    N = w.shape[1]
    tm, tn = min(tm, T), min(tn, N)
    assert T % tm == 0 and N % tn == 0
    return pl.pallas_call(
        _norm_matmul_kernel,
        grid=(T // tm, N // tn),
        in_specs=[
            pl.BlockSpec((tm, D), lambda i, j: (i, 0)),
            pl.BlockSpec((1, D), lambda i, j: (0, 0)),
            pl.BlockSpec((D, tn), lambda i, j: (0, j)),
        ],
        out_specs=pl.BlockSpec((tm, tn), lambda i, j: (i, j)),
        out_shape=jax.ShapeDtypeStruct((T, N), out_dtype),
        scratch_shapes=[pltpu.VMEM((tm, D), w.dtype)],
        compiler_params=_cparams("arbitrary", "arbitrary"),
        name="norm_matmul",
    )(h, gain.reshape(1, D), w)


def _matmul_residual_kernel(y_ref, w_ref, h_ref, o_ref):
    o_ref[...] = h_ref[...] + jnp.dot(y_ref[...], w_ref[...], preferred_element_type=F32)


def matmul_residual(y, w, h, *, tm, tn):
    T, K = y.shape
    N = w.shape[1]
    tm, tn = min(tm, T), min(tn, N)
    assert T % tm == 0 and N % tn == 0
    return pl.pallas_call(
        _matmul_residual_kernel,
        grid=(T // tm, N // tn),
        in_specs=[
            pl.BlockSpec((tm, K), lambda i, j: (i, 0)),
            pl.BlockSpec((K, tn), lambda i, j: (0, j)),
            pl.BlockSpec((tm, tn), lambda i, j: (i, j)),
        ],
        out_specs=pl.BlockSpec((tm, tn), lambda i, j: (i, j)),
        out_shape=jax.ShapeDtypeStruct((T, N), F32),
        compiler_params=_cparams("arbitrary", "arbitrary"),
        name="matmul_residual",
    )(y, w, h)


def _retention_kernel(q_ref, k_ref, v_ref, g_ref, cos_ref, sin_ref, dmask_ref, qdec_ref, kdec_ref, gn_ref,
                      o_ref, state_ref, *, chunk, n_chunks, k_scale):
    @pl.when(pl.program_id(2) == 0)
    def _():
        state_ref[...] = jnp.zeros_like(state_ref)

    half = q_ref.shape[-1] // 2
    dmask = dmask_ref[0]
    qdec = qdec_ref[0]
    kdec = kdec_ref[0]
    chunk_decay = qdec[chunk - 1:chunk, :]
    gn = gn_ref[...]

    def rotate(x, cos, sin):
        x1, x2 = x[:, :half], x[:, half:]
        return jnp.concatenate([x1 * cos - x2 * sin, x1 * sin + x2 * cos], axis=-1)

    for ci in range(n_chunks):
        rows = pl.ds(ci * chunk, chunk)
        cos, sin = cos_ref[rows, :], sin_ref[rows, :]
        q = rotate(q_ref[0, rows, :], cos, sin)
        k = rotate(k_ref[0, rows, :], cos, sin) * k_scale
        v = v_ref[0, rows, :].astype(MXU_DTYPE)
        qm = q.astype(MXU_DTYPE)
        scores = lax.dot_general(qm, k.astype(MXU_DTYPE), (((1,), (1,)), ((), ())),
                                 preferred_element_type=F32) * dmask
        inner = jnp.dot(scores.astype(MXU_DTYPE), v, preferred_element_type=F32)
        state = state_ref[...]
        cross = jnp.dot(qm, state.astype(MXU_DTYPE), preferred_element_type=F32) * qdec
        k_dec_t = (k * kdec).T.astype(MXU_DTYPE)
        state_ref[...] = state * chunk_decay + jnp.dot(k_dec_t, v, preferred_element_type=F32)
        out = inner + cross
        mu = jnp.mean(out, axis=-1, keepdims=True)
        cen = out - mu
        var = jnp.mean(cen * cen, axis=-1, keepdims=True)
        normed = cen * lax.rsqrt(var + RMS_EPS) * gn
        g = g_ref[0, rows, :]
        o_ref[0, rows, :] = ((g * _sigmoid(g)) * normed).astype(o_ref.dtype)


def retention_core(proj, gn_gain, *, chunk=2 * RET_CHUNK, tokens_per_step=1024):
    B, S, P = proj.shape
    H = RET_HEADS
    dk = P // (6 * H)
    dv = 2 * dk
    tc = min(tokens_per_step, S)
    assert S % tc == 0 and tc % chunk == 0
    half = dk // 2
    pos = jnp.arange(S, dtype=F32)
    freqs = ROPE_BASE ** (-jnp.arange(half, dtype=F32) / half)
    ang = pos[:, None] * freqs[None, :]
    cos, sin = jnp.cos(ang), jnp.sin(ang)
    log_gamma = jnp.log(1.0 - 2.0 ** (-5.0 - jnp.arange(H, dtype=F32)))
    idx = jnp.arange(chunk, dtype=F32)
    diff = idx[:, None] - idx[None, :]
    dmask = jnp.where(diff >= 0, jnp.exp(jnp.maximum(diff, 0.0)[None] * log_gamma[:, None, None]), 0.0)
    qdec = jnp.exp((idx[None, :] + 1.0) * log_gamma[:, None])[:, :, None]
    kdec = jnp.exp((chunk - 1.0 - idx[None, :]) * log_gamma[:, None])[:, :, None]
    kern = functools.partial(_retention_kernel, chunk=chunk, n_chunks=tc // chunk, k_scale=dk ** -0.5)
    k_off, v_off, g_off = (H * dk) // dk, (2 * H * dk) // dv, (2 * H * dk + H * dv) // dv
    return pl.pallas_call(
        kern,
        grid=(B, H, S // tc),
        in_specs=[
            pl.BlockSpec((1, tc, dk), lambda b, h, c: (b, c, h)),
            pl.BlockSpec((1, tc, dk), lambda b, h, c: (b, c, k_off + h)),
            pl.BlockSpec((1, tc, dv), lambda b, h, c: (b, c, v_off + h)),
            pl.BlockSpec((1, tc, dv), lambda b, h, c: (b, c, g_off + h)),
            pl.BlockSpec((tc, half), lambda b, h, c: (c, 0)),
            pl.BlockSpec((tc, half), lambda b, h, c: (c, 0)),
            pl.BlockSpec((1, chunk, chunk), lambda b, h, c: (h, 0, 0)),
            pl.BlockSpec((1, chunk, 1), lambda b, h, c: (h, 0, 0)),
            pl.BlockSpec((1, chunk, 1), lambda b, h, c: (h, 0, 0)),
            pl.BlockSpec((1, dv), lambda b, h, c: (0, h)),
        ],
        out_specs=pl.BlockSpec((1, tc, dv), lambda b, h, c: (b, c, h)),
        out_shape=jax.ShapeDtypeStruct((B, S, H * dv), MXU_DTYPE),
        scratch_shapes=[pltpu.VMEM((dk, dv), F32)],
        compiler_params=_cparams("arbitrary", "arbitrary", "arbitrary"),
        name="retention_core",
    )(proj, proj, proj, proj, cos, sin, dmask, qdec, kdec, gn_gain.reshape(1, H * dv))


def _conv_ffn_kernel(x_ref, g_ref, wa_ref, wb_ref, cwa_ref, cwb_ref, cba_ref, cbb_ref, wo_ref, fg_ref,
                     o_ref, xn_ref, acc_ref, carry_a_ref, carry_b_ref, *, tiles_per_seq, final_norm):
    i, j = pl.program_id(0), pl.program_id(1)
    tm = x_ref.shape[0]

    @pl.when((i == 0) & (j == 0))
    def _():
        carry_a_ref[...] = jnp.zeros_like(carry_a_ref)
        carry_b_ref[...] = jnp.zeros_like(carry_b_ref)

    @pl.when(j == 0)
    def _():
        xn_ref[...] = _rms_normalize(x_ref[...], g_ref[...]).astype(xn_ref.dtype)
        acc_ref[...] = jnp.zeros_like(acc_ref)

    seq_start = (i % tiles_per_seq) == 0

    def causal_conv(w_ref, cw_ref, cb_ref, carry_ref):
        u = jnp.dot(xn_ref[...], w_ref[...], preferred_element_type=F32)
        prev = jnp.where(seq_start, 0.0, carry_ref[j])
        carry_ref[j] = u[tm - 8:, :]
        row = lax.broadcasted_iota(jnp.int32, u.shape, 0)
        u1 = jnp.where(row == 0, prev[7:8, :], pltpu.roll(u, 1, 0))
        u2 = jnp.where(row == 0, prev[6:7, :], jnp.where(row == 1, prev[7:8, :], pltpu.roll(u, 2, 0)))
        cw = cw_ref[...]
        return ((cb_ref[...] + u2 * cw[0:1, :]) + u1 * cw[1:2, :]) + u * cw[2:3, :]

    a = causal_conv(wa_ref, cwa_ref, cba_ref, carry_a_ref)
    b = causal_conv(wb_ref, cwb_ref, cbb_ref, carry_b_ref)
    act = ((a * _sigmoid(a)) * b).astype(wo_ref.dtype)
    acc_ref[...] += jnp.dot(act, wo_ref[...], preferred_element_type=F32)

    @pl.when(j == pl.num_programs(1) - 1)
    def _():
        out = x_ref[...] + acc_ref[...]
        if final_norm:
            out = _rms_normalize(out, fg_ref[...])
        o_ref[...] = out


def conv_ffn_block(h, seq_len, norm_gain, w_in, conv_w, conv_b, w_out, final_gain, *, final_norm, tm=512, tn=512):
    T, D = h.shape
    F = w_out.shape[0]
    tm, tn = min(tm, seq_len), min(tn, F)
    assert seq_len % tm == 0 and F % tn == 0 and tm >= 8
    nj = F // tn
    kern = functools.partial(_conv_ffn_kernel, tiles_per_seq=seq_len // tm, final_norm=final_norm)
    return pl.pallas_call(
        kern,
        grid=(T // tm, nj),
        in_specs=[
            pl.BlockSpec((tm, D), lambda i, j: (i, 0)),
            pl.BlockSpec((1, D), lambda i, j: (0, 0)),
            pl.BlockSpec((D, tn), lambda i, j: (0, j)),
            pl.BlockSpec((D, tn), lambda i, j: (0, nj + j)),
            pl.BlockSpec((CONV_W, tn), lambda i, j: (0, j)),
            pl.BlockSpec((CONV_W, tn), lambda i, j: (0, nj + j)),
            pl.BlockSpec((1, tn), lambda i, j: (0, j)),
            pl.BlockSpec((1, tn), lambda i, j: (0, nj + j)),
            pl.BlockSpec((tn, D), lambda i, j: (j, 0)),
            pl.BlockSpec((1, D), lambda i, j: (0, 0)),
        ],
        out_specs=pl.BlockSpec((tm, D), lambda i, j: (i, 0)),
        out_shape=jax.ShapeDtypeStruct((T, D), F32),
        scratch_shapes=[
            pltpu.VMEM((tm, D), w_in.dtype),
            pltpu.VMEM((tm, D), F32),
            pltpu.VMEM((nj, 8, tn), F32),
            pltpu.VMEM((nj, 8, tn), F32),
        ],
        compiler_params=_cparams("arbitrary", "arbitrary"),
        name="conv_ffn",
    )(h, norm_gain.reshape(1, D), w_in, w_in, conv_w, conv_w, conv_b.reshape(1, 2 * F), conv_b.reshape(1, 2 * F),
      w_out, final_gain.reshape(1, D))


def _kv_proj_kernel(x_ref, g_ref, w_ref, cmp_ref, kslc_ref, kwin_ref, vt_ref, *, seq_len):
    tm = x_ref.shape[0]
    d = NSA_HEAD_DIM
    Hkv = NSA_KV_HEADS
    xn = _rms_normalize(x_ref[...], g_ref[...]).astype(w_ref.dtype)
    res = jnp.dot(xn, w_ref[...], preferred_element_type=F32)
    col = lambda branch, hh: res[:, (branch * Hkv + hh) * d:(branch * Hkv + hh + 1) * d]
    pos = (pl.program_id(0) * tm) % seq_len + lax.broadcasted_iota(jnp.int32, (tm, LANE), 0)
    blk = (pos >> (SLC_BLOCK.bit_length() - 1)) & (LANE - 1)
    onehot = jnp.where(blk == lax.broadcasted_iota(jnp.int32, (tm, LANE), 1), 1.0, 0.0).astype(kslc_ref.dtype)
    ones = jnp.ones((BF16_SUBLANES, tm), vt_ref.dtype)
    for hh in range(Hkv):
        cmp_ref[hh] = col(0, hh)
        cmp_ref[Hkv + hh] = col(1, hh)
        kslc_ref[hh, :, :d] = col(2, hh).astype(kslc_ref.dtype)
        kslc_ref[hh, :, d:] = onehot
        kwin_ref[hh] = col(4, hh).astype(kwin_ref.dtype)
        vt_ref[hh, :d, :] = col(3, hh).T.astype(vt_ref.dtype)
        vt_ref[hh, d:, :] = ones
        vt_ref[Hkv + hh, :d, :] = col(5, hh).T.astype(vt_ref.dtype)
        vt_ref[Hkv + hh, d:, :] = ones


def kv_projection(h, seq_len, gain, w_kv, *, tm=256):
    T, D = h.shape
    Hkv, d = NSA_KV_HEADS, NSA_HEAD_DIM
    N = w_kv.shape[1]
    tm = min(tm, seq_len)
    assert seq_len % tm == 0 and N == 2 * N_BRANCH * Hkv * d and tm % LANE == 0
    row3 = lambda i: (0, i, 0)
    return pl.pallas_call(
        functools.partial(_kv_proj_kernel, seq_len=seq_len),
        grid=(T // tm,),
        in_specs=[
            pl.BlockSpec((tm, D), lambda i: (i, 0)),
            pl.BlockSpec((1, D), lambda i: (0, 0)),
            pl.BlockSpec((D, N), lambda i: (0, 0)),
        ],
        out_specs=[
            pl.BlockSpec((2 * Hkv, tm, d), row3),
            pl.BlockSpec((Hkv, tm, d + LANE), row3),
            pl.BlockSpec((Hkv, tm, d), row3),
            pl.BlockSpec((2 * Hkv, d + BF16_SUBLANES, tm), lambda i: (0, 0, i)),
        ],
        out_shape=[
            jax.ShapeDtypeStruct((2 * Hkv, T, d), F32),
            jax.ShapeDtypeStruct((Hkv, T, d + LANE), MXU_DTYPE),
            jax.ShapeDtypeStruct((Hkv, T, d), MXU_DTYPE),
            jax.ShapeDtypeStruct((2 * Hkv, d + BF16_SUBLANES, T), MXU_DTYPE),
        ],
        compiler_params=_cparams("arbitrary"),
        name="kv_projection",
    )(h, gain.reshape(1, D), w_kv)


def _q_proj_kernel(x_ref, g_ref, w_ref, qt_ref, gate_ref, *, q_scale):
    d = NSA_HEAD_DIM
    xn = _rms_normalize(x_ref[...], g_ref[...]).astype(w_ref.dtype)
    res = jnp.dot(xn, w_ref[...], preferred_element_type=F32)
    for hq in range(NSA_Q_HEADS):
        qt_ref[hq] = (res[:, hq * d:(hq + 1) * d] * q_scale).T.astype(qt_ref.dtype)
    gate_ref[...] = res[:, NSA_Q_HEADS * d:]


def q_projection(h, gain, w_q, *, tm=256):
    T, D = h.shape
    d = NSA_HEAD_DIM
    N = w_q.shape[1]
    tm = min(tm, T)
    assert T % tm == 0 and N == NSA_Q_HEADS * d + LANE and tm % LANE == 0
    return pl.pallas_call(
        functools.partial(_q_proj_kernel, q_scale=d ** -0.5 * LOG2_E),
        grid=(T // tm,),
        in_specs=[
            pl.BlockSpec((tm, D), lambda i: (i, 0)),
            pl.BlockSpec((1, D), lambda i: (0, 0)),
            pl.BlockSpec((D, N), lambda i: (0, 0)),
        ],
        out_specs=[
            pl.BlockSpec((NSA_Q_HEADS, d, tm), lambda i: (0, 0, i)),
            pl.BlockSpec((tm, LANE), lambda i: (i, 0)),
        ],
        out_shape=[
            jax.ShapeDtypeStruct((NSA_Q_HEADS, d, T), MXU_DTYPE),
            jax.ShapeDtypeStruct((T, LANE), F32),
        ],
        compiler_params=_cparams("arbitrary"),
        name="q_projection",
    )(h, gain.reshape(1, D), w_q)


def _compress_kernel(x_ref, pe_ref, w1_ref, w2_ref, o_ref):
    d = x_ref.shape[2]
    n = o_ref.shape[2]
    first = jnp.zeros((n, w1_ref.shape[2]), F32)
    second = jnp.zeros((n, w1_ref.shape[2]), F32)
    for p in range(CMP_STRIDE):
        xp = x_ref[0, pl.ds(p, n, stride=CMP_STRIDE), :]
        q = CMP_STRIDE + p
        first = first + jnp.dot((xp + pe_ref[0, p:p + 1, :]).astype(MXU_DTYPE), w1_ref[0, p * d:(p + 1) * d, :],
                                preferred_element_type=F32)
        second = second + jnp.dot((xp + pe_ref[0, q:q + 1, :]).astype(MXU_DTYPE), w1_ref[0, q * d:(q + 1) * d, :],
                                  preferred_element_type=F32)
    hid = first + pltpu.roll(second, n - 1, 0)
    gelu = 0.5 * hid * (1.0 + jnp.tanh(0.7978845608028654 * (hid + 0.044715 * (hid * hid * hid))))
    out = jnp.dot(gelu.astype(MXU_DTYPE), w2_ref[0], preferred_element_type=F32)
    row = lax.broadcasted_iota(jnp.int32, out.shape, 0)
    o_ref[0, 0] = jnp.where(row < n - 1, out, 0.0)


def compress_tokens(x, batch, pe, w1, w2):
    H2, T, d = x.shape
    Hkv = H2 // 2
    S = T // batch
    n = S // CMP_STRIDE
    hid = w2.shape[1]
    return pl.pallas_call(
        _compress_kernel,
        grid=(2, batch, Hkv),
        in_specs=[
            pl.BlockSpec((1, S, d), lambda s, b, hh: (s * Hkv + hh, b, 0)),
            pl.BlockSpec((1, CMP_BLOCK, d), lambda s, b, hh: (s, 0, 0)),
            pl.BlockSpec((1, CMP_BLOCK * d, hid), lambda s, b, hh: (s, 0, 0)),
            pl.BlockSpec((1, hid, d), lambda s, b, hh: (s, 0, 0)),
        ],
        out_specs=pl.BlockSpec((1, 1, n, d), lambda s, b, hh: (s, b * Hkv + hh, 0, 0)),
        out_shape=jax.ShapeDtypeStruct((2, batch * Hkv, n, d), F32),
        compiler_params=_cparams("arbitrary", "arbitrary", "arbitrary"),
        name="compress_tokens",
    )(x, pe, w1, w2)


def _nsa_cmp_kernel(q_ref, kc_ref, vct_ref, o_ref, sel_ref, qt_ref, *, top_k):
    i = pl.program_id(2)
    tq = q_ref.shape[2]
    d = NSA_HEAD_DIM
    G = NSA_GROUP
    nsel = kc_ref.shape[2] // SLC_RATIO
    qsz = nsel // CMP_SPANS
    span_rows = SLC_RATIO * qsz
    t0 = i * tq
    for g in range(G):
        qt_ref[:, g * tq:(g + 1) * tq] = q_ref[g]

    def attend(n_spans):
        nrows, nj = n_spans * span_rows, n_spans * qsz
        row = lax.broadcasted_iota(jnp.int32, (nrows, tq), 0)
        tok = t0 + lax.broadcasted_iota(jnp.int32, (nrows, tq), 1)
        blk = (row >> (span_rows.bit_length() - 1)) * qsz + (row & (qsz - 1))
        cblk = SLC_RATIO * blk + ((row >> (qsz.bit_length() - 1)) & (SLC_RATIO - 1))
        bias = jnp.where(cblk * CMP_STRIDE + (CMP_BLOCK - 1) <= tok, 0.0, NEG)
        s = (jnp.dot(kc_ref[0, 0, :nrows, :], qt_ref[...], preferred_element_type=F32)
             + jnp.concatenate([bias] * G, axis=1))
        e = jnp.exp2(s - jnp.max(s, axis=0, keepdims=True))
        denom = jnp.sum(e, axis=0, keepdims=True)
        any_valid = jnp.concatenate([tok[0:1, :] >= CMP_BLOCK - 1] * G, axis=1)
        p = e * jnp.where(any_valid, 1.0 / denom, 0.0)
        out_t = jnp.dot(vct_ref[0, 0, :, :nrows], p.astype(MXU_DTYPE), preferred_element_type=F32)
        imp = None
        for g in range(G):
            o_ref[0, :, g * d:(g + 1) * d] = out_t[:, g * tq:(g + 1) * tq].T
            pg = p[:, g * tq:(g + 1) * tq]
            imp = pg if imp is None else imp + pg

        def phase(r):
            return jnp.concatenate([imp[q * span_rows + r * qsz:q * span_rows + (r + 1) * qsz, :]
                                    for q in range(n_spans)], axis=0)

        j = lax.broadcasted_iota(jnp.int32, (nj, tq), 0)
        p_slc = jnp.where(j == 0, 0.0, pltpu.roll(phase(SLC_RATIO - 1), 1, 0))
        for r in range(SLC_RATIO):
            p_slc = p_slc + phase(r)
        cur = (t0 + lax.broadcasted_iota(jnp.int32, (nj, tq), 1)) >> (SLC_BLOCK.bit_length() - 1)
        causal = j <= cur
        if nj < nsel:
            sel_ref[0, 0, nj:, :] = jnp.full((nsel - nj, tq), MASK_BIAS, sel_ref.dtype)

        @pl.when(t0 < top_k * SLC_BLOCK)
        def _():
            sel_ref[0, 0, :nj, :] = jnp.where(causal, 0.0, MASK_BIAS).astype(sel_ref.dtype)

        @pl.when(t0 >= top_k * SLC_BLOCK)
        def _():
            forced = (j == 0) | (j == cur) | (j == cur - 1)
            score = jnp.where(causal & jnp.logical_not(forced), p_slc, PICKED)
            jf = j.astype(F32)
            for _ in range(top_k - 3):
                m = jnp.max(score, axis=0, keepdims=True)
                first = jnp.min(jnp.where(score == m, jf, float(nsel)), axis=0, keepdims=True)
                score = jnp.where(jf == first, PICKED, score)
            sel_ref[0, 0, :nj, :] = jnp.where((score == PICKED) & causal, 0.0, MASK_BIAS).astype(sel_ref.dtype)

    last_span = (t0 + tq - 1) // (qsz * SLC_BLOCK)
    for n_spans in range(1, CMP_SPANS + 1):
        pl.when(last_span == n_spans - 1)(functools.partial(attend, n_spans))


def nsa_cmp_branch(qt, B, kc, vct, *, tq=512):
    S = qt.shape[2] // B
    Hkv, G, d = NSA_KV_HEADS, NSA_GROUP, NSA_HEAD_DIM
    ncp = kc.shape[2]
    nsel = ncp // SLC_RATIO
    top_k = min(SLC_TOPK, nsel)
    tq = min(tq, S)
    assert S % tq == 0 and nsel & (nsel - 1) == 0 and tq & (tq - 1) == 0
    assert (top_k * SLC_BLOCK) % tq == 0 and top_k >= 3
    assert nsel % (CMP_SPANS * BF16_SUBLANES) == 0 and (SLC_RATIO * nsel // CMP_SPANS) % LANE == 0
    assert (nsel // CMP_SPANS * SLC_BLOCK) % tq == 0
    kern = functools.partial(_nsa_cmp_kernel, top_k=top_k)
    return pl.pallas_call(
        kern,
        grid=(B, Hkv, S // tq),
        in_specs=[
            pl.BlockSpec((G, d, tq), lambda b, h, i: (h, 0, b * (S // tq) + i)),
            pl.BlockSpec((1, 1, ncp, d), lambda b, h, i: (b, h, 0, 0)),
            pl.BlockSpec((1, 1, d, ncp), lambda b, h, i: (b, h, 0, 0)),
        ],
        out_specs=[
            pl.BlockSpec((1, tq, G * d), lambda b, h, i: (b, i, h)),
            pl.BlockSpec((1, 1, nsel, tq), lambda b, h, i: (b, h, 0, i)),
        ],
        out_shape=[
            jax.ShapeDtypeStruct((B, S, Hkv * G * d), F32),
            jax.ShapeDtypeStruct((B, Hkv, nsel, S), MXU_DTYPE),
        ],
        scratch_shapes=[pltpu.VMEM((d, G * tq), MXU_DTYPE)],
        compiler_params=_cparams("arbitrary", "arbitrary", "arbitrary"),
        name="nsa_cmp",
    )(qt, kc, vct)


def _nsa_slc_kernel(q_ref, sel_ref, k_ref, vt_ref, o_ref, qa_ref, m_ref, acc_ref, s_ref, smax_ref, *, tk):
    i = pl.program_id(2)
    tq = q_ref.shape[2]
    d = NSA_HEAD_DIM
    G = NSA_GROUP
    nblk = sel_ref.shape[2]
    span = LANE * SLC_BLOCK
    n_halves = pl.cdiv(nblk, LANE)

    m_ref[...] = jnp.full_like(m_ref, PICKED)
    acc_ref[...] = jnp.zeros_like(acc_ref)
    for g in range(G):
        qa_ref[:d, g * tq:(g + 1) * tq] = q_ref[g]

    def scores(kt_idx):
        k0 = pl.multiple_of(kt_idx * tk, tk)
        return jnp.dot(k_ref[0, pl.ds(k0, tk), :], qa_ref[...], preferred_element_type=F32)

    def put_scores(slot, kt_idx):
        s = scores(kt_idx)
        s_ref[slot] = s
        smax_ref[slot] = jnp.max(s, axis=0, keepdims=True)

    def accumulate(kt_idx, s, smax):
        k0 = pl.multiple_of(kt_idx * tk, tk)
        m_old = m_ref[...]
        m_new = jnp.maximum(m_old, smax)
        alpha = jnp.exp2(m_old - m_new)
        p = jnp.exp2(s - m_new)
        acc_ref[...] = alpha * acc_ref[...] + jnp.dot(vt_ref[0, :, pl.ds(k0, tk)], p.astype(MXU_DTYPE),
                                                      preferred_element_type=F32)
        m_ref[...] = m_new

    last = ((i + 1) * tq - 1) // tk
    tiles_per_half = span // tk
    for hf in range(n_halves):
        lo = hf * tiles_per_half
        hi = jnp.minimum(last, lo + tiles_per_half)

        @pl.when(lo <= last)
        def _():
            sel_half = sel_ref[0, 0, hf * LANE:(hf + 1) * LANE, :]
            for g in range(G):
                qa_ref[d:, g * tq:(g + 1) * tq] = sel_half
            n = hi - lo
            odd = (n & 1) == 1
            diag_here = last < lo + tiles_per_half

            def accumulate_diagonal(s):
                col = lax.broadcasted_iota(jnp.int32, s.shape, 1)
                tok = i * tq + (col & (tq - 1))
                key = last * tk + lax.broadcasted_iota(jnp.int32, s.shape, 0)
                masked = jnp.where(key <= tok, s, NEG)
                accumulate(last, masked, jnp.max(masked, axis=0, keepdims=True))

            put_scores(0, lo)

            def pair(pi, carry):
                t = lo + 2 * pi
                put_scores(1, t + 1)
                accumulate(t, s_ref[0], smax_ref[0])
                put_scores(0, t + 2)
                accumulate(t + 1, s_ref[1], smax_ref[1])
                return carry

            lax.fori_loop(0, n >> 1, pair, 0)

            @pl.when(odd & diag_here)
            def _():
                put_scores(1, last)
                accumulate(hi - 1, s_ref[0], smax_ref[0])
                accumulate_diagonal(s_ref[1])

            @pl.when(odd & jnp.logical_not(diag_here))
            def _():
                accumulate(hi - 1, s_ref[0], smax_ref[0])

            @pl.when(jnp.logical_not(odd) & diag_here)
            def _():
                accumulate_diagonal(s_ref[0])

    out = acc_ref[:d, :] * (1.0 / acc_ref[d:d + 1, :])
    for g in range(G):
        o_ref[0, :, g * d:(g + 1) * d] = out[:, g * tq:(g + 1) * tq].T


def nsa_slc_branch(qt, B, sel, k_aug, vt, *, tq=512, tk=1024):
    S = qt.shape[2] // B
    Hkv, G, d = NSA_KV_HEADS, NSA_GROUP, NSA_HEAD_DIM
    nblk = sel.shape[2]
    dv = vt.shape[1]
    tq, tk = min(tq, S), min(tk, S)
    assert S % tq == 0 and S % tk == 0 and tk % tq == 0 and tq & (tq - 1) == 0
    assert nblk % LANE == 0 and (LANE * SLC_BLOCK) % tk == 0
    kern = functools.partial(_nsa_slc_kernel, tk=tk)
    return pl.pallas_call(
        kern,
        grid=(B, Hkv, S // tq),
        in_specs=[
            pl.BlockSpec((G, d, tq), lambda b, h, i: (h, 0, b * (S // tq) + i)),
            pl.BlockSpec((1, 1, nblk, tq), lambda b, h, i: (b, h, 0, i)),
            pl.BlockSpec((1, S, d + LANE), lambda b, h, i: (h, b, 0), pipeline_mode=pl.Buffered(1)),
            pl.BlockSpec((1, dv, S), lambda b, h, i: (h, 0, b), pipeline_mode=pl.Buffered(1)),
        ],
        out_specs=pl.BlockSpec((1, tq, G * d), lambda b, h, i: (b, i, h)),
        out_shape=jax.ShapeDtypeStruct((B, S, Hkv * G * d), F32),
        scratch_shapes=[
            pltpu.VMEM((d + LANE, G * tq), MXU_DTYPE),
            pltpu.VMEM((1, G * tq), F32),
            pltpu.VMEM((dv, G * tq), F32),
            pltpu.VMEM((2, tk, G * tq), F32),
            pltpu.VMEM((2, 1, G * tq), F32),
        ],
        compiler_params=_cparams("arbitrary", "arbitrary", "arbitrary"),
        name="nsa_slc",
    )(qt, sel, k_aug, vt)


def _nsa_win_kernel(q_ref, k_ref, vt_ref, bias_a_ref, bias_b_ref, o_ref, qt_ref, s_ref):
    i = pl.program_id(2)
    tq = q_ref.shape[2]
    th = tq // 2
    d = NSA_HEAD_DIM
    G = NSA_GROUP
    nk = WIN + th
    for hf in range(2):
        for g in range(G):
            qt_ref[hf, :, g * th:(g + 1) * th] = q_ref[g, :, hf * th:(hf + 1) * th]

    def start(hf):
        return pl.multiple_of(jnp.maximum(i * tq + hf * th - WIN, 0), th)

    def put_scores(hf, bias_ref):
        s_ref[hf] = (jnp.dot(k_ref[0, pl.ds(start(hf), nk), :], qt_ref[hf], preferred_element_type=F32)
                     + jnp.concatenate([bias_ref[0]] * G, axis=1))

    def finish(hf):
        s = s_ref[hf]
        e = jnp.exp2(s - jnp.max(s, axis=0, keepdims=True))
        p = e * (1.0 / jnp.sum(e, axis=0, keepdims=True))
        out_t = jnp.dot(vt_ref[0, :d, pl.ds(start(hf), nk)], p.astype(MXU_DTYPE), preferred_element_type=F32)
        for g in range(G):
            o_ref[0, hf * th:(hf + 1) * th, g * d:(g + 1) * d] = out_t[:, g * th:(g + 1) * th].T

    put_scores(0, bias_a_ref)
    put_scores(1, bias_b_ref)
    finish(0)
    finish(1)


def nsa_win_branch(qt, B, k_win, vt, *, tq=512):
    S = qt.shape[2] // B
    Hkv, G, d = NSA_KV_HEADS, NSA_GROUP, NSA_HEAD_DIM
    dv = vt.shape[1]
    tq = min(tq, S)
    th = tq // 2
    assert S % tq == 0 and th % LANE == 0 and th & (th - 1) == 0 and WIN % th == 0 and S >= WIN + th
    kern = _nsa_win_kernel
    n_clipped = WIN // th
    t0 = jnp.minimum(jnp.arange(n_clipped + 1) * th, WIN)[:, None, None]
    pos = (t0 - jnp.minimum(t0, WIN)) + jnp.arange(WIN + th)[None, :, None]
    tok = t0 + jnp.arange(th)[None, None, :]
    bias = jnp.where((pos <= tok) & (pos > tok - WIN), 0.0, NEG).astype(F32)
    return pl.pallas_call(
        kern,
        grid=(B, Hkv, S // tq),
        in_specs=[
            pl.BlockSpec((G, d, tq), lambda b, h, i: (h, 0, b * (S // tq) + i)),
            pl.BlockSpec((1, S, d), lambda b, h, i: (h, b, 0)),
            pl.BlockSpec((1, dv, S), lambda b, h, i: (Hkv + h, 0, b)),
            pl.BlockSpec((1, WIN + th, th), lambda b, h, i: (jnp.minimum(2 * i, n_clipped), 0, 0)),
            pl.BlockSpec((1, WIN + th, th), lambda b, h, i: (jnp.minimum(2 * i + 1, n_clipped), 0, 0)),
        ],
        out_specs=pl.BlockSpec((1, tq, G * d), lambda b, h, i: (b, i, h)),
        out_shape=jax.ShapeDtypeStruct((B, S, Hkv * G * d), F32),
        scratch_shapes=[pltpu.VMEM((2, d, G * th), MXU_DTYPE), pltpu.VMEM((2, WIN + th, G * th), F32)],
        compiler_params=_cparams("arbitrary", "arbitrary", "arbitrary"),
        name="nsa_win",
    )(qt, k_win, vt, bias, bias)


def _nsa_out_kernel(oc_ref, os_ref, ow_ref, gate_ref, w_ref, h_ref, o_ref, merged_ref):
    d = NSA_HEAD_DIM
    gates = _sigmoid(gate_ref[...])
    for hq in range(NSA_Q_HEADS):
        cols = slice(hq * d, (hq + 1) * d)
        c = N_BRANCH * hq
        merged = (gates[:, c:c + 1] * oc_ref[:, cols] + gates[:, c + 1:c + 2] * os_ref[:, cols]
                  + gates[:, c + 2:c + 3] * ow_ref[:, cols])
        merged_ref[:, cols] = merged.astype(merged_ref.dtype)
    o_ref[...] = h_ref[...] + jnp.dot(merged_ref[...], w_ref[...], preferred_element_type=F32)


def nsa_out(o_cmp, o_slc, o_win, gate_logits, w_o, h, *, tm=256):
    T, HD = o_cmp.shape
    D = w_o.shape[1]
    tm = min(tm, T)
    assert T % tm == 0 and gate_logits.shape == (T, LANE)
    row = lambda i: (i, 0)
    return pl.pallas_call(
        _nsa_out_kernel,
        grid=(T // tm,),
        in_specs=[
            pl.BlockSpec((tm, HD), row),
            pl.BlockSpec((tm, HD), row),
            pl.BlockSpec((tm, HD), row),
            pl.BlockSpec((tm, LANE), row),
            pl.BlockSpec((HD, D), lambda i: (0, 0)),
            pl.BlockSpec((tm, D), row),
        ],
        out_specs=pl.BlockSpec((tm, D), row),
        out_shape=jax.ShapeDtypeStruct((T, D), F32),
        scratch_shapes=[pltpu.VMEM((tm, HD), w_o.dtype)],
        compiler_params=_cparams("arbitrary"),
        name="nsa_out",
    )(o_cmp, o_slc, o_win, gate_logits, w_o, h)


def _span_major(x, nsel):
    B, n, H, d = x.shape
    qsz = nsel // CMP_SPANS
    return x.reshape(B, CMP_SPANS, qsz, SLC_RATIO, H, d).transpose(0, 4, 1, 3, 2, 5).reshape(B, H, n, d)


def _cast(w):
    return w.astype(MXU_DTYPE)


def nsa_shared_kv(h, B, S, kv_gain, w_kv, cmp_pe_k, cmp_w1_k, cmp_w2_k, cmp_pe_v, cmp_w1_v, cmp_w2_v):
    Hkv, d = NSA_KV_HEADS, NSA_HEAD_DIM
    cmp_in, k_slc, k_win, vt = kv_projection(h, S, kv_gain, _cast(w_kv))
    n = S // CMP_STRIDE
    nsel = S // SLC_BLOCK
    cmp = compress_tokens(cmp_in, B, jnp.stack([cmp_pe_k, cmp_pe_v]), _cast(jnp.stack([cmp_w1_k, cmp_w1_v])),
                          _cast(jnp.stack([cmp_w2_k, cmp_w2_v])))
    cmp = cmp.reshape(2, B, Hkv, n, d).transpose(0, 1, 3, 2, 4)
    kc = _cast(_span_major(cmp[0], nsel))
    vct = _cast(_span_major(cmp[1], nsel).transpose(0, 1, 3, 2))
    return kc, vct, k_slc, k_win, vt


def nsa_attention_block(h, B, S, mix_gain, w_q, w_o, shared):
    kc, vct, k_slc, k_win, vt = shared
    T = B * S
    nq = NSA_Q_HEADS * NSA_HEAD_DIM
    nqp = nq + LANE
    w_q = jnp.pad(w_q, ((0, 0), (0, nqp - w_q.shape[1])))
    qt, gate_logits = q_projection(h, mix_gain, _cast(w_q))
    o_cmp, sel = nsa_cmp_branch(qt, B, kc, vct)
    o_slc = nsa_slc_branch(qt, B, sel, k_slc, vt)
    o_win = nsa_win_branch(qt, B, k_win, vt)
    return nsa_out(o_cmp.reshape(T, nq), o_slc.reshape(T, nq), o_win.reshape(T, nq), gate_logits, _cast(w_o), h)


def nsa_mixer(h, B, S, mix_gain, kv_gain, w_kv, cmp_pe_k, cmp_w1_k, cmp_w2_k, cmp_pe_v, cmp_w1_v, cmp_w2_v, w_q, w_o):
    shared = nsa_shared_kv(h, B, S, kv_gain, w_kv, cmp_pe_k, cmp_w1_k, cmp_w2_k, cmp_pe_v, cmp_w1_v, cmp_w2_v)
    return nsa_attention_block(h, B, S, mix_gain, w_q, w_o, shared)


def kernel(x, norm_mix_gain, norm_ffn_gain, ret_w_in, ret_gn_gain, ret_w_out, nsa_kv_norm_gain, nsa_w_kv, cmp_pe_k, cmp_w1_k, cmp_w2_k, cmp_pe_v, cmp_w1_v, cmp_w2_v, nsa_w_q, nsa_w_o, ffn_w_in, ffn_conv_w, ffn_conv_b, ffn_w_out, final_norm_gain):
    B, S, D = x.shape
    T = B * S
    depth = norm_mix_gain.shape[0]
    n_a = depth // 2
    h = x.reshape(T, D)
    shared = None
    for layer in range(depth):
        if layer < n_a:
            proj = norm_matmul(h, norm_mix_gain[layer], _cast(ret_w_in[layer]), tm=1024, tn=1024)
            y = retention_core(proj.reshape(B, S, -1), ret_gn_gain[layer])
            h = matmul_residual(y.reshape(T, -1), _cast(ret_w_out[layer]), h, tm=512, tn=1024)
        else:
            if layer == n_a:
                shared = nsa_shared_kv(h, B, S, nsa_kv_norm_gain, nsa_w_kv, cmp_pe_k, cmp_w1_k, cmp_w2_k,
                                       cmp_pe_v, cmp_w1_v, cmp_w2_v)
            b = layer - n_a
            h = nsa_attention_block(h, B, S, norm_mix_gain[layer], nsa_w_q[b], nsa_w_o[b], shared)
        h = conv_ffn_block(h, S, norm_ffn_gain[layer], _cast(ffn_w_in[layer]), ffn_conv_w[layer], ffn_conv_b[layer],
                           _cast(ffn_w_out[layer]), final_norm_gain, final_norm=(layer == depth - 1))
    return h.reshape(B, S, D)
```

```python
import functools

import jax
import jax.numpy as jnp
from jax import lax
from jax.experimental import pallas as pl
from jax.experimental.pallas import tpu as pltpu

F32 = jnp.float32
MXU_DTYPE = jnp.bfloat16

RMS_EPS = 1e-6
ROPE_BASE = 10000.0

RET_HEADS = 8
RET_CHUNK = 128

NSA_Q_HEADS = 16
NSA_KV_HEADS = 4
NSA_GROUP = NSA_Q_HEADS // NSA_KV_HEADS
NSA_HEAD_DIM = 128
N_BRANCH = 3
CMP_BLOCK = 32
CMP_STRIDE = 16
CMP_RATIO = CMP_BLOCK // CMP_STRIDE
SLC_BLOCK = 64
SLC_TOPK = 16
SLC_RATIO = SLC_BLOCK // CMP_STRIDE
CMP_SPANS = 8
WIN = 512
CONV_W = 3

NEG = -1e30
MASK_BIAS = -(2.0 ** 100)
PICKED = -3.0e38
SEL_FORCE = 1e30
LOG2_E = 1.4426950408889634

LANE = 128
BF16_SUBLANES = 16
VMEM_LIMIT_BYTES = 56 * 1024 * 1024


def _cparams(*sem):
    return pltpu.CompilerParams(dimension_semantics=sem, vmem_limit_bytes=VMEM_LIMIT_BYTES)


def _sigmoid(x):
    return 1.0 / (1.0 + jnp.exp(-x))


def _rms_normalize(x, gain):
    ms = jnp.mean(x * x, axis=-1, keepdims=True)
    return x * lax.rsqrt(ms + RMS_EPS) * gain


def _norm_matmul_kernel(x_ref, g_ref, w_ref, o_ref, xn_ref):
    @pl.when(pl.program_id(1) == 0)
    def _():
        xn_ref[...] = _rms_normalize(x_ref[...], g_ref[...]).astype(xn_ref.dtype)

    o_ref[...] = jnp.dot(xn_ref[...], w_ref[...], preferred_element_type=F32).astype(o_ref.dtype)


def norm_matmul(h, gain, w, *, tm, tn, out_dtype=F32):
    T, D = h.shape
    N = w.shape[1]
    tm, tn = min(tm, T), min(tn, N)
    assert T % tm == 0 and N % tn == 0
    return pl.pallas_call(
        _norm_matmul_kernel,
        grid=(T // tm, N // tn),
        in_specs=[
            pl.BlockSpec((tm, D), lambda i, j: (i, 0)),
            pl.BlockSpec((1, D), lambda i, j: (0, 0)),
            pl.BlockSpec((D, tn), lambda i, j: (0, j)),
        ],
        out_specs=pl.BlockSpec((tm, tn), lambda i, j: (i, j)),
        out_shape=jax.ShapeDtypeStruct((T, N), out_dtype),
        scratch_shapes=[pltpu.VMEM((tm, D), w.dtype)],
        compiler_params=_cparams("arbitrary", "arbitrary"),
        name="norm_matmul",
    )(h, gain.reshape(1, D), w)


def _matmul_residual_kernel(y_ref, w_ref, h_ref, o_ref):
    o_ref[...] = h_ref[...] + jnp.dot(y_ref[...], w_ref[...], preferred_element_type=F32)


def matmul_residual(y, w, h, *, tm, tn):
    T, K = y.shape
    N = w.shape[1]
    tm, tn = min(tm, T), min(tn, N)
    assert T % tm == 0 and N % tn == 0
    return pl.pallas_call(
        _matmul_residual_kernel,
        grid=(T // tm, N // tn),
        in_specs=[
            pl.BlockSpec((tm, K), lambda i, j: (i, 0)),
            pl.BlockSpec((K, tn), lambda i, j: (0, j)),
            pl.BlockSpec((tm, tn), lambda i, j: (i, j)),
        ],
        out_specs=pl.BlockSpec((tm, tn), lambda i, j: (i, j)),
        out_shape=jax.ShapeDtypeStruct((T, N), F32),
        compiler_params=_cparams("arbitrary", "arbitrary"),
        name="matmul_residual",
    )(y, w, h)


def _retention_kernel(q_ref, k_ref, v_ref, g_ref, cos_ref, sin_ref, dmask_ref, qdec_ref, kdec_ref, gn_ref,
                      o_ref, state_ref, *, chunk, n_chunks, k_scale):
    @pl.when(pl.program_id(2) == 0)
    def _():
        state_ref[...] = jnp.zeros_like(state_ref)

    half = q_ref.shape[-1] // 2
    dmask = dmask_ref[0]
    qdec = qdec_ref[0]
    kdec = kdec_ref[0]
    chunk_decay = qdec[chunk - 1:chunk, :]
    gn = gn_ref[...]

    def rotate(x, cos, sin):
        x1, x2 = x[:, :half], x[:, half:]
        return jnp.concatenate([x1 * cos - x2 * sin, x1 * sin + x2 * cos], axis=-1)

    for ci in range(n_chunks):
        rows = pl.ds(ci * chunk, chunk)
        cos, sin = cos_ref[rows, :], sin_ref[rows, :]
        q = rotate(q_ref[0, rows, :], cos, sin)
        k = rotate(k_ref[0, rows, :], cos, sin) * k_scale
        v = v_ref[0, rows, :].astype(MXU_DTYPE)
        qm = q.astype(MXU_DTYPE)
        scores = lax.dot_general(qm, k.astype(MXU_DTYPE), (((1,), (1,)), ((), ())),
                                 preferred_element_type=F32) * dmask
        inner = jnp.dot(scores.astype(MXU_DTYPE), v, preferred_element_type=F32)
        state = state_ref[...]
        cross = jnp.dot(qm, state.astype(MXU_DTYPE), preferred_element_type=F32) * qdec
        k_dec_t = (k * kdec).T.astype(MXU_DTYPE)
        state_ref[...] = state * chunk_decay + jnp.dot(k_dec_t, v, preferred_element_type=F32)
        out = inner + cross
        mu = jnp.mean(out, axis=-1, keepdims=True)
        cen = out - mu
        var = jnp.mean(cen * cen, axis=-1, keepdims=True)
        normed = cen * lax.rsqrt(var + RMS_EPS) * gn
        g = g_ref[0, rows, :]
        o_ref[0, rows, :] = ((g * _sigmoid(g)) * normed).astype(o_ref.dtype)


def retention_core(proj, gn_gain, *, chunk=2 * RET_CHUNK, tokens_per_step=2048):
    B, S, P = proj.shape
    H = RET_HEADS
    dk = P // (6 * H)
    dv = 2 * dk
    tc = min(tokens_per_step, S)
    assert S % tc == 0 and tc % chunk == 0
    half = dk // 2
    pos = jnp.arange(S, dtype=F32)
    freqs = ROPE_BASE ** (-jnp.arange(half, dtype=F32) / half)
    ang = pos[:, None] * freqs[None, :]
    cos, sin = jnp.cos(ang), jnp.sin(ang)
    log_gamma = jnp.log(1.0 - 2.0 ** (-5.0 - jnp.arange(H, dtype=F32)))
    idx = jnp.arange(chunk, dtype=F32)
    diff = idx[:, None] - idx[None, :]
    dmask = jnp.where(diff >= 0, jnp.exp(jnp.maximum(diff, 0.0)[None] * log_gamma[:, None, None]), 0.0)
    qdec = jnp.exp((idx[None, :] + 1.0) * log_gamma[:, None])[:, :, None]
    kdec = jnp.exp((chunk - 1.0 - idx[None, :]) * log_gamma[:, None])[:, :, None]
    kern = functools.partial(_retention_kernel, chunk=chunk, n_chunks=tc // chunk, k_scale=dk ** -0.5)
    k_off, v_off, g_off = (H * dk) // dk, (2 * H * dk) // dv, (2 * H * dk + H * dv) // dv
    return pl.pallas_call(
        kern,
        grid=(B, H, S // tc),
        in_specs=[
            pl.BlockSpec((1, tc, dk), lambda b, h, c: (b, c, h)),
            pl.BlockSpec((1, tc, dk), lambda b, h, c: (b, c, k_off + h)),
            pl.BlockSpec((1, tc, dv), lambda b, h, c: (b, c, v_off + h)),
            pl.BlockSpec((1, tc, dv), lambda b, h, c: (b, c, g_off + h)),
            pl.BlockSpec((tc, half), lambda b, h, c: (c, 0)),
            pl.BlockSpec((tc, half), lambda b, h, c: (c, 0)),
            pl.BlockSpec((1, chunk, chunk), lambda b, h, c: (h, 0, 0)),
            pl.BlockSpec((1, chunk, 1), lambda b, h, c: (h, 0, 0)),
            pl.BlockSpec((1, chunk, 1), lambda b, h, c: (h, 0, 0)),
            pl.BlockSpec((1, dv), lambda b, h, c: (0, h)),
        ],
        out_specs=pl.BlockSpec((1, tc, dv), lambda b, h, c: (b, c, h)),
        out_shape=jax.ShapeDtypeStruct((B, S, H * dv), MXU_DTYPE),
        scratch_shapes=[pltpu.VMEM((dk, dv), F32)],
        compiler_params=_cparams("arbitrary", "arbitrary", "arbitrary"),
        name="retention_core",
    )(proj, proj, proj, proj, cos, sin, dmask, qdec, kdec, gn_gain.reshape(1, H * dv))


def _conv_ffn_kernel(x_ref, g_ref, wa_ref, wb_ref, cwa_ref, cwb_ref, cba_ref, cbb_ref, wo_ref, fg_ref,
                     o_ref, xn_ref, acc_ref, carry_a_ref, carry_b_ref, *, tiles_per_seq, final_norm):
    i, j = pl.program_id(0), pl.program_id(1)
    tm = x_ref.shape[0]

    @pl.when((i == 0) & (j == 0))
    def _():
        carry_a_ref[...] = jnp.zeros_like(carry_a_ref)
        carry_b_ref[...] = jnp.zeros_like(carry_b_ref)

    @pl.when(j == 0)
    def _():
        xn_ref[...] = _rms_normalize(x_ref[...], g_ref[...]).astype(xn_ref.dtype)
        acc_ref[...] = jnp.zeros_like(acc_ref)

    seq_start = (i % tiles_per_seq) == 0

    def causal_conv(w_ref, cw_ref, cb_ref, carry_ref):
        u = jnp.dot(xn_ref[...], w_ref[...], preferred_element_type=F32)
        prev = jnp.where(seq_start, 0.0, carry_ref[j])
        carry_ref[j] = u[tm - 8:, :]
        row = lax.broadcasted_iota(jnp.int32, u.shape, 0)
        u1 = jnp.where(row == 0, prev[7:8, :], pltpu.roll(u, 1, 0))
        u2 = jnp.where(row == 0, prev[6:7, :], jnp.where(row == 1, prev[7:8, :], pltpu.roll(u, 2, 0)))
        cw = cw_ref[...]
        return ((cb_ref[...] + u2 * cw[0:1, :]) + u1 * cw[1:2, :]) + u * cw[2:3, :]

    a = causal_conv(wa_ref, cwa_ref, cba_ref, carry_a_ref)
    b = causal_conv(wb_ref, cwb_ref, cbb_ref, carry_b_ref)
    act = ((a * _sigmoid(a)) * b).astype(wo_ref.dtype)
    acc_ref[...] += jnp.dot(act, wo_ref[...], preferred_element_type=F32)

    @pl.when(j == pl.num_programs(1) - 1)
    def _():
        out = x_ref[...] + acc_ref[...]
        if final_norm:
            out = _rms_normalize(out, fg_ref[...])
        o_ref[...] = out


def conv_ffn_block(h, seq_len, norm_gain, w_in, conv_w, conv_b, w_out, final_gain, *, final_norm, tm=512, tn=512):
    T, D = h.shape
    F = w_out.shape[0]
    tm, tn = min(tm, seq_len), min(tn, F)
    assert seq_len % tm == 0 and F % tn == 0 and tm >= 8
    nj = F // tn
    kern = functools.partial(_conv_ffn_kernel, tiles_per_seq=seq_len // tm, final_norm=final_norm)
    return pl.pallas_call(
        kern,
        grid=(T // tm, nj),
        in_specs=[
            pl.BlockSpec((tm, D), lambda i, j: (i, 0)),
            pl.BlockSpec((1, D), lambda i, j: (0, 0)),
            pl.BlockSpec((D, tn), lambda i, j: (0, j)),
            pl.BlockSpec((D, tn), lambda i, j: (0, nj + j)),
            pl.BlockSpec((CONV_W, tn), lambda i, j: (0, j)),
            pl.BlockSpec((CONV_W, tn), lambda i, j: (0, nj + j)),
            pl.BlockSpec((1, tn), lambda i, j: (0, j)),
            pl.BlockSpec((1, tn), lambda i, j: (0, nj + j)),
            pl.BlockSpec((tn, D), lambda i, j: (j, 0)),
            pl.BlockSpec((1, D), lambda i, j: (0, 0)),
        ],
        out_specs=pl.BlockSpec((tm, D), lambda i, j: (i, 0)),
        out_shape=jax.ShapeDtypeStruct((T, D), F32),
        scratch_shapes=[
            pltpu.VMEM((tm, D), w_in.dtype),
            pltpu.VMEM((tm, D), F32),
            pltpu.VMEM((nj, 8, tn), F32),
            pltpu.VMEM((nj, 8, tn), F32),
        ],
        compiler_params=_cparams("arbitrary", "arbitrary"),
        name="conv_ffn",
    )(h, norm_gain.reshape(1, D), w_in, w_in, conv_w, conv_w, conv_b.reshape(1, 2 * F), conv_b.reshape(1, 2 * F),
      w_out, final_gain.reshape(1, D))


def _kv_proj_kernel(x_ref, g_ref, w_ref, cmp_ref, kslc_ref, kwin_ref, vt_ref, *, seq_len):
    tm = x_ref.shape[0]
    d = NSA_HEAD_DIM
    Hkv = NSA_KV_HEADS
    xn = _rms_normalize(x_ref[...], g_ref[...]).astype(w_ref.dtype)
    res = jnp.dot(xn, w_ref[...], preferred_element_type=F32)
    col = lambda branch, hh: res[:, (branch * Hkv + hh) * d:(branch * Hkv + hh + 1) * d]
    pos = (pl.program_id(0) * tm) % seq_len + lax.broadcasted_iota(jnp.int32, (tm, LANE), 0)
    blk = (pos >> (SLC_BLOCK.bit_length() - 1)) & (LANE - 1)
    onehot = jnp.where(blk == lax.broadcasted_iota(jnp.int32, (tm, LANE), 1), 1.0, 0.0).astype(kslc_ref.dtype)
    ones = jnp.ones((BF16_SUBLANES, tm), vt_ref.dtype)
    for hh in range(Hkv):
        cmp_ref[hh] = col(0, hh)
        cmp_ref[Hkv + hh] = col(1, hh)
        kslc_ref[hh, :, :d] = col(2, hh).astype(kslc_ref.dtype)
        kslc_ref[hh, :, d:] = onehot
        kwin_ref[hh] = col(4, hh).astype(kwin_ref.dtype)
        vt_ref[hh, :d, :] = col(3, hh).T.astype(vt_ref.dtype)
        vt_ref[hh, d:, :] = ones
        vt_ref[Hkv + hh, :d, :] = col(5, hh).T.astype(vt_ref.dtype)
        vt_ref[Hkv + hh, d:, :] = ones


def kv_projection(h, seq_len, gain, w_kv, *, tm=512):
    T, D = h.shape
    Hkv, d = NSA_KV_HEADS, NSA_HEAD_DIM
    N = w_kv.shape[1]
    tm = min(tm, seq_len)
    assert seq_len % tm == 0 and N == 2 * N_BRANCH * Hkv * d and tm % LANE == 0
    row3 = lambda i: (0, i, 0)
    return pl.pallas_call(
        functools.partial(_kv_proj_kernel, seq_len=seq_len),
        grid=(T // tm,),
        in_specs=[
            pl.BlockSpec((tm, D), lambda i: (i, 0)),
            pl.BlockSpec((1, D), lambda i: (0, 0)),
            pl.BlockSpec((D, N), lambda i: (0, 0)),
        ],
        out_specs=[
            pl.BlockSpec((2 * Hkv, tm, d), row3),
            pl.BlockSpec((Hkv, tm, d + LANE), row3),
            pl.BlockSpec((Hkv, tm, d), row3),
            pl.BlockSpec((2 * Hkv, d + BF16_SUBLANES, tm), lambda i: (0, 0, i)),
        ],
        out_shape=[
            jax.ShapeDtypeStruct((2 * Hkv, T, d), F32),
            jax.ShapeDtypeStruct((Hkv, T, d + LANE), MXU_DTYPE),
            jax.ShapeDtypeStruct((Hkv, T, d), MXU_DTYPE),
            jax.ShapeDtypeStruct((2 * Hkv, d + BF16_SUBLANES, T), MXU_DTYPE),
        ],
        compiler_params=_cparams("arbitrary"),
        name="kv_projection",
    )(h, gain.reshape(1, D), w_kv)


def _q_proj_kernel(x_ref, g_ref, w_ref, qt_ref, gate_ref, *, q_scale):
    d = NSA_HEAD_DIM
    xn = _rms_normalize(x_ref[...], g_ref[...]).astype(w_ref.dtype)
    res = jnp.dot(xn, w_ref[...], preferred_element_type=F32)
    for hq in range(NSA_Q_HEADS):
        qt_ref[hq] = (res[:, hq * d:(hq + 1) * d] * q_scale).T.astype(qt_ref.dtype)
    gate_ref[...] = res[:, NSA_Q_HEADS * d:]


def q_projection(h, gain, w_q, *, tm=512):
    T, D = h.shape
    d = NSA_HEAD_DIM
    N = w_q.shape[1]
    tm = min(tm, T)
    assert T % tm == 0 and N == NSA_Q_HEADS * d + LANE and tm % LANE == 0
    return pl.pallas_call(
        functools.partial(_q_proj_kernel, q_scale=d ** -0.5 * LOG2_E),
        grid=(T // tm,),
        in_specs=[
            pl.BlockSpec((tm, D), lambda i: (i, 0)),
            pl.BlockSpec((1, D), lambda i: (0, 0)),
            pl.BlockSpec((D, N), lambda i: (0, 0)),
        ],
        out_specs=[
            pl.BlockSpec((NSA_Q_HEADS, d, tm), lambda i: (0, 0, i)),
            pl.BlockSpec((tm, LANE), lambda i: (i, 0)),
        ],
        out_shape=[
            jax.ShapeDtypeStruct((NSA_Q_HEADS, d, T), MXU_DTYPE),
            jax.ShapeDtypeStruct((T, LANE), F32),
        ],
        compiler_params=_cparams("arbitrary"),
        name="q_projection",
    )(h, gain.reshape(1, D), w_q)


def _compress_kernel(x_ref, pe_ref, w1_ref, w2_ref, o_ref):
    d = x_ref.shape[2]
    n = o_ref.shape[2]
    first = jnp.zeros((n, w1_ref.shape[2]), F32)
    second = jnp.zeros((n, w1_ref.shape[2]), F32)
    for p in range(CMP_STRIDE):
        xp = x_ref[0, pl.ds(p, n, stride=CMP_STRIDE), :]
        q = CMP_STRIDE + p
        first = first + jnp.dot((xp + pe_ref[0, p:p + 1, :]).astype(MXU_DTYPE), w1_ref[0, p * d:(p + 1) * d, :],
                                preferred_element_type=F32)
        second = second + jnp.dot((xp + pe_ref[0, q:q + 1, :]).astype(MXU_DTYPE), w1_ref[0, q * d:(q + 1) * d, :],
                                  preferred_element_type=F32)
    hid = first + pltpu.roll(second, n - 1, 0)
    gelu = 0.5 * hid * (1.0 + jnp.tanh(0.7978845608028654 * (hid + 0.044715 * (hid * hid * hid))))
    out = jnp.dot(gelu.astype(MXU_DTYPE), w2_ref[0], preferred_element_type=F32)
    row = lax.broadcasted_iota(jnp.int32, out.shape, 0)
    o_ref[0, 0] = jnp.where(row < n - 1, out, 0.0)


def compress_tokens(x, batch, pe, w1, w2):
    H2, T, d = x.shape
    Hkv = H2 // 2
    S = T // batch
    n = S // CMP_STRIDE
    hid = w2.shape[1]
    return pl.pallas_call(
        _compress_kernel,
        grid=(2, batch, Hkv),
        in_specs=[
            pl.BlockSpec((1, S, d), lambda s, b, hh: (s * Hkv + hh, b, 0)),
            pl.BlockSpec((1, CMP_BLOCK, d), lambda s, b, hh: (s, 0, 0)),
            pl.BlockSpec((1, CMP_BLOCK * d, hid), lambda s, b, hh: (s, 0, 0)),
            pl.BlockSpec((1, hid, d), lambda s, b, hh: (s, 0, 0)),
        ],
        out_specs=pl.BlockSpec((1, 1, n, d), lambda s, b, hh: (s, b * Hkv + hh, 0, 0)),
        out_shape=jax.ShapeDtypeStruct((2, batch * Hkv, n, d), F32),
        compiler_params=_cparams("arbitrary", "arbitrary", "arbitrary"),
        name="compress_tokens",
    )(x, pe, w1, w2)


def _nsa_cmp_kernel(q_ref, kc_ref, vct_ref, o_ref, sel_ref, qt_ref, *, top_k):
    i = pl.program_id(2)
    tq = q_ref.shape[2]
    d = NSA_HEAD_DIM
    G = NSA_GROUP
    nsel = kc_ref.shape[2] // SLC_RATIO
    qsz = nsel // CMP_SPANS
    span_rows = SLC_RATIO * qsz
    t0 = i * tq
    for g in range(G):
        qt_ref[:, g * tq:(g + 1) * tq] = q_ref[g]

    def attend(n_spans):
        nrows, nj = n_spans * span_rows, n_spans * qsz
        row = lax.broadcasted_iota(jnp.int32, (nrows, tq), 0)
        tok = t0 + lax.broadcasted_iota(jnp.int32, (nrows, tq), 1)
        blk = (row >> (span_rows.bit_length() - 1)) * qsz + (row & (qsz - 1))
        cblk = SLC_RATIO * blk + ((row >> (qsz.bit_length() - 1)) & (SLC_RATIO - 1))
        bias = jnp.where(cblk * CMP_STRIDE + (CMP_BLOCK - 1) <= tok, 0.0, NEG)
        s = (jnp.dot(kc_ref[0, 0, :nrows, :], qt_ref[...], preferred_element_type=F32)
             + jnp.concatenate([bias] * G, axis=1))
        e = jnp.exp2(s - jnp.max(s, axis=0, keepdims=True))
        denom = jnp.sum(e, axis=0, keepdims=True)
        any_valid = jnp.concatenate([tok[0:1, :] >= CMP_BLOCK - 1] * G, axis=1)
        p = e * jnp.where(any_valid, 1.0 / denom, 0.0)
        out_t = jnp.dot(vct_ref[0, 0, :, :nrows], p.astype(MXU_DTYPE), preferred_element_type=F32)
        imp = None
        for g in range(G):
            o_ref[0, :, g * d:(g + 1) * d] = out_t[:, g * tq:(g + 1) * tq].T
            pg = p[:, g * tq:(g + 1) * tq]
            imp = pg if imp is None else imp + pg

        def phase(r):
            return jnp.concatenate([imp[q * span_rows + r * qsz:q * span_rows + (r + 1) * qsz, :]
                                    for q in range(n_spans)], axis=0)

        j = lax.broadcasted_iota(jnp.int32, (nj, tq), 0)
        p_slc = jnp.where(j == 0, 0.0, pltpu.roll(phase(SLC_RATIO - 1), 1, 0))
        for r in range(SLC_RATIO):
            p_slc = p_slc + phase(r)
        cur = (t0 + lax.broadcasted_iota(jnp.int32, (nj, tq), 1)) >> (SLC_BLOCK.bit_length() - 1)
        causal = j <= cur
        if nj < nsel:
            sel_ref[0, 0, nj:, :] = jnp.full((nsel - nj, tq), MASK_BIAS, sel_ref.dtype)

        @pl.when(t0 < top_k * SLC_BLOCK)
        def _():
            sel_ref[0, 0, :nj, :] = jnp.where(causal, 0.0, MASK_BIAS).astype(sel_ref.dtype)

        @pl.when(t0 >= top_k * SLC_BLOCK)
        def _():
            forced = (j == 0) | (j == cur) | (j == cur - 1)
            score = jnp.where(causal & jnp.logical_not(forced), p_slc, PICKED)
            jf = j.astype(F32)
            for _ in range(top_k - 3):
                m = jnp.max(score, axis=0, keepdims=True)
                first = jnp.min(jnp.where(score == m, jf, float(nsel)), axis=0, keepdims=True)
                score = jnp.where(jf == first, PICKED, score)
            sel_ref[0, 0, :nj, :] = jnp.where((score == PICKED) & causal, 0.0, MASK_BIAS).astype(sel_ref.dtype)

    last_span = (t0 + tq - 1) // (qsz * SLC_BLOCK)
    for n_spans in range(1, CMP_SPANS + 1):
        pl.when(last_span == n_spans - 1)(functools.partial(attend, n_spans))


def nsa_cmp_branch(qt, B, kc, vct, *, tq=512):
    S = qt.shape[2] // B
    Hkv, G, d = NSA_KV_HEADS, NSA_GROUP, NSA_HEAD_DIM
    ncp = kc.shape[2]
    nsel = ncp // SLC_RATIO
    top_k = min(SLC_TOPK, nsel)
    tq = min(tq, S)
    assert S % tq == 0 and nsel & (nsel - 1) == 0 and tq & (tq - 1) == 0
    assert (top_k * SLC_BLOCK) % tq == 0 and top_k >= 3
    assert nsel % (CMP_SPANS * BF16_SUBLANES) == 0 and (SLC_RATIO * nsel // CMP_SPANS) % LANE == 0
    assert (nsel // CMP_SPANS * SLC_BLOCK) % tq == 0
    kern = functools.partial(_nsa_cmp_kernel, top_k=top_k)
    return pl.pallas_call(
        kern,
        grid=(B, Hkv, S // tq),
        in_specs=[
            pl.BlockSpec((G, d, tq), lambda b, h, i: (h, 0, b * (S // tq) + i)),
            pl.BlockSpec((1, 1, ncp, d), lambda b, h, i: (b, h, 0, 0)),
            pl.BlockSpec((1, 1, d, ncp), lambda b, h, i: (b, h, 0, 0)),
        ],
        out_specs=[
            pl.BlockSpec((1, tq, G * d), lambda b, h, i: (b, i, h)),
            pl.BlockSpec((1, 1, nsel, tq), lambda b, h, i: (b, h, 0, i)),
        ],
        out_shape=[
            jax.ShapeDtypeStruct((B, S, Hkv * G * d), F32),
            jax.ShapeDtypeStruct((B, Hkv, nsel, S), MXU_DTYPE),
        ],
        scratch_shapes=[pltpu.VMEM((d, G * tq), MXU_DTYPE)],
        compiler_params=_cparams("arbitrary", "arbitrary", "arbitrary"),
        name="nsa_cmp",
    )(qt, kc, vct)


def _nsa_slc_kernel(q_ref, sel_ref, k_ref, vt_ref, o_ref, qa_ref, m_ref, acc_ref, s_ref, smax_ref, *, tk):
    i = pl.program_id(2)
    tq = q_ref.shape[2]
    d = NSA_HEAD_DIM
    G = NSA_GROUP
    nblk = sel_ref.shape[2]
    span = LANE * SLC_BLOCK
    n_halves = pl.cdiv(nblk, LANE)

    m_ref[...] = jnp.full_like(m_ref, PICKED)
    acc_ref[...] = jnp.zeros_like(acc_ref)
    for g in range(G):
        qa_ref[:d, g * tq:(g + 1) * tq] = q_ref[g]

    def scores(kt_idx):
        k0 = pl.multiple_of(kt_idx * tk, tk)
        return jnp.dot(k_ref[0, pl.ds(k0, tk), :], qa_ref[...], preferred_element_type=F32)

    def put_scores(slot, kt_idx):
        s = scores(kt_idx)
        s_ref[slot] = s
        smax_ref[slot] = jnp.max(s, axis=0, keepdims=True)

    def accumulate(kt_idx, s, smax):
        k0 = pl.multiple_of(kt_idx * tk, tk)
        m_old = m_ref[...]
        m_new = jnp.maximum(m_old, smax)
        alpha = jnp.exp2(m_old - m_new)
        p = jnp.exp2(s - m_new)
        acc_ref[...] = alpha * acc_ref[...] + jnp.dot(vt_ref[0, :, pl.ds(k0, tk)], p.astype(MXU_DTYPE),
                                                      preferred_element_type=F32)
        m_ref[...] = m_new

    last = ((i + 1) * tq - 1) // tk
    tiles_per_half = span // tk
    for hf in range(n_halves):
        lo = hf * tiles_per_half
        hi = jnp.minimum(last, lo + tiles_per_half)

        @pl.when(lo <= last)
        def _():
            sel_half = sel_ref[0, 0, hf * LANE:(hf + 1) * LANE, :]
            for g in range(G):
                qa_ref[d:, g * tq:(g + 1) * tq] = sel_half
            n = hi - lo
            odd = (n & 1) == 1
            diag_here = last < lo + tiles_per_half

            def accumulate_diagonal(s):
                col = lax.broadcasted_iota(jnp.int32, s.shape, 1)
                tok = i * tq + (col & (tq - 1))
                key = last * tk + lax.broadcasted_iota(jnp.int32, s.shape, 0)
                masked = jnp.where(key <= tok, s, NEG)
                accumulate(last, masked, jnp.max(masked, axis=0, keepdims=True))

            put_scores(0, lo)

            def pair(pi, carry):
                t = lo + 2 * pi
                put_scores(1, t + 1)
                accumulate(t, s_ref[0], smax_ref[0])
                put_scores(0, t + 2)
                accumulate(t + 1, s_ref[1], smax_ref[1])
                return carry

            lax.fori_loop(0, n >> 1, pair, 0)

            @pl.when(odd & diag_here)
            def _():
                put_scores(1, last)
                accumulate(hi - 1, s_ref[0], smax_ref[0])
                accumulate_diagonal(s_ref[1])

            @pl.when(odd & jnp.logical_not(diag_here))
            def _():
                accumulate(hi - 1, s_ref[0], smax_ref[0])

            @pl.when(jnp.logical_not(odd) & diag_here)
            def _():
                accumulate_diagonal(s_ref[0])

    out = acc_ref[:d, :] * (1.0 / acc_ref[d:d + 1, :])
    for g in range(G):
        o_ref[0, :, g * d:(g + 1) * d] = out[:, g * tq:(g + 1) * tq].T


def nsa_slc_branch(qt, B, sel, k_aug, vt, *, tq=512, tk=1024):
    S = qt.shape[2] // B
    Hkv, G, d = NSA_KV_HEADS, NSA_GROUP, NSA_HEAD_DIM
    nblk = sel.shape[2]
    dv = vt.shape[1]
    tq, tk = min(tq, S), min(tk, S)
    assert S % tq == 0 and S % tk == 0 and tk % tq == 0 and tq & (tq - 1) == 0
    assert nblk % LANE == 0 and (LANE * SLC_BLOCK) % tk == 0
    kern = functools.partial(_nsa_slc_kernel, tk=tk)
    return pl.pallas_call(
        kern,
        grid=(B, Hkv, S // tq),
        in_specs=[
            pl.BlockSpec((G, d, tq), lambda b, h, i: (h, 0, b * (S // tq) + i)),
            pl.BlockSpec((1, 1, nblk, tq), lambda b, h, i: (b, h, 0, i)),
            pl.BlockSpec((1, S, d + LANE), lambda b, h, i: (h, b, 0), pipeline_mode=pl.Buffered(1)),
            pl.BlockSpec((1, dv, S), lambda b, h, i: (h, 0, b), pipeline_mode=pl.Buffered(1)),
        ],
        out_specs=pl.BlockSpec((1, tq, G * d), lambda b, h, i: (b, i, h)),
        out_shape=jax.ShapeDtypeStruct((B, S, Hkv * G * d), F32),
        scratch_shapes=[
            pltpu.VMEM((d + LANE, G * tq), MXU_DTYPE),
            pltpu.VMEM((1, G * tq), F32),
            pltpu.VMEM((dv, G * tq), F32),
            pltpu.VMEM((2, tk, G * tq), F32),
            pltpu.VMEM((2, 1, G * tq), F32),
        ],
        compiler_params=_cparams("arbitrary", "arbitrary", "arbitrary"),
        name="nsa_slc",
    )(qt, sel, k_aug, vt)


def _nsa_win_kernel(q_ref, k_ref, vt_ref, bias_a_ref, bias_b_ref, o_ref, qt_ref, s_ref):
    i = pl.program_id(2)
    tq = q_ref.shape[2]
    th = tq // 2
    d = NSA_HEAD_DIM
    G = NSA_GROUP
    nk = WIN + th
    for hf in range(2):
        for g in range(G):
            qt_ref[hf, :, g * th:(g + 1) * th] = q_ref[g, :, hf * th:(hf + 1) * th]

    def start(hf):
        return pl.multiple_of(jnp.maximum(i * tq + hf * th - WIN, 0), th)

    def put_scores(hf, bias_ref):
        s_ref[hf] = (jnp.dot(k_ref[0, pl.ds(start(hf), nk), :], qt_ref[hf], preferred_element_type=F32)
                     + jnp.concatenate([bias_ref[0]] * G, axis=1))

    def finish(hf):
        s = s_ref[hf]
        e = jnp.exp2(s - jnp.max(s, axis=0, keepdims=True))
        p = e * (1.0 / jnp.sum(e, axis=0, keepdims=True))
        out_t = jnp.dot(vt_ref[0, :d, pl.ds(start(hf), nk)], p.astype(MXU_DTYPE), preferred_element_type=F32)
        for g in range(G):
            o_ref[0, hf * th:(hf + 1) * th, g * d:(g + 1) * d] = out_t[:, g * th:(g + 1) * th].T

    put_scores(0, bias_a_ref)
    put_scores(1, bias_b_ref)
    finish(0)
    finish(1)


def nsa_win_branch(qt, B, k_win, vt, *, tq=512):
    S = qt.shape[2] // B
    Hkv, G, d = NSA_KV_HEADS, NSA_GROUP, NSA_HEAD_DIM
    dv = vt.shape[1]
    tq = min(tq, S)
    th = tq // 2
    assert S % tq == 0 and th % LANE == 0 and th & (th - 1) == 0 and WIN % th == 0 and S >= WIN + th
    kern = _nsa_win_kernel
    n_clipped = WIN // th
    t0 = jnp.minimum(jnp.arange(n_clipped + 1) * th, WIN)[:, None, None]
    pos = (t0 - jnp.minimum(t0, WIN)) + jnp.arange(WIN + th)[None, :, None]
    tok = t0 + jnp.arange(th)[None, None, :]
    bias = jnp.where((pos <= tok) & (pos > tok - WIN), 0.0, NEG).astype(F32)
    return pl.pallas_call(
        kern,
        grid=(B, Hkv, S // tq),
        in_specs=[
            pl.BlockSpec((G, d, tq), lambda b, h, i: (h, 0, b * (S // tq) + i)),
            pl.BlockSpec((1, S, d), lambda b, h, i: (h, b, 0)),
            pl.BlockSpec((1, dv, S), lambda b, h, i: (Hkv + h, 0, b)),
            pl.BlockSpec((1, WIN + th, th), lambda b, h, i: (jnp.minimum(2 * i, n_clipped), 0, 0)),
            pl.BlockSpec((1, WIN + th, th), lambda b, h, i: (jnp.minimum(2 * i + 1, n_clipped), 0, 0)),
        ],
        out_specs=pl.BlockSpec((1, tq, G * d), lambda b, h, i: (b, i, h)),
        out_shape=jax.ShapeDtypeStruct((B, S, Hkv * G * d), F32),
        scratch_shapes=[pltpu.VMEM((2, d, G * th), MXU_DTYPE), pltpu.VMEM((2, WIN + th, G * th), F32)],
        compiler_params=_cparams("arbitrary", "arbitrary", "arbitrary"),
        name="nsa_win",
    )(qt, k_win, vt, bias, bias)


def _nsa_out_kernel(oc_ref, os_ref, ow_ref, gate_ref, w_ref, h_ref, o_ref, merged_ref):
    d = NSA_HEAD_DIM
    gates = _sigmoid(gate_ref[...])
    for hq in range(NSA_Q_HEADS):
        cols = slice(hq * d, (hq + 1) * d)
        c = N_BRANCH * hq
        merged = (gates[:, c:c + 1] * oc_ref[:, cols] + gates[:, c + 1:c + 2] * os_ref[:, cols]
                  + gates[:, c + 2:c + 3] * ow_ref[:, cols])
        merged_ref[:, cols] = merged.astype(merged_ref.dtype)
    o_ref[...] = h_ref[...] + jnp.dot(merged_ref[...], w_ref[...], preferred_element_type=F32)


def nsa_out(o_cmp, o_slc, o_win, gate_logits, w_o, h, *, tm=256):
    T, HD = o_cmp.shape
    D = w_o.shape[1]
    tm = min(tm, T)
    assert T % tm == 0 and gate_logits.shape == (T, LANE)
    row = lambda i: (i, 0)
    return pl.pallas_call(
        _nsa_out_kernel,
        grid=(T // tm,),
        in_specs=[
            pl.BlockSpec((tm, HD), row),
            pl.BlockSpec((tm, HD), row),
            pl.BlockSpec((tm, HD), row),
            pl.BlockSpec((tm, LANE), row),
            pl.BlockSpec((HD, D), lambda i: (0, 0)),
            pl.BlockSpec((tm, D), row),
        ],
        out_specs=pl.BlockSpec((tm, D), row),
        out_shape=jax.ShapeDtypeStruct((T, D), F32),
        scratch_shapes=[pltpu.VMEM((tm, HD), w_o.dtype)],
        compiler_params=_cparams("arbitrary"),
        name="nsa_out",
    )(o_cmp, o_slc, o_win, gate_logits, w_o, h)


def _span_major(x, nsel):
    B, n, H, d = x.shape
    qsz = nsel // CMP_SPANS
    return x.reshape(B, CMP_SPANS, qsz, SLC_RATIO, H, d).transpose(0, 4, 1, 3, 2, 5).reshape(B, H, n, d)


def _cast(w):
    return w.astype(MXU_DTYPE)


def nsa_shared_kv(h, B, S, kv_gain, w_kv, cmp_pe_k, cmp_w1_k, cmp_w2_k, cmp_pe_v, cmp_w1_v, cmp_w2_v):
    Hkv, d = NSA_KV_HEADS, NSA_HEAD_DIM
    cmp_in, k_slc, k_win, vt = kv_projection(h, S, kv_gain, _cast(w_kv))
    n = S // CMP_STRIDE
    nsel = S // SLC_BLOCK
    cmp = compress_tokens(cmp_in, B, jnp.stack([cmp_pe_k, cmp_pe_v]), _cast(jnp.stack([cmp_w1_k, cmp_w1_v])),
                          _cast(jnp.stack([cmp_w2_k, cmp_w2_v])))
    cmp = cmp.reshape(2, B, Hkv, n, d).transpose(0, 1, 3, 2, 4)
    kc = _cast(_span_major(cmp[0], nsel))
    vct = _cast(_span_major(cmp[1], nsel).transpose(0, 1, 3, 2))
    return kc, vct, k_slc, k_win, vt


def nsa_attention_block(h, B, S, mix_gain, w_q, w_o, shared):
    kc, vct, k_slc, k_win, vt = shared
    T = B * S
    nq = NSA_Q_HEADS * NSA_HEAD_DIM
    nqp = nq + LANE
    w_q = jnp.pad(w_q, ((0, 0), (0, nqp - w_q.shape[1])))
    qt, gate_logits = q_projection(h, mix_gain, _cast(w_q))
    o_cmp, sel = nsa_cmp_branch(qt, B, kc, vct)
    o_slc = nsa_slc_branch(qt, B, sel, k_slc, vt)
    o_win = nsa_win_branch(qt, B, k_win, vt)
    return nsa_out(o_cmp.reshape(T, nq), o_slc.reshape(T, nq), o_win.reshape(T, nq), gate_logits, _cast(w_o), h)


def nsa_mixer(h, B, S, mix_gain, kv_gain, w_kv, cmp_pe_k, cmp_w1_k, cmp_w2_k, cmp_pe_v, cmp_w1_v, cmp_w2_v, w_q, w_o):
    shared = nsa_shared_kv(h, B, S, kv_gain, w_kv, cmp_pe_k, cmp_w1_k, cmp_w2_k, cmp_pe_v, cmp_w1_v, cmp_w2_v)
    return nsa_attention_block(h, B, S, mix_gain, w_q, w_o, shared)


def kernel(x, norm_mix_gain, norm_ffn_gain, ret_w_in, ret_gn_gain, ret_w_out, nsa_kv_norm_gain, nsa_w_kv, cmp_pe_k, cmp_w1_k, cmp_w2_k, cmp_pe_v, cmp_w1_v, cmp_w2_v, nsa_w_q, nsa_w_o, ffn_w_in, ffn_conv_w, ffn_conv_b, ffn_w_out, final_norm_gain):
    B, S, D = x.shape
    T = B * S
    depth = norm_mix_gain.shape[0]
    n_a = depth // 2
    h = x.reshape(T, D)
    shared = None
    for layer in range(depth):
        if layer < n_a:
            proj = norm_matmul(h, norm_mix_gain[layer], _cast(ret_w_in[layer]), tm=1024, tn=1024)
            y = retention_core(proj.reshape(B, S, -1), ret_gn_gain[layer])
            h = matmul_residual(y.reshape(T, -1), _cast(ret_w_out[layer]), h, tm=512, tn=1024)
        else:
            if layer == n_a:
                shared = nsa_shared_kv(h, B, S, nsa_kv_norm_gain, nsa_w_kv, cmp_pe_k, cmp_w1_k, cmp_w2_k,
                                       cmp_pe_v, cmp_w1_v, cmp_w2_v)
            b = layer - n_a
            h = nsa_attention_block(h, B, S, norm_mix_gain[layer], nsa_w_q[b], nsa_w_o[b], shared)
        h = conv_ffn_block(h, S, norm_ffn_gain[layer], _cast(ffn_w_in[layer]), ffn_conv_w[layer], ffn_conv_b[layer],
                           _cast(ffn_w_out[layer]), final_norm_gain, final_norm=(layer == depth - 1))
    return h.reshape(B, S, D)
```
